```python
import math
import jax
import jax.numpy as jnp
from jax import lax
import numpy as np

D_MODEL = 2048
BATCH = 8
SEQ = 4096
DEPTH = 4

CHUNK = 64
Q_BLOCK = 128
NORM_EPS = 1e-6
ROPE_THETA = 10000.0

FOX_HEADS = 6
FOX_DH = 128
FOX_W = FOX_HEADS * FOX_DH
FORGET_BIAS_CENTER = 3.0

MLA_HEADS = 6
MLA_NOPE = 128
MLA_ROPE = 64
MLA_V = 128
MLA_Q_LORA = 512
MLA_KV_LORA = 256
MLA_W = MLA_HEADS * MLA_V

RET_HEADS = 4
RET_DK = 128
RET_DV = 256
RET_QK_W = RET_HEADS * RET_DK
RET_V_W = RET_HEADS * RET_DV

N_BRANCH = 3

D_FF = 5632
CONV_W = 3

IN_SPLITS = (FOX_W, FOX_W, FOX_W, FOX_HEADS,
             MLA_Q_LORA, MLA_KV_LORA, MLA_ROPE,
             RET_QK_W, RET_QK_W, RET_V_W, RET_V_W,
             N_BRANCH * D_MODEL)
IN_WIDTH = sum(IN_SPLITS)

kernel_name = 'hybrid_fox_mla_retention_convffn'


def rms_norm(x, g):
    xf = x.astype(jnp.float32)
    y = xf * lax.rsqrt(jnp.mean(xf * xf, axis=-1, keepdims=True) + NORM_EPS)
    return (y * g.astype(jnp.float32)).astype(x.dtype)


def apply_rope(x):
    s, d = x.shape[1], x.shape[-1]
    pos = jnp.arange(s, dtype=jnp.float32)
    inv_freq = ROPE_THETA ** (-jnp.arange(0, d, 2, dtype=jnp.float32) / d)
    ang = pos[:, None] * inv_freq[None, :]
    cos = jnp.cos(ang)[None, :, None, :]
    sin = jnp.sin(ang)[None, :, None, :]
    xf = x.astype(jnp.float32)
    x1, x2 = xf[..., : d // 2], xf[..., d // 2:]
    return jnp.concatenate([x1 * cos - x2 * sin, x2 * cos + x1 * sin], axis=-1).astype(x.dtype)


def block_attention(q, k, v, scale, frame_causal, log_decay_cum):
    s_len = q.shape[2]
    outs = []
    for s0 in range(0, s_len, Q_BLOCK):
        s1 = s0 + Q_BLOCK
        logits = jnp.einsum('bhqd,bhkd->bhqk', q[:, :, s0:s1], k[:, :, :s1]).astype(jnp.float32) * scale
        if log_decay_cum is not None:
            logits = logits + log_decay_cum[:, :, s0:s1, None] - log_decay_cum[:, :, None, :s1]
        q_pos = jnp.arange(s0, s1)
        k_pos = jnp.arange(s1)
        if frame_causal:
            mask = k_pos[None, :] <= q_pos[:, None]
        else:
            mask = (k_pos // CHUNK)[None, :] <= (q_pos // CHUNK)[:, None]
        p = jax.nn.softmax(jnp.where(mask, logits, -jnp.inf), axis=-1).astype(v.dtype)
        outs.append(jnp.einsum('bhqk,bhkd->bhqd', p, v[:, :, :s1]))
    return jnp.concatenate(outs, axis=2)


def fox_mixer(q, k, v, f_logit, b_f):
    b, s = q.shape[:2]
    def heads(t):
        return t.reshape(b, s, FOX_HEADS, FOX_DH).transpose(0, 2, 1, 3)
    log_f = jax.nn.log_sigmoid(f_logit.astype(jnp.float32) + b_f.astype(jnp.float32))
    c = jnp.cumsum(log_f, axis=1).transpose(0, 2, 1)
    o = block_attention(heads(q), heads(k), heads(v), FOX_DH ** -0.5, True, c)
    return o.transpose(0, 2, 1, 3).reshape(b, s, FOX_W)


def mla_mixer(c_q, c_kv, k_rope, q_norm_g, kv_norm_g, w_uq, w_ukv):
    b, s = c_q.shape[:2]
    q = (rms_norm(c_q, q_norm_g) @ w_uq).reshape(b, s, MLA_HEADS, MLA_NOPE + MLA_ROPE)
    q = jnp.concatenate([q[..., :MLA_NOPE], apply_rope(q[..., MLA_NOPE:])], axis=-1)
    kv = (rms_norm(c_kv, kv_norm_g) @ w_ukv).reshape(b, s, MLA_HEADS, MLA_NOPE + MLA_V)
    k_nope, v = kv[..., :MLA_NOPE], kv[..., MLA_NOPE:]
    k_r = apply_rope(k_rope[:, :, None, :])
    k = jnp.concatenate([k_nope, jnp.broadcast_to(k_r, (b, s, MLA_HEADS, MLA_ROPE))], axis=-1)
    o = block_attention(q.transpose(0, 2, 1, 3), k.transpose(0, 2, 1, 3), v.transpose(0, 2, 1, 3),
                        (MLA_NOPE + MLA_ROPE) ** -0.5, False, None)
    return o.transpose(0, 2, 1, 3).reshape(b, s, MLA_W)


def retention_mixer(q, k, v, g):
    b, s = q.shape[:2]
    n = s // CHUNK
    dt = v.dtype
    q = apply_rope(q.reshape(b, s, RET_HEADS, RET_DK))
    k = apply_rope(k.reshape(b, s, RET_HEADS, RET_DK)) * (RET_DK ** -0.5)
    qc = q.reshape(b, n, CHUNK, RET_HEADS, RET_DK)
    kc = k.reshape(b, n, CHUNK, RET_HEADS, RET_DK)
    vc = v.reshape(b, n, CHUNK, RET_HEADS, RET_DV)
    log_gamma = jnp.log(1.0 - 2.0 ** (-5.0 - jnp.arange(RET_HEADS, dtype=jnp.float32)))
    idx = jnp.arange(CHUNK, dtype=jnp.float32)
    intra_decay = jnp.exp(log_gamma[:, None, None] * jnp.abs(idx[:, None] - idx[None, :]))
    state_in = jnp.exp(log_gamma[:, None] * (CHUNK - 1 - idx)[None, :])
    cross_decay = jnp.exp(log_gamma[:, None] * (idx + 1.0)[None, :])
    chunk_decay = jnp.exp(log_gamma * CHUNK)
    scores = jnp.einsum('bnjhd,bnlhd->bnhjl', qc, kc) * intra_decay.astype(dt)
    intra = jnp.einsum('bnhjl,bnlhe->bnjhe', scores, vc)
    kv_chunk = jnp.einsum('bnlhd,hl,bnlhe->nbhde', kc, state_in.astype(dt), vc).astype(jnp.float32)
    def step(state, kv_n):
        return chunk_decay[None, :, None, None] * state + kv_n, state
    _, prev = lax.scan(step, jnp.zeros((b, RET_HEADS, RET_DK, RET_DV), jnp.float32), kv_chunk)
    cross = jnp.einsum('bnjhd,nbhde->bnjhe', qc, prev.astype(dt)) * cross_decay.T[None, None, :, :, None].astype(dt)
    o = (intra + cross).reshape(b, s, RET_HEADS, RET_DV).astype(jnp.float32)
    o = o * lax.rsqrt(jnp.mean(o * o, axis=-1, keepdims=True) + NORM_EPS)
    return o.reshape(b, s, RET_V_W).astype(dt) * jax.nn.silu(g)


def conv_ffn(h, w_up, w_gate, conv_w, conv_b, w_down):
    s = h.shape[1]
    u = h @ w_up
    u_pad = jnp.pad(u, ((0, 0), (CONV_W - 1, 0), (0, 0)))
    u_conv = conv_b + sum(conv_w[i] * u_pad[:, i:i + s] for i in range(CONV_W))
    return (jax.nn.gelu(u_conv) * (h @ w_gate)) @ w_down


def _normal(key, shape, fan_in):
    return jax.random.normal(key, shape, jnp.float32) * (fan_in ** -0.5)


def _gain(key, shape):
    return 1.0 + 0.02 * jax.random.normal(key, shape, jnp.float32)


def setup_inputs(seed: int = 0) -> dict:
    key = jax.random.key(seed)
    ks = jax.random.split(key, 20)
    L, D = DEPTH, D_MODEL
    return {
        'x': jax.random.normal(ks[0], (BATCH, SEQ, D), jnp.float32),
        'norm1_g': _gain(ks[1], (L, D)),
        'w_in': _normal(ks[2], (L, D, IN_WIDTH), D),
        'mla_q_norm_g': _gain(ks[3], (L, MLA_Q_LORA)),
        'mla_kv_norm_g': _gain(ks[4], (L, MLA_KV_LORA)),
        'mla_w_uq': _normal(ks[5], (L, MLA_Q_LORA, MLA_HEADS * (MLA_NOPE + MLA_ROPE)), MLA_Q_LORA),
        'mla_w_ukv': _normal(ks[6], (L, MLA_KV_LORA, MLA_HEADS * (MLA_NOPE + MLA_V)), MLA_KV_LORA),
        'fox_b_f': FORGET_BIAS_CENTER + 0.1 * jax.random.normal(ks[7], (L, FOX_HEADS), jnp.float32),
        'w_br_fox': _normal(ks[8], (L, FOX_W, D), FOX_W),
        'w_br_mla': _normal(ks[9], (L, MLA_W, D), MLA_W),
        'w_br_ret': _normal(ks[10], (L, RET_V_W, D), RET_V_W),
        'w_out': _normal(ks[11], (L, D, D), D),
        'norm2_g': _gain(ks[12], (L, D)),
        'ffn_w_up': _normal(ks[13], (L, D, D_FF), D),
        'ffn_w_gate': _normal(ks[14], (L, D, D_FF), D),
        'ffn_conv_w': _normal(ks[15], (L, CONV_W, D_FF), CONV_W),
        'ffn_conv_b': 0.02 * jax.random.normal(ks[16], (L, D_FF), jnp.float32),
        'ffn_w_down': _normal(ks[17], (L, D_FF, D), D_FF),
        'final_norm_g': _gain(ks[18], (D,)),
    }


def reference(x, norm1_g, w_in, mla_q_norm_g, mla_kv_norm_g, mla_w_uq, mla_w_ukv, fox_b_f,
              w_br_fox, w_br_mla, w_br_ret, w_out, norm2_g, ffn_w_up, ffn_w_gate,
              ffn_conv_w, ffn_conv_b, ffn_w_down, final_norm_g):
    b, s, d = x.shape
    split_points = [int(p) for p in np.cumsum(IN_SPLITS)[:-1]]
    for i in range(DEPTH):
        h = rms_norm(x, norm1_g[i])
        (fq, fk, fv, ff, mq, mkv, mkr, rq, rk, rv, rg, gates) = jnp.split(h @ w_in[i], split_points, axis=-1)
        a = fox_mixer(fq, fk, fv, ff, fox_b_f[i])
        bm = mla_mixer(mq, mkv, mkr, mla_q_norm_g[i], mla_kv_norm_g[i], mla_w_uq[i], mla_w_ukv[i])
        c = retention_mixer(rq, rk, rv, rg)
        g = jax.nn.sigmoid(gates.astype(jnp.float32)).astype(x.dtype).reshape(b, s, N_BRANCH, d)
        merged = (g[:, :, 0] * (a @ w_br_fox[i])
                  + g[:, :, 1] * (bm @ w_br_mla[i])
                  + g[:, :, 2] * (c @ w_br_ret[i]))
        x = x + merged @ w_out[i]
        x = x + conv_ffn(rms_norm(x, norm2_g[i]), ffn_w_up[i], ffn_w_gate[i],
                         ffn_conv_w[i], ffn_conv_b[i], ffn_w_down[i])
    return rms_norm(x, final_norm_g)
```

```python
import functools
import math

import jax
import jax.numpy as jnp
from jax import lax
from jax.experimental import pallas as pl
from jax.experimental.pallas import tpu as pltpu

F32 = jnp.float32
BF16 = jnp.bfloat16

NORM_EPS = 1e-6
ROPE_THETA = 10000.0
CHUNK = 64

FOX_HEADS, FOX_DH = 6, 128
MLA_HEADS, MLA_NOPE, MLA_ROPE, MLA_V = 6, 128, 64, 128
MLA_QK_PAD = 256
RET_HEADS, RET_DK, RET_DV = 4, 128, 256
CONV_W = 3
LANES = 128
HALO = 16

FOX_W = FOX_HEADS * FOX_DH
MLA_W = MLA_HEADS * MLA_V
RET_QK_W = RET_HEADS * RET_DK
RET_V_W = RET_HEADS * RET_DV

OFF_FQ, OFF_FK, OFF_FV = 0, FOX_W, 2 * FOX_W
OFF_MKV = 3 * FOX_W
OFF_MQ = OFF_MKV + 256
OFF_RV = OFF_MQ + 512
OFF_RG = OFF_RV + RET_V_W
OFF_RQ = OFF_RG + RET_V_W
OFF_RK = OFF_RQ + RET_QK_W
OFF_MKR = OFF_RK + RET_QK_W
OFF_MKRR = OFF_MKR + LANES
P_WIDTH = OFF_MKRR + LANES

VMEM_LIMIT_BYTES = 56 * 1024 * 1024


def _tile(n, pref, align):
    t = (min(pref, n) // align) * align
    while t > align and n % t:
        t -= align
    assert t >= align and n % t == 0, (n, pref, align)
    return t


def _params(*sem):
    return pltpu.CompilerParams(dimension_semantics=sem, vmem_limit_bytes=VMEM_LIMIT_BYTES)


def _rms(x, g):
    return x * lax.rsqrt(jnp.mean(x * x, axis=-1, keepdims=True) + NORM_EPS) * g


def _norm_rows(x_ref, g_ref, h_ref, rows, h_off=0):
    rc = _tile(rows, 256, 8)

    def body(c, carry):
        r0 = pl.multiple_of(c * rc, rc)
        h_ref[pl.ds(h_off + r0, rc), :] = _rms(x_ref[pl.ds(r0, rc), :], g_ref[...]).astype(BF16)
        return carry

    lax.fori_loop(0, rows // rc, body, 0)


def _norm_mm_kernel(x_ref, g_ref, w_ref, o_ref, h_ref, *, sigmoid):
    @pl.when(pl.program_id(1) == 0)
    def _():
        _norm_rows(x_ref, g_ref, h_ref, x_ref.shape[0])

    y = jnp.dot(h_ref[...], w_ref[...], preferred_element_type=F32)
    if sigmoid:
        y = jax.nn.sigmoid(y)
    o_ref[...] = y.astype(o_ref.dtype)


def _norm_matmul(x, g, w, out_dtype, sigmoid, tm_pref, tn_pref):
    t, d = x.shape
    n = w.shape[1]
    tm = _tile(t, tm_pref, 8)
    tn = _tile(n, tn_pref, LANES)
    return pl.pallas_call(
        functools.partial(_norm_mm_kernel, sigmoid=sigmoid),
        grid=(t // tm, n // tn),
        in_specs=[pl.BlockSpec((tm, d), lambda i, j: (i, 0)),
                  pl.BlockSpec((1, d), lambda i, j: (0, 0)),
                  pl.BlockSpec((d, tn), lambda i, j: (0, j))],
        out_specs=pl.BlockSpec((tm, tn), lambda i, j: (i, j)),
        out_shape=jax.ShapeDtypeStruct((t, n), out_dtype),
        scratch_shapes=[pltpu.VMEM((tm, d), BF16)],
        compiler_params=_params("parallel", "arbitrary"),
        name="in_proj_sigmoid" if sigmoid else "in_proj",
    )(x, g, w)


def _fox_gate_kernel(x_ref, g_ref, w_ref, b_ref, tri_ref, o_ref, h_ref, carry_ref):
    @pl.when(pl.program_id(1) == 0)
    def _():
        carry_ref[...] = jnp.zeros_like(carry_ref)

    _norm_rows(x_ref, g_ref, h_ref, x_ref.shape[0])
    f = jnp.dot(h_ref[...], w_ref[...], preferred_element_type=F32) + b_ref[...]
    lf = jnp.minimum(f, 0.0) - jnp.log1p(jnp.exp(-jnp.abs(f)))
    hi = lf.astype(BF16)
    r1 = lf - hi.astype(F32)
    mid = r1.astype(BF16)
    lo = (r1 - mid.astype(F32)).astype(BF16)
    tri = tri_ref[...]
    c = (jnp.dot(tri, hi, preferred_element_type=F32)
         + jnp.dot(tri, mid, preferred_element_type=F32)
         + jnp.dot(tri, lo, preferred_element_type=F32)) + carry_ref[0:1, :]
    o_ref[...] = c
    carry_ref[...] = jnp.broadcast_to(c[-1:, :], carry_ref.shape)


def _fox_gate(x, g, w_ff, b_ff, batch, seq):
    t, d = x.shape
    ts = _tile(seq, 512, 8)
    nb = seq // ts
    tri = (jnp.arange(ts)[:, None] >= jnp.arange(ts)[None, :]).astype(BF16)
    return pl.pallas_call(
        _fox_gate_kernel,
        grid=(batch, nb),
        in_specs=[pl.BlockSpec((ts, d), lambda b, n: (b * nb + n, 0)),
                  pl.BlockSpec((1, d), lambda b, n: (0, 0)),
                  pl.BlockSpec((d, LANES), lambda b, n: (0, 0)),
                  pl.BlockSpec((1, LANES), lambda b, n: (0, 0)),
                  pl.BlockSpec((ts, ts), lambda b, n: (0, 0))],
        out_specs=pl.BlockSpec((ts, LANES), lambda b, n: (b * nb + n, 0)),
        out_shape=jax.ShapeDtypeStruct((t, LANES), F32),
        scratch_shapes=[pltpu.VMEM((ts, d), BF16), pltpu.VMEM((8, LANES), F32)],
        compiler_params=_params("parallel", "arbitrary"),
        name="fox_gate_cumsum",
    )(x, g, w_ff, b_ff, tri)


def _mla_prep_kernel(mq_ref, mkv_ref, kr_ref, krr_ref, gq_ref, gkv_ref, wq_ref, wqr_ref,
                     wk_ref, wv_ref, cos_ref, sin_ref, q_ref, k_ref, v_ref):
    cos = cos_ref[...]
    sin = sin_ref[...]
    cqn = _rms(mq_ref[...].astype(F32), gq_ref[...]).astype(BF16)
    q = jnp.dot(cqn, wq_ref[...], preferred_element_type=F32)
    qrot = jnp.dot(cqn, wqr_ref[...], preferred_element_type=F32)
    ckvn = _rms(mkv_ref[...].astype(F32), gkv_ref[...]).astype(BF16)
    kn = jnp.dot(ckvn, wk_ref[...], preferred_element_type=F32)
    v_ref[...] = jnp.dot(ckvn, wv_ref[...], preferred_element_type=F32).astype(BF16)
    kr = (kr_ref[...].astype(F32) * cos + krr_ref[...].astype(F32) * sin).astype(BF16)
    for h in range(MLA_HEADS):
        lo = h * MLA_QK_PAD
        q_ref[:, lo:lo + LANES] = q[:, lo:lo + LANES].astype(BF16)
        q_ref[:, lo + LANES:lo + 2 * LANES] = (
            q[:, lo + LANES:lo + 2 * LANES] * cos + qrot[:, h * LANES:(h + 1) * LANES] * sin).astype(BF16)
        k_ref[:, lo:lo + LANES] = kn[:, h * LANES:(h + 1) * LANES].astype(BF16)
        k_ref[:, lo + LANES:lo + 2 * LANES] = kr


def _mla_prep(p, gq, gkv, wq, wqr, wk, wv, cos, sin, seq):
    t = p.shape[0]
    tm = _tile(seq, 512, 8)
    ns = seq // tm
    ql, kvl = wq.shape[0], wk.shape[0]
    row = lambda i: (i, 0)
    const = lambda i: (0, 0)
    return pl.pallas_call(
        _mla_prep_kernel,
        grid=(t // tm,),
        in_specs=[pl.BlockSpec((tm, ql), lambda i: (i, OFF_MQ // ql)),
                  pl.BlockSpec((tm, kvl), lambda i: (i, OFF_MKV // kvl)),
                  pl.BlockSpec((tm, LANES), lambda i: (i, OFF_MKR // LANES)),
                  pl.BlockSpec((tm, LANES), lambda i: (i, OFF_MKRR // LANES)),
                  pl.BlockSpec((1, ql), const),
                  pl.BlockSpec((1, kvl), const),
                  pl.BlockSpec(wq.shape, const),
                  pl.BlockSpec(wqr.shape, const),
                  pl.BlockSpec(wk.shape, const),
                  pl.BlockSpec(wv.shape, const),
                  pl.BlockSpec((tm, LANES), lambda i: (i % ns, 0)),
                  pl.BlockSpec((tm, LANES), lambda i: (i % ns, 0))],
        out_specs=[pl.BlockSpec((tm, MLA_HEADS * MLA_QK_PAD), row),
                   pl.BlockSpec((tm, MLA_HEADS * MLA_QK_PAD), row),
                   pl.BlockSpec((tm, MLA_W), row)],
        out_shape=[jax.ShapeDtypeStruct((t, MLA_HEADS * MLA_QK_PAD), BF16),
                   jax.ShapeDtypeStruct((t, MLA_HEADS * MLA_QK_PAD), BF16),
                   jax.ShapeDtypeStruct((t, MLA_W), BF16)],
        compiler_params=_params("parallel"),
        name="mla_prep",
    )(p, p, p, p, gq, gkv, wq, wqr, wk, wv, cos, sin)


def _attn_kernel(qi_tab, ki_tab, *refs, heads, dq, dv, scale, has_bias, tq, tk):
    if has_bias:
        q_ref, k_ref, v_ref, rel_ref, bias_ref, o_ref, m_ref, l_ref, acc_ref = refs
    else:
        q_ref, k_ref, v_ref, rel_ref, o_ref, m_ref, l_ref, acc_ref = refs
    step = pl.program_id(1)
    qi = qi_tab[step]
    ki = ki_tab[step]

    @pl.when(ki == 0)
    def _():
        m_ref[...] = jnp.full_like(m_ref, -jnp.inf)
        l_ref[...] = jnp.zeros_like(l_ref)
        acc_ref[...] = jnp.zeros_like(acc_ref)

    visible = rel_ref[...] >= (ki * tk - qi * tq)
    for h in range(heads):
        q = q_ref[:, h * dq:(h + 1) * dq]
        k = k_ref[:, h * dq:(h + 1) * dq]
        s = lax.dot_general(q, k, (((1,), (1,)), ((), ())), preferred_element_type=F32) * scale
        if has_bias:
            s = s + bias_ref[0, h]
        s = jnp.where(visible, s, -jnp.inf)
        m_prev = m_ref[h]
        m_new = jnp.maximum(m_prev, jnp.max(s, axis=-1, keepdims=True))
        alpha = jnp.exp(m_prev - m_new)
        p = jnp.exp(s - m_new)
        l_ref[h] = alpha * l_ref[h] + jnp.sum(p, axis=-1, keepdims=True)
        pv = jnp.dot(p.astype(BF16), v_ref[:, h * dv:(h + 1) * dv], preferred_element_type=F32)
        acc_ref[:, h * dv:(h + 1) * dv] = alpha * acc_ref[:, h * dv:(h + 1) * dv] + pv
        m_ref[h] = m_new

    @pl.when(ki == qi)
    def _():
        for h in range(heads):
            o_ref[:, h * dv:(h + 1) * dv] = (
                acc_ref[:, h * dv:(h + 1) * dv] / l_ref[h]).astype(o_ref.dtype)


def _attention(q_arr, q_cb, k_arr, k_cb, v_arr, v_cb, rel, bias, *, batch, seq, heads, dq, dv, scale, name):
    tq = tk = rel.shape[0]
    nq = seq // tq
    pairs = [(qi, ki) for qi in range(nq) for ki in range(qi + 1)]
    qi_tab = jnp.asarray([p[0] for p in pairs], jnp.int32)
    ki_tab = jnp.asarray([p[1] for p in pairs], jnp.int32)
    has_bias = bias is not None
    in_specs = [pl.BlockSpec((tq, heads * dq), lambda b, s, qt, kt: (b * nq + qt[s], q_cb)),
                pl.BlockSpec((tk, heads * dq), lambda b, s, qt, kt: (b * nq + kt[s], k_cb)),
                pl.BlockSpec((tk, heads * dv), lambda b, s, qt, kt: (b * nq + kt[s], v_cb)),
                pl.BlockSpec((tq, tk), lambda b, s, qt, kt: (0, 0))]
    args = [q_arr, k_arr, v_arr, rel]
    if has_bias:
        in_specs.append(pl.BlockSpec((1, heads, 1, tk), lambda b, s, qt, kt: (b, 0, 0, kt[s])))
        args.append(bias)
    return pl.pallas_call(
        functools.partial(_attn_kernel, heads=heads, dq=dq, dv=dv, scale=scale,
                          has_bias=has_bias, tq=tq, tk=tk),
        grid_spec=pltpu.PrefetchScalarGridSpec(
            num_scalar_prefetch=2,
            grid=(batch, len(pairs)),
            in_specs=in_specs,
            out_specs=pl.BlockSpec((tq, heads * dv), lambda b, s, qt, kt: (b * nq + qt[s], 0)),
            scratch_shapes=[pltpu.VMEM((heads, tq, 1), F32),
                            pltpu.VMEM((heads, tq, 1), F32),
                            pltpu.VMEM((tq, heads * dv), F32)]),
        out_shape=jax.ShapeDtypeStruct((batch * seq, heads * dv), BF16),
        compiler_params=_params("parallel", "arbitrary"),
        name=name,
    )(qi_tab, ki_tab, *args)


def _retention_kernel(q_ref, k_ref, v_ref, g_ref, cos_ref, sin_ref, dec_ref, sin_state_ref,
                      cross_ref, o_ref, state_ref, *, block_decay):
    @pl.when(pl.program_id(1) == 0)
    def _():
        state_ref[...] = jnp.zeros_like(state_ref)

    cos = cos_ref[...]
    sin = sin_ref[...]

    def rope(x):
        return x * cos + pltpu.roll(x, RET_DK // 2, axis=1) * sin

    for h in range(RET_HEADS):
        qs = slice(h * RET_DK, (h + 1) * RET_DK)
        vs = slice(h * RET_DV, (h + 1) * RET_DV)
        q = rope(q_ref[:, qs].astype(F32)).astype(BF16)
        kf = rope(k_ref[:, qs].astype(F32)) * (RET_DK ** -0.5)
        v = v_ref[:, vs]
        scores = lax.dot_general(q, kf.astype(BF16), (((1,), (1,)), ((), ())),
                                 preferred_element_type=F32) * dec_ref[h]
        o = jnp.dot(scores.astype(BF16), v, preferred_element_type=F32)
        state = state_ref[h]
        cross = jnp.dot(q, state.astype(BF16), preferred_element_type=F32)
        cd = cross_ref[h]
        o = o + cross * jnp.concatenate([cd, cd], axis=1)
        o = o * lax.rsqrt(jnp.mean(o * o, axis=-1, keepdims=True) + NORM_EPS)
        g = g_ref[:, vs].astype(F32)
        o_ref[:, vs] = (o * (g * jax.nn.sigmoid(g))).astype(o_ref.dtype)
        ks = (kf * sin_state_ref[h]).astype(BF16)
        kv = lax.dot_general(ks, v, (((0,), (0,)), ((), ())), preferred_element_type=F32)
        state_ref[h] = block_decay[h] * state + kv


def _retention(p, cos, sin, dec, sin_state, cross, block_decay, batch, seq):
    blk = dec.shape[1]
    nb = seq // blk
    row = lambda b, n: (b * nb + n, 0)
    const3 = lambda b, n: (0, 0, 0)
    return pl.pallas_call(
        functools.partial(_retention_kernel, block_decay=block_decay),
        grid=(batch, nb),
        in_specs=[pl.BlockSpec((blk, RET_QK_W), lambda b, n: (b * nb + n, OFF_RQ // RET_QK_W)),
                  pl.BlockSpec((blk, RET_QK_W), lambda b, n: (b * nb + n, OFF_RK // RET_QK_W)),
                  pl.BlockSpec((blk, RET_V_W), lambda b, n: (b * nb + n, OFF_RV // RET_V_W)),
                  pl.BlockSpec((blk, RET_V_W), lambda b, n: (b * nb + n, OFF_RG // RET_V_W)),
                  pl.BlockSpec((blk, RET_DK), lambda b, n: (n, 0)),
                  pl.BlockSpec((blk, RET_DK), lambda b, n: (n, 0)),
                  pl.BlockSpec(dec.shape, const3),
                  pl.BlockSpec(sin_state.shape, const3),
                  pl.BlockSpec(cross.shape, const3)],
        out_specs=pl.BlockSpec((blk, RET_V_W), row),
        out_shape=jax.ShapeDtypeStruct((batch * seq, RET_V_W), BF16),
        scratch_shapes=[pltpu.VMEM((RET_HEADS, RET_DK, RET_DV), F32)],
        compiler_params=_params("parallel", "arbitrary"),
        name="retention",
    )(p, p, p, p, cos, sin, dec, sin_state, cross)


def _merge_kernel(a_ref, b_ref, c_ref, g0_ref, g1_ref, g2_ref, wa_ref, wb_ref, wc_ref, o_ref):
    m = g0_ref[...] * jnp.dot(a_ref[...], wa_ref[...], preferred_element_type=F32)
    m = m + g1_ref[...] * jnp.dot(b_ref[...], wb_ref[...], preferred_element_type=F32)
    m = m + g2_ref[...] * jnp.dot(c_ref[...], wc_ref[...], preferred_element_type=F32)
    o_ref[...] = m.astype(o_ref.dtype)


def _merge(a, bm, c, gates, wa, wb, wc):
    t = a.shape[0]
    d = wa.shape[1]
    tm = _tile(t, 512, 8)
    tn = _tile(d, 1024, LANES)
    nj = d // tn
    act = lambda w: pl.BlockSpec((tm, w), lambda i, j: (i, 0))
    wgt = lambda k: pl.BlockSpec((k, tn), lambda i, j: (0, j))
    gate = lambda n: pl.BlockSpec((tm, tn), lambda i, j: (i, n * nj + j))
    return pl.pallas_call(
        _merge_kernel,
        grid=(t // tm, nj),
        in_specs=[act(a.shape[1]), act(bm.shape[1]), act(c.shape[1]), gate(0), gate(1), gate(2),
                  wgt(wa.shape[0]), wgt(wb.shape[0]), wgt(wc.shape[0])],
        out_specs=pl.BlockSpec((tm, tn), lambda i, j: (i, j)),
        out_shape=jax.ShapeDtypeStruct((t, d), BF16),
        compiler_params=_params("parallel", "arbitrary"),
        name="branch_merge",
    )(a, bm, c, gates, gates, gates, wa, wb, wc)


def _mm_res_kernel(a_ref, w_ref, r_ref, o_ref):
    o_ref[...] = r_ref[...] + jnp.dot(a_ref[...], w_ref[...], preferred_element_type=F32)


def _matmul_residual(a, w, r, tm_pref, tn_pref, name):
    t, k = a.shape
    n = w.shape[1]
    tm = _tile(t, tm_pref, 8)
    tn = _tile(n, tn_pref, LANES)
    return pl.pallas_call(
        _mm_res_kernel,
        grid=(n // tn, t // tm),
        in_specs=[pl.BlockSpec((tm, k), lambda j, i: (i, 0)),
                  pl.BlockSpec((k, tn), lambda j, i: (0, j)),
                  pl.BlockSpec((tm, tn), lambda j, i: (i, j))],
        out_specs=pl.BlockSpec((tm, tn), lambda j, i: (i, j)),
        out_shape=jax.ShapeDtypeStruct((t, n), F32),
        compiler_params=_params("parallel", "arbitrary"),
        name=name,
    )(a, w, r)


def _ffn_up_kernel(x_ref, xh_ref, g_ref, wu_ref, wg_ref, cw_ref, cb_ref, o_ref, h_ref, *, seq):
    tm = x_ref.shape[0]

    @pl.when(pl.program_id(1) == 0)
    def _():
        keep = ((pl.program_id(0) * tm) % seq != 0).astype(F32)
        h_ref[0:HALO, :] = (_rms(xh_ref[...], g_ref[...]) * keep).astype(BF16)
        _norm_rows(x_ref, g_ref, h_ref, tm, h_off=HALO)

    u = jnp.dot(h_ref[...], wu_ref[...], preferred_element_type=F32)
    gate = jnp.dot(h_ref[HALO:, :], wg_ref[...], preferred_element_type=F32)
    conv = (cb_ref[...] + cw_ref[2:3, :] * u
            + cw_ref[1:2, :] * pltpu.roll(u, 1, axis=0)
            + cw_ref[0:1, :] * pltpu.roll(u, 2, axis=0))
    o_ref[...] = (jax.nn.gelu(conv[HALO:, :]) * gate).astype(o_ref.dtype)


def _ffn_up(x, g, wu, wg, cw, cb, seq, tm_pref, tf_pref):
    t, d = x.shape
    f = wu.shape[1]
    tm = _tile(seq, tm_pref, HALO)
    tf = _tile(f, tf_pref, LANES)
    hb = tm // HALO
    return pl.pallas_call(
        functools.partial(_ffn_up_kernel, seq=seq),
        grid=(t // tm, f // tf),
        in_specs=[pl.BlockSpec((tm, d), lambda i, j: (i, 0)),
                  pl.BlockSpec((HALO, d), lambda i, j: (jnp.maximum(i * hb - 1, 0), 0)),
                  pl.BlockSpec((1, d), lambda i, j: (0, 0)),
                  pl.BlockSpec((d, tf), lambda i, j: (0, j)),
                  pl.BlockSpec((d, tf), lambda i, j: (0, j)),
                  pl.BlockSpec((8, tf), lambda i, j: (0, j)),
                  pl.BlockSpec((1, tf), lambda i, j: (0, j))],
        out_specs=pl.BlockSpec((tm, tf), lambda i, j: (i, j)),
        out_shape=jax.ShapeDtypeStruct((t, f), BF16),
        scratch_shapes=[pltpu.VMEM((HALO + tm, d), BF16)],
        compiler_params=_params("parallel", "arbitrary"),
        name="ffn_up_conv_gate",
    )(x, x, g, wu, wg, cw, cb)


def _final_norm_kernel(x_ref, g_ref, o_ref):
    o_ref[...] = _rms(x_ref[...], g_ref[...])


def _final_norm(x, g):
    t, d = x.shape
    tm = _tile(t, 256, 8)
    return pl.pallas_call(
        _final_norm_kernel,
        grid=(t // tm,),
        in_specs=[pl.BlockSpec((tm, d), lambda i: (i, 0)), pl.BlockSpec((1, d), lambda i: (0, 0))],
        out_specs=pl.BlockSpec((tm, d), lambda i: (i, 0)),
        out_shape=jax.ShapeDtypeStruct((t, d), F32),
        compiler_params=_params("parallel"),
        name="final_norm",
    )(x, g)


def _rope_tables(seq, d):
    pos = jnp.arange(seq, dtype=F32)
    inv_freq = ROPE_THETA ** (-jnp.arange(0, d, 2, dtype=F32) / d)
    ang = pos[:, None] * inv_freq[None, :]
    return jnp.cos(ang), jnp.sin(ang)


def _retention_tables(blk):
    log_gamma = jnp.log(1.0 - 2.0 ** (-5.0 - jnp.arange(RET_HEADS, dtype=F32)))
    idx = jnp.arange(blk, dtype=F32)
    chunk = jnp.arange(blk) // CHUNK
    dist = jnp.abs(idx[:, None] - idx[None, :])
    dec = jnp.where(chunk[None, :] <= chunk[:, None],
                    jnp.exp(log_gamma[:, None, None] * dist[None]), 0.0)
    sin_state = jnp.exp(log_gamma[:, None] * (blk - 1 - idx)[None, :])
    cross = jnp.exp(log_gamma[:, None] * (idx + 1.0)[None, :])
    bcast = lambda a: jnp.broadcast_to(a[:, :, None], (RET_HEADS, blk, RET_DK))
    block_decay = tuple(math.exp(math.log(1.0 - 2.0 ** (-5.0 - h)) * blk) for h in range(RET_HEADS))
    return dec, bcast(sin_state), bcast(cross), block_decay


def _rotate_half_cols(w):
    half = w.shape[-1] // 2
    return jnp.concatenate([-w[..., half:], w[..., :half]], axis=-1)


def kernel(x, norm1_g, w_in, mla_q_norm_g, mla_kv_norm_g, mla_w_uq, mla_w_ukv, fox_b_f,
           w_br_fox, w_br_mla, w_br_ret, w_out, norm2_g, ffn_w_up, ffn_w_gate,
           ffn_conv_w, ffn_conv_b, ffn_w_down, final_norm_g):
    batch, seq, d = x.shape
    depth = w_in.shape[0]
    t = batch * seq
    q_lora = mla_w_uq.shape[1]
    kv_lora = mla_w_ukv.shape[1]

    widths = (FOX_W, FOX_W, FOX_W, FOX_HEADS, q_lora, kv_lora, MLA_ROPE,
              RET_QK_W, RET_QK_W, RET_V_W, RET_V_W, 3 * d)
    offs = [0]
    for wd in widths:
        offs.append(offs[-1] + wd)
    seg = lambda k: w_in[:, :, offs[k]:offs[k + 1]]
    zeros = lambda n: jnp.zeros((depth, d, n), w_in.dtype)
    w_p = jnp.concatenate(
        [seg(0), seg(1), seg(2), seg(5), seg(4), seg(9), seg(10), seg(7), seg(8),
         seg(6), zeros(LANES - MLA_ROPE), _rotate_half_cols(seg(6)), zeros(LANES - MLA_ROPE)],
        axis=-1).astype(BF16)
    assert w_p.shape[-1] == P_WIDTH and q_lora == 512 and kv_lora == 256
    w_gates = seg(11).astype(BF16)
    w_ff = jnp.concatenate([seg(3), zeros(LANES - FOX_HEADS)], axis=-1).astype(BF16)
    b_ff = jnp.pad(fox_b_f, ((0, 0), (0, LANES - FOX_HEADS)))[:, None, :]

    uq = mla_w_uq.reshape(depth, q_lora, MLA_HEADS, MLA_NOPE + MLA_ROPE)
    uq_rope = uq[..., MLA_NOPE:]
    pad_q = jnp.zeros((depth, q_lora, MLA_HEADS, MLA_QK_PAD - MLA_NOPE - MLA_ROPE), uq.dtype)
    w_q = jnp.concatenate([uq, pad_q], axis=-1).reshape(depth, q_lora, -1).astype(BF16)
    pad_r = jnp.zeros((depth, q_lora, MLA_HEADS, LANES - MLA_ROPE), uq.dtype)
    w_qr = jnp.concatenate([_rotate_half_cols(uq_rope), pad_r], axis=-1).reshape(depth, q_lora, -1).astype(BF16)
    ukv = mla_w_ukv.reshape(depth, kv_lora, MLA_HEADS, MLA_NOPE + MLA_V)
    w_k = ukv[..., :MLA_NOPE].reshape(depth, kv_lora, -1).astype(BF16)
    w_v = ukv[..., MLA_NOPE:].reshape(depth, kv_lora, -1).astype(BF16)

    w_bf, w_bm, w_br = w_br_fox.astype(BF16), w_br_mla.astype(BF16), w_br_ret.astype(BF16)
    w_o = w_out.astype(BF16)
    w_up, w_gate, w_down = ffn_w_up.astype(BF16), ffn_w_gate.astype(BF16), ffn_w_down.astype(BF16)
    conv_w = jnp.pad(ffn_conv_w, ((0, 0), (0, 8 - CONV_W), (0, 0)))
    conv_b = ffn_conv_b[:, None, :]

    cos64, sin64 = _rope_tables(seq, MLA_ROPE)
    zpad = jnp.zeros((seq, LANES - MLA_ROPE), F32)
    mla_cos = jnp.concatenate([cos64, cos64, zpad], axis=1)
    mla_sin = jnp.concatenate([sin64, sin64, zpad], axis=1)
    cos128, sin128 = _rope_tables(seq, RET_DK)
    ret_cos = jnp.concatenate([cos128, cos128], axis=1)
    ret_sin = jnp.concatenate([-sin128, sin128], axis=1)
    ret_blk = _tile(seq, 512, CHUNK)
    dec, sin_state, cross, block_decay = _retention_tables(ret_blk)

    att_blk = _tile(seq, 512, CHUNK)
    row = jnp.arange(att_blk, dtype=jnp.int32)[:, None]
    col = jnp.arange(att_blk, dtype=jnp.int32)[None, :]
    rel_frame = row - col
    rel_chunk = (row // CHUNK) * CHUNK + (CHUNK - 1) - col

    xf = x.reshape(t, d)
    for i in range(depth):
        g1 = norm1_g[i][None, :]
        p = _norm_matmul(xf, g1, w_p[i], BF16, False, 1024, 1280)
        gates = _norm_matmul(xf, g1, w_gates[i], F32, True, 1024, 1024)
        c = _fox_gate(xf, g1, w_ff[i], b_ff[i], batch, seq)
        neg_c = -jnp.transpose(c.reshape(batch, seq, LANES)[:, :, :FOX_HEADS], (0, 2, 1))[:, :, None, :]

        a = _attention(p, OFF_FQ // FOX_W, p, OFF_FK // FOX_W, p, OFF_FV // FOX_W, rel_frame, neg_c,
                       batch=batch, seq=seq, heads=FOX_HEADS, dq=FOX_DH, dv=FOX_DH,
                       scale=FOX_DH ** -0.5, name="fox_attention")
        qm, km, vm = _mla_prep(p, mla_q_norm_g[i][None, :], mla_kv_norm_g[i][None, :],
                               w_q[i], w_qr[i], w_k[i], w_v[i], mla_cos, mla_sin, seq)
        bm = _attention(qm, 0, km, 0, vm, 0, rel_chunk, None,
                        batch=batch, seq=seq, heads=MLA_HEADS, dq=MLA_QK_PAD, dv=MLA_V,
                        scale=(MLA_NOPE + MLA_ROPE) ** -0.5, name="mla_attention")
        cr = _retention(p, ret_cos, ret_sin, dec, sin_state, cross, block_decay, batch, seq)

        merged = _merge(a, bm, cr, gates, w_bf[i], w_bm[i], w_br[i])
        xf = _matmul_residual(merged, w_o[i], xf, 1024, 1024, "out_proj")
        act = _ffn_up(xf, norm2_g[i][None, :], w_up[i], w_gate[i], conv_w[i], conv_b[i], seq, 512, 512)
        xf = _matmul_residual(act, w_down[i], xf, 512, 1024, "ffn_down")
    return _final_norm(xf, final_norm_g[None, :]).reshape(batch, seq, d)
```

```python
import functools
import math

import jax
import jax.numpy as jnp
from jax import lax
from jax.experimental import pallas as pl
from jax.experimental.pallas import tpu as pltpu

F32 = jnp.float32
BF16 = jnp.bfloat16

NORM_EPS = 1e-6
ROPE_THETA = 10000.0
CHUNK = 64

FOX_HEADS, FOX_DH = 6, 128
MLA_HEADS, MLA_NOPE, MLA_ROPE, MLA_V = 6, 128, 64, 128
MLA_QK_PAD = 256
RET_HEADS, RET_DK, RET_DV = 4, 128, 256
CONV_W = 3
LANES = 128
HALO = 16
BIAS_TERMS = 3

FOX_W = FOX_HEADS * FOX_DH
MLA_W = MLA_HEADS * MLA_V
RET_QK_W = RET_HEADS * RET_DK
RET_V_W = RET_HEADS * RET_DV

OFF_FQ, OFF_FK, OFF_FV = 0, FOX_W, 2 * FOX_W
OFF_MKV = 3 * FOX_W
OFF_MQ = OFF_MKV + 256
OFF_RV = OFF_MQ + 512
OFF_RG = OFF_RV + RET_V_W
OFF_RQ = OFF_RG + RET_V_W
OFF_RK = OFF_RQ + RET_QK_W
OFF_MKR = OFF_RK + RET_QK_W
OFF_MKRR = OFF_MKR + LANES
P_WIDTH = OFF_MKRR + LANES

VMEM_LIMIT_BYTES = 56 * 1024 * 1024


def _tile(n, pref, align):
    t = (min(pref, n) // align) * align
    while t > align and n % t:
        t -= align
    assert t >= align and n % t == 0, (n, pref, align)
    return t


def _params(*sem):
    return pltpu.CompilerParams(dimension_semantics=sem, vmem_limit_bytes=VMEM_LIMIT_BYTES)


def _rms(x, g):
    return x * lax.rsqrt(jnp.mean(x * x, axis=-1, keepdims=True) + NORM_EPS) * g


def _norm_rows(x_ref, g_ref, h_ref, rows, h_off=0):
    rc = _tile(rows, 256, 8)

    def body(c, carry):
        r0 = pl.multiple_of(c * rc, rc)
        h_ref[pl.ds(h_off + r0, rc), :] = _rms(x_ref[pl.ds(r0, rc), :], g_ref[...]).astype(BF16)
        return carry

    lax.fori_loop(0, rows // rc, body, 0)


def _norm_mm_kernel(x_ref, g_ref, w_ref, o_ref, h_ref, *, sigmoid):
    @pl.when(pl.program_id(1) == 0)
    def _():
        _norm_rows(x_ref, g_ref, h_ref, x_ref.shape[0])

    y = jnp.dot(h_ref[...], w_ref[...], preferred_element_type=F32)
    if sigmoid:
        y = jax.nn.sigmoid(y)
    o_ref[...] = y.astype(o_ref.dtype)


def _norm_matmul(x, g, w, out_dtype, sigmoid, tm_pref, tn_pref):
    t, d = x.shape
    n = w.shape[1]
    tm = _tile(t, tm_pref, 8)
    tn = _tile(n, tn_pref, LANES)
    return pl.pallas_call(
        functools.partial(_norm_mm_kernel, sigmoid=sigmoid),
        grid=(t // tm, n // tn),
        in_specs=[pl.BlockSpec((tm, d), lambda i, j: (i, 0)),
                  pl.BlockSpec((1, d), lambda i, j: (0, 0)),
                  pl.BlockSpec((d, tn), lambda i, j: (0, j))],
        out_specs=pl.BlockSpec((tm, tn), lambda i, j: (i, j)),
        out_shape=jax.ShapeDtypeStruct((t, n), out_dtype),
        scratch_shapes=[pltpu.VMEM((tm, d), BF16)],
        compiler_params=_params("parallel", "arbitrary"),
        name="in_proj_sigmoid" if sigmoid else "in_proj",
    )(x, g, w)


def _fox_gate_kernel(x_ref, g_ref, w_ref, b_ref, tri_ref, o_ref, h_ref, carry_ref):
    @pl.when(pl.program_id(1) == 0)
    def _():
        carry_ref[...] = jnp.zeros_like(carry_ref)

    _norm_rows(x_ref, g_ref, h_ref, x_ref.shape[0])
    f = jnp.dot(h_ref[...], w_ref[...], preferred_element_type=F32) + b_ref[...]
    lf = jnp.minimum(f, 0.0) - jnp.log1p(jnp.exp(-jnp.abs(f)))
    hi = lf.astype(BF16)
    r1 = lf - hi.astype(F32)
    mid = r1.astype(BF16)
    lo = (r1 - mid.astype(F32)).astype(BF16)
    tri = tri_ref[...]
    c = (jnp.dot(tri, hi, preferred_element_type=F32)
         + jnp.dot(tri, mid, preferred_element_type=F32)
         + jnp.dot(tri, lo, preferred_element_type=F32)) + carry_ref[0:1, :]
    carry_ref[...] = jnp.broadcast_to(c[-1:, :], carry_ref.shape)
    lane = lax.broadcasted_iota(jnp.int32, (c.shape[0], LANES), 1)
    for h in range(FOX_HEADS):
        rest = jnp.broadcast_to(c[:, h:h + 1], lane.shape) * (-(FOX_DH ** 0.5))
        slab = jnp.zeros(lane.shape, F32)
        for term in range(BIAS_TERMS):
            part = rest.astype(BF16).astype(F32)
            slab = jnp.where(lane == term, part, slab)
            rest = rest - part
        o_ref[:, h * LANES:(h + 1) * LANES] = slab.astype(BF16)


def _fox_gate(x, g, w_ff, b_ff, batch, seq):
    t, d = x.shape
    ts = _tile(seq, 512, 8)
    nb = seq // ts
    tri = (jnp.arange(ts)[:, None] >= jnp.arange(ts)[None, :]).astype(BF16)
    return pl.pallas_call(
        _fox_gate_kernel,
        grid=(batch, nb),
        in_specs=[pl.BlockSpec((ts, d), lambda b, n: (b * nb + n, 0)),
                  pl.BlockSpec((1, d), lambda b, n: (0, 0)),
                  pl.BlockSpec((d, LANES), lambda b, n: (0, 0)),
                  pl.BlockSpec((1, LANES), lambda b, n: (0, 0)),
                  pl.BlockSpec((ts, ts), lambda b, n: (0, 0))],
        out_specs=pl.BlockSpec((ts, FOX_HEADS * LANES), lambda b, n: (b * nb + n, 0)),
        out_shape=jax.ShapeDtypeStruct((t, FOX_HEADS * LANES), BF16),
        scratch_shapes=[pltpu.VMEM((ts, d), BF16), pltpu.VMEM((8, LANES), F32)],
        compiler_params=_params("parallel", "arbitrary"),
        name="fox_gate_cumsum",
    )(x, g, w_ff, b_ff, tri)


def _mla_prep_kernel(mq_ref, mkv_ref, kr_ref, krr_ref, gq_ref, gkv_ref, wq_ref, wqr_ref,
                     wk_ref, wv_ref, cos_ref, sin_ref, q_ref, k_ref, v_ref):
    cos = cos_ref[...]
    sin = sin_ref[...]
    cqn = _rms(mq_ref[...].astype(F32), gq_ref[...]).astype(BF16)
    q = jnp.dot(cqn, wq_ref[...], preferred_element_type=F32)
    qrot = jnp.dot(cqn, wqr_ref[...], preferred_element_type=F32)
    ckvn = _rms(mkv_ref[...].astype(F32), gkv_ref[...]).astype(BF16)
    kn = jnp.dot(ckvn, wk_ref[...], preferred_element_type=F32)
    v_ref[...] = jnp.dot(ckvn, wv_ref[...], preferred_element_type=F32).astype(BF16)
    kr = (kr_ref[...].astype(F32) * cos + krr_ref[...].astype(F32) * sin).astype(BF16)
    for h in range(MLA_HEADS):
        lo = h * MLA_QK_PAD
        q_ref[:, lo:lo + LANES] = q[:, lo:lo + LANES].astype(BF16)
        q_ref[:, lo + LANES:lo + 2 * LANES] = (
            q[:, lo + LANES:lo + 2 * LANES] * cos + qrot[:, h * LANES:(h + 1) * LANES] * sin).astype(BF16)
        k_ref[:, lo:lo + LANES] = kn[:, h * LANES:(h + 1) * LANES].astype(BF16)
        k_ref[:, lo + LANES:lo + 2 * LANES] = kr


def _mla_prep(p, gq, gkv, wq, wqr, wk, wv, cos, sin, seq):
    t = p.shape[0]
    tm = _tile(seq, 512, 8)
    ns = seq // tm
    ql, kvl = wq.shape[0], wk.shape[0]
    row = lambda i: (i, 0)
    const = lambda i: (0, 0)
    return pl.pallas_call(
        _mla_prep_kernel,
        grid=(t // tm,),
        in_specs=[pl.BlockSpec((tm, ql), lambda i: (i, OFF_MQ // ql)),
                  pl.BlockSpec((tm, kvl), lambda i: (i, OFF_MKV // kvl)),
                  pl.BlockSpec((tm, LANES), lambda i: (i, OFF_MKR // LANES)),
                  pl.BlockSpec((tm, LANES), lambda i: (i, OFF_MKRR // LANES)),
                  pl.BlockSpec((1, ql), const),
                  pl.BlockSpec((1, kvl), const),
                  pl.BlockSpec(wq.shape, const),
                  pl.BlockSpec(wqr.shape, const),
                  pl.BlockSpec(wk.shape, const),
                  pl.BlockSpec(wv.shape, const),
                  pl.BlockSpec((tm, LANES), lambda i: (i % ns, 0)),
                  pl.BlockSpec((tm, LANES), lambda i: (i % ns, 0))],
        out_specs=[pl.BlockSpec((tm, MLA_HEADS * MLA_QK_PAD), row),
                   pl.BlockSpec((tm, MLA_HEADS * MLA_QK_PAD), row),
                   pl.BlockSpec((tm, MLA_W), row)],
        out_shape=[jax.ShapeDtypeStruct((t, MLA_HEADS * MLA_QK_PAD), BF16),
                   jax.ShapeDtypeStruct((t, MLA_HEADS * MLA_QK_PAD), BF16),
                   jax.ShapeDtypeStruct((t, MLA_W), BF16)],
        compiler_params=_params("parallel"),
        name="mla_prep",
    )(p, p, p, p, gq, gkv, wq, wqr, wk, wv, cos, sin)


def _attn_kernel(qi_tab, ki_tab, *refs, heads, dq, dv, coef, has_bias, tq, tk):
    if has_bias:
        q_ref, k_ref, v_ref, rel_ref, kb_ref, o_ref, m_ref, l_ref, acc_ref = refs
    else:
        q_ref, k_ref, v_ref, rel_ref, o_ref, m_ref, l_ref, acc_ref = refs
    step = pl.program_id(1)
    qi = qi_tab[step]
    ki = ki_tab[step]

    @pl.when(ki == 0)
    def _():
        m_ref[...] = jnp.full_like(m_ref, -jnp.inf)
        l_ref[...] = jnp.zeros_like(l_ref)
        acc_ref[...] = jnp.zeros_like(acc_ref)

    def accumulate(masked):
        ones_v = jnp.ones((tk, LANES), BF16)
        if has_bias:
            lane = lax.broadcasted_iota(jnp.int32, (tq, LANES), 1)
            ones_q = jnp.where(lane < BIAS_TERMS, 1.0, 0.0).astype(BF16)
        if masked:
            visible = rel_ref[...] >= (ki * tk - qi * tq)
        for h in range(heads):
            q = q_ref[:, h * dq:(h + 1) * dq]
            k = k_ref[:, h * dq:(h + 1) * dq]
            if has_bias:
                q = jnp.concatenate([q, ones_q], axis=1)
                k = jnp.concatenate([k, kb_ref[:, h * LANES:(h + 1) * LANES]], axis=1)
            s = lax.dot_general(q, k, (((1,), (1,)), ((), ())), preferred_element_type=F32)
            if masked:
                s = jnp.where(visible, s, -jnp.inf)
            m_prev = m_ref[h]
            m_new = jnp.maximum(m_prev, jnp.max(s, axis=1, keepdims=True))
            alpha = jnp.exp2((m_prev - m_new) * coef)
            p = jnp.concatenate(
                [jnp.exp2((s[:, j * LANES:(j + 1) * LANES] - m_new) * coef) for j in range(tk // LANES)],
                axis=1).astype(BF16)
            v1 = jnp.concatenate([v_ref[:, h * dv:(h + 1) * dv], ones_v], axis=1)
            pv = jnp.dot(p, v1, preferred_element_type=F32)
            acc_ref[:, h * dv:(h + 1) * dv] = alpha * acc_ref[:, h * dv:(h + 1) * dv] + pv[:, :dv]
            l_ref[h] = alpha * l_ref[h] + pv[:, dv:]
            m_ref[h] = m_new

    @pl.when(ki != qi)
    def _():
        accumulate(False)

    @pl.when(ki == qi)
    def _():
        accumulate(True)
        for h in range(heads):
            o_ref[:, h * dv:(h + 1) * dv] = (
                acc_ref[:, h * dv:(h + 1) * dv] / l_ref[h]).astype(o_ref.dtype)


def _attention(q_arr, q_cb, k_arr, k_cb, v_arr, v_cb, rel, kbias, *, batch, seq, heads, dq, dv, scale, name):
    assert dv == LANES
    tq = tk = rel.shape[0]
    nq = seq // tq
    pairs = [(qi, ki) for qi in range(nq) for ki in range(qi + 1)]
    qi_tab = jnp.asarray([p[0] for p in pairs], jnp.int32)
    ki_tab = jnp.asarray([p[1] for p in pairs], jnp.int32)
    has_bias = kbias is not None
    in_specs = [pl.BlockSpec((tq, heads * dq), lambda b, s, qt, kt: (b * nq + qt[s], q_cb)),
                pl.BlockSpec((tk, heads * dq), lambda b, s, qt, kt: (b * nq + kt[s], k_cb)),
                pl.BlockSpec((tk, heads * dv), lambda b, s, qt, kt: (b * nq + kt[s], v_cb)),
                pl.BlockSpec((tq, tk), lambda b, s, qt, kt: (0, 0))]
    args = [q_arr, k_arr, v_arr, rel]
    if has_bias:
        in_specs.append(pl.BlockSpec((tk, heads * LANES), lambda b, s, qt, kt: (b * nq + kt[s], 0)))
        args.append(kbias)
    return pl.pallas_call(
        functools.partial(_attn_kernel, heads=heads, dq=dq, dv=dv, coef=scale * math.log2(math.e),
                          has_bias=has_bias, tq=tq, tk=tk),
        grid_spec=pltpu.PrefetchScalarGridSpec(
            num_scalar_prefetch=2,
            grid=(batch, len(pairs)),
            in_specs=in_specs,
            out_specs=pl.BlockSpec((tq, heads * dv), lambda b, s, qt, kt: (b * nq + qt[s], 0)),
            scratch_shapes=[pltpu.VMEM((heads, tq, LANES), F32),
                            pltpu.VMEM((heads, tq, LANES), F32),
                            pltpu.VMEM((tq, heads * dv), F32)]),
        out_shape=jax.ShapeDtypeStruct((batch * seq, heads * dv), BF16),
        compiler_params=_params("parallel", "arbitrary"),
        name=name,
    )(qi_tab, ki_tab, *args)


def _retention_kernel(q_ref, k_ref, v_ref, g_ref, cos_ref, sin_ref, dec_ref, sin_state_ref,
                      cross_ref, o_ref, state_ref, *, block_decay):
    @pl.when(pl.program_id(1) == 0)
    def _():
        state_ref[...] = jnp.zeros_like(state_ref)

    cos = cos_ref[...]
    sin = sin_ref[...]

    def rope(x):
        return x * cos + pltpu.roll(x, RET_DK // 2, axis=1) * sin

    for h in range(RET_HEADS):
        qs = slice(h * RET_DK, (h + 1) * RET_DK)
        vs = slice(h * RET_DV, (h + 1) * RET_DV)
        q = rope(q_ref[:, qs].astype(F32)).astype(BF16)
        kf = rope(k_ref[:, qs].astype(F32)) * (RET_DK ** -0.5)
        v = v_ref[:, vs]
        scores = lax.dot_general(q, kf.astype(BF16), (((1,), (1,)), ((), ())),
                                 preferred_element_type=F32) * dec_ref[h]
        o = jnp.dot(scores.astype(BF16), v, preferred_element_type=F32)
        state = state_ref[h]
        cross = jnp.dot(q, state.astype(BF16), preferred_element_type=F32)
        cd = cross_ref[h]
        o = o + cross * jnp.concatenate([cd, cd], axis=1)
        o = o * lax.rsqrt(jnp.mean(o * o, axis=-1, keepdims=True) + NORM_EPS)
        g = g_ref[:, vs].astype(F32)
        o_ref[:, vs] = (o * (g * jax.nn.sigmoid(g))).astype(o_ref.dtype)
        ks = (kf * sin_state_ref[h]).astype(BF16)
        kv = lax.dot_general(ks, v, (((0,), (0,)), ((), ())), preferred_element_type=F32)
        state_ref[h] = block_decay[h] * state + kv


def _retention(p, cos, sin, dec, sin_state, cross, block_decay, batch, seq):
    blk = dec.shape[1]
    nb = seq // blk
    row = lambda b, n: (b * nb + n, 0)
    const3 = lambda b, n: (0, 0, 0)
    return pl.pallas_call(
        functools.partial(_retention_kernel, block_decay=block_decay),
        grid=(batch, nb),
        in_specs=[pl.BlockSpec((blk, RET_QK_W), lambda b, n: (b * nb + n, OFF_RQ // RET_QK_W)),
                  pl.BlockSpec((blk, RET_QK_W), lambda b, n: (b * nb + n, OFF_RK // RET_QK_W)),
                  pl.BlockSpec((blk, RET_V_W), lambda b, n: (b * nb + n, OFF_RV // RET_V_W)),
                  pl.BlockSpec((blk, RET_V_W), lambda b, n: (b * nb + n, OFF_RG // RET_V_W)),
                  pl.BlockSpec((blk, RET_DK), lambda b, n: (n, 0)),
                  pl.BlockSpec((blk, RET_DK), lambda b, n: (n, 0)),
                  pl.BlockSpec(dec.shape, const3),
                  pl.BlockSpec(sin_state.shape, const3),
                  pl.BlockSpec(cross.shape, const3)],
        out_specs=pl.BlockSpec((blk, RET_V_W), row),
        out_shape=jax.ShapeDtypeStruct((batch * seq, RET_V_W), BF16),
        scratch_shapes=[pltpu.VMEM((RET_HEADS, RET_DK, RET_DV), F32)],
        compiler_params=_params("parallel", "arbitrary"),
        name="retention",
    )(p, p, p, p, cos, sin, dec, sin_state, cross)


def _merge_kernel(a_ref, b_ref, c_ref, g0_ref, g1_ref, g2_ref, wa_ref, wb_ref, wc_ref, o_ref):
    m = g0_ref[...] * jnp.dot(a_ref[...], wa_ref[...], preferred_element_type=F32)
    m = m + g1_ref[...] * jnp.dot(b_ref[...], wb_ref[...], preferred_element_type=F32)
    m = m + g2_ref[...] * jnp.dot(c_ref[...], wc_ref[...], preferred_element_type=F32)
    o_ref[...] = m.astype(o_ref.dtype)


def _merge(a, bm, c, gates, wa, wb, wc):
    t = a.shape[0]
    d = wa.shape[1]
    tm = _tile(t, 512, 8)
    tn = _tile(d, 1024, LANES)
    nj = d // tn
    act = lambda w: pl.BlockSpec((tm, w), lambda i, j: (i, 0))
    wgt = lambda k: pl.BlockSpec((k, tn), lambda i, j: (0, j))
    gate = lambda n: pl.BlockSpec((tm, tn), lambda i, j: (i, n * nj + j))
    return pl.pallas_call(
        _merge_kernel,
        grid=(t // tm, nj),
        in_specs=[act(a.shape[1]), act(bm.shape[1]), act(c.shape[1]), gate(0), gate(1), gate(2),
                  wgt(wa.shape[0]), wgt(wb.shape[0]), wgt(wc.shape[0])],
        out_specs=pl.BlockSpec((tm, tn), lambda i, j: (i, j)),
        out_shape=jax.ShapeDtypeStruct((t, d), BF16),
        compiler_params=_params("parallel", "arbitrary"),
        name="branch_merge",
    )(a, bm, c, gates, gates, gates, wa, wb, wc)


def _mm_res_kernel(a_ref, w_ref, r_ref, o_ref):
    o_ref[...] = r_ref[...] + jnp.dot(a_ref[...], w_ref[...], preferred_element_type=F32)


def _matmul_residual(a, w, r, tm_pref, tn_pref, name):
    t, k = a.shape
    n = w.shape[1]
    tm = _tile(t, tm_pref, 8)
    tn = _tile(n, tn_pref, LANES)
    return pl.pallas_call(
        _mm_res_kernel,
        grid=(n // tn, t // tm),
        in_specs=[pl.BlockSpec((tm, k), lambda j, i: (i, 0)),
                  pl.BlockSpec((k, tn), lambda j, i: (0, j)),
                  pl.BlockSpec((tm, tn), lambda j, i: (i, j))],
        out_specs=pl.BlockSpec((tm, tn), lambda j, i: (i, j)),
        out_shape=jax.ShapeDtypeStruct((t, n), F32),
        compiler_params=_params("parallel", "arbitrary"),
        name=name,
    )(a, w, r)


def _ffn_up_kernel(x_ref, xh_ref, g_ref, wu_ref, wg_ref, cw_ref, cb_ref, o_ref, h_ref, *, seq):
    tm = x_ref.shape[0]

    @pl.when(pl.program_id(1) == 0)
    def _():
        keep = ((pl.program_id(0) * tm) % seq != 0).astype(F32)
        h_ref[0:HALO, :] = (_rms(xh_ref[...], g_ref[...]) * keep).astype(BF16)
        _norm_rows(x_ref, g_ref, h_ref, tm, h_off=HALO)

    u = jnp.dot(h_ref[...], wu_ref[...], preferred_element_type=F32)
    gate = jnp.dot(h_ref[HALO:, :], wg_ref[...], preferred_element_type=F32)
    conv = (cb_ref[...] + cw_ref[2:3, :] * u
            + cw_ref[1:2, :] * pltpu.roll(u, 1, axis=0)
            + cw_ref[0:1, :] * pltpu.roll(u, 2, axis=0))
    o_ref[...] = (jax.nn.gelu(conv[HALO:, :]) * gate).astype(o_ref.dtype)


def _ffn_up(x, g, wu, wg, cw, cb, seq, tm_pref, tf_pref):
    t, d = x.shape
    f = wu.shape[1]
    tm = _tile(seq, tm_pref, HALO)
    tf = _tile(f, tf_pref, LANES)
    hb = tm // HALO
    return pl.pallas_call(
        functools.partial(_ffn_up_kernel, seq=seq),
        grid=(t // tm, f // tf),
        in_specs=[pl.BlockSpec((tm, d), lambda i, j: (i, 0)),
                  pl.BlockSpec((HALO, d), lambda i, j: (jnp.maximum(i * hb - 1, 0), 0)),
                  pl.BlockSpec((1, d), lambda i, j: (0, 0)),
                  pl.BlockSpec((d, tf), lambda i, j: (0, j)),
                  pl.BlockSpec((d, tf), lambda i, j: (0, j)),
                  pl.BlockSpec((8, tf), lambda i, j: (0, j)),
                  pl.BlockSpec((1, tf), lambda i, j: (0, j))],
        out_specs=pl.BlockSpec((tm, tf), lambda i, j: (i, j)),
        out_shape=jax.ShapeDtypeStruct((t, f), BF16),
        scratch_shapes=[pltpu.VMEM((HALO + tm, d), BF16)],
        compiler_params=_params("parallel", "arbitrary"),
        name="ffn_up_conv_gate",
    )(x, x, g, wu, wg, cw, cb)


def _final_norm_kernel(x_ref, g_ref, o_ref):
    o_ref[...] = _rms(x_ref[...], g_ref[...])


def _final_norm(x, g):
    t, d = x.shape
    tm = _tile(t, 256, 8)
    return pl.pallas_call(
        _final_norm_kernel,
        grid=(t // tm,),
        in_specs=[pl.BlockSpec((tm, d), lambda i: (i, 0)), pl.BlockSpec((1, d), lambda i: (0, 0))],
        out_specs=pl.BlockSpec((tm, d), lambda i: (i, 0)),
        out_shape=jax.ShapeDtypeStruct((t, d), F32),
        compiler_params=_params("parallel"),
        name="final_norm",
    )(x, g)


def _rope_tables(seq, d):
    pos = jnp.arange(seq, dtype=F32)
    inv_freq = ROPE_THETA ** (-jnp.arange(0, d, 2, dtype=F32) / d)
    ang = pos[:, None] * inv_freq[None, :]
    return jnp.cos(ang), jnp.sin(ang)


def _retention_tables(blk):
    log_gamma = jnp.log(1.0 - 2.0 ** (-5.0 - jnp.arange(RET_HEADS, dtype=F32)))
    idx = jnp.arange(blk, dtype=F32)
    chunk = jnp.arange(blk) // CHUNK
    dist = jnp.abs(idx[:, None] - idx[None, :])
    dec = jnp.where(chunk[None, :] <= chunk[:, None],
                    jnp.exp(log_gamma[:, None, None] * dist[None]), 0.0)
    sin_state = jnp.exp(log_gamma[:, None] * (blk - 1 - idx)[None, :])
    cross = jnp.exp(log_gamma[:, None] * (idx + 1.0)[None, :])
    bcast = lambda a: jnp.broadcast_to(a[:, :, None], (RET_HEADS, blk, RET_DK))
    block_decay = tuple(math.exp(math.log(1.0 - 2.0 ** (-5.0 - h)) * blk) for h in range(RET_HEADS))
    return dec, bcast(sin_state), bcast(cross), block_decay


def _rotate_half_cols(w):
    half = w.shape[-1] // 2
    return jnp.concatenate([-w[..., half:], w[..., :half]], axis=-1)


def kernel(x, norm1_g, w_in, mla_q_norm_g, mla_kv_norm_g, mla_w_uq, mla_w_ukv, fox_b_f,
           w_br_fox, w_br_mla, w_br_ret, w_out, norm2_g, ffn_w_up, ffn_w_gate,
           ffn_conv_w, ffn_conv_b, ffn_w_down, final_norm_g):
    batch, seq, d = x.shape
    depth = w_in.shape[0]
    t = batch * seq
    q_lora = mla_w_uq.shape[1]
    kv_lora = mla_w_ukv.shape[1]

    widths = (FOX_W, FOX_W, FOX_W, FOX_HEADS, q_lora, kv_lora, MLA_ROPE,
              RET_QK_W, RET_QK_W, RET_V_W, RET_V_W, 3 * d)
    offs = [0]
    for wd in widths:
        offs.append(offs[-1] + wd)
    seg = lambda k: w_in[:, :, offs[k]:offs[k + 1]]
    zeros = lambda n: jnp.zeros((depth, d, n), w_in.dtype)
    w_p = jnp.concatenate(
        [seg(0), seg(1), seg(2), seg(5), seg(4), seg(9), seg(10), seg(7), seg(8),
         seg(6), zeros(LANES - MLA_ROPE), _rotate_half_cols(seg(6)), zeros(LANES - MLA_ROPE)],
        axis=-1).astype(BF16)
    assert w_p.shape[-1] == P_WIDTH and q_lora == 512 and kv_lora == 256
    w_gates = seg(11).astype(BF16)
    w_ff = jnp.concatenate([seg(3), zeros(LANES - FOX_HEADS)], axis=-1).astype(BF16)
    b_ff = jnp.pad(fox_b_f, ((0, 0), (0, LANES - FOX_HEADS)))[:, None, :]

    uq = mla_w_uq.reshape(depth, q_lora, MLA_HEADS, MLA_NOPE + MLA_ROPE)
    uq_rope = uq[..., MLA_NOPE:]
    pad_q = jnp.zeros((depth, q_lora, MLA_HEADS, MLA_QK_PAD - MLA_NOPE - MLA_ROPE), uq.dtype)
    w_q = jnp.concatenate([uq, pad_q], axis=-1).reshape(depth, q_lora, -1).astype(BF16)
    pad_r = jnp.zeros((depth, q_lora, MLA_HEADS, LANES - MLA_ROPE), uq.dtype)
    w_qr = jnp.concatenate([_rotate_half_cols(uq_rope), pad_r], axis=-1).reshape(depth, q_lora, -1).astype(BF16)
    ukv = mla_w_ukv.reshape(depth, kv_lora, MLA_HEADS, MLA_NOPE + MLA_V)
    w_k = ukv[..., :MLA_NOPE].reshape(depth, kv_lora, -1).astype(BF16)
    w_v = ukv[..., MLA_NOPE:].reshape(depth, kv_lora, -1).astype(BF16)

    w_bf, w_bm, w_br = w_br_fox.astype(BF16), w_br_mla.astype(BF16), w_br_ret.astype(BF16)
    w_o = w_out.astype(BF16)
    w_up, w_gate, w_down = ffn_w_up.astype(BF16), ffn_w_gate.astype(BF16), ffn_w_down.astype(BF16)
    conv_w = jnp.pad(ffn_conv_w, ((0, 0), (0, 8 - CONV_W), (0, 0)))
    conv_b = ffn_conv_b[:, None, :]

    cos64, sin64 = _rope_tables(seq, MLA_ROPE)
    zpad = jnp.zeros((seq, LANES - MLA_ROPE), F32)
    mla_cos = jnp.concatenate([cos64, cos64, zpad], axis=1)
    mla_sin = jnp.concatenate([sin64, sin64, zpad], axis=1)
    cos128, sin128 = _rope_tables(seq, RET_DK)
    ret_cos = jnp.concatenate([cos128, cos128], axis=1)
    ret_sin = jnp.concatenate([-sin128, sin128], axis=1)
    ret_blk = _tile(seq, 512, CHUNK)
    dec, sin_state, cross, block_decay = _retention_tables(ret_blk)

    att_blk = _tile(seq, 512, CHUNK)
    row = jnp.arange(att_blk, dtype=jnp.int32)[:, None]
    col = jnp.arange(att_blk, dtype=jnp.int32)[None, :]
    rel_frame = row - col
    rel_chunk = (row // CHUNK) * CHUNK + (CHUNK - 1) - col

    xf = x.reshape(t, d)
    for i in range(depth):
        g1 = norm1_g[i][None, :]
        p = _norm_matmul(xf, g1, w_p[i], BF16, False, 1024, 1280)
        gates = _norm_matmul(xf, g1, w_gates[i], F32, True, 1024, 1024)
        kbias = _fox_gate(xf, g1, w_ff[i], b_ff[i], batch, seq)

        a = _attention(p, OFF_FQ // FOX_W, p, OFF_FK // FOX_W, p, OFF_FV // FOX_W, rel_frame, kbias,
                       batch=batch, seq=seq, heads=FOX_HEADS, dq=FOX_DH, dv=FOX_DH,
                       scale=FOX_DH ** -0.5, name="fox_attention")
        qm, km, vm = _mla_prep(p, mla_q_norm_g[i][None, :], mla_kv_norm_g[i][None, :],
                               w_q[i], w_qr[i], w_k[i], w_v[i], mla_cos, mla_sin, seq)
        bm = _attention(qm, 0, km, 0, vm, 0, rel_chunk, None,
                        batch=batch, seq=seq, heads=MLA_HEADS, dq=MLA_QK_PAD, dv=MLA_V,
                        scale=(MLA_NOPE + MLA_ROPE) ** -0.5, name="mla_attention")
        cr = _retention(p, ret_cos, ret_sin, dec, sin_state, cross, block_decay, batch, seq)

        merged = _merge(a, bm, cr, gates, w_bf[i], w_bm[i], w_br[i])
        xf = _matmul_residual(merged, w_o[i], xf, 1024, 1024, "out_proj")
        act = _ffn_up(xf, norm2_g[i][None, :], w_up[i], w_gate[i], conv_w[i], conv_b[i], seq, 512, 512)
        xf = _matmul_residual(act, w_down[i], xf, 512, 1024, "ffn_down")
    return _final_norm(xf, final_norm_g[None, :]).reshape(batch, seq, d)
```

```python
import functools
import math

import jax
import jax.numpy as jnp
from jax import lax
from jax.experimental import pallas as pl
from jax.experimental.pallas import tpu as pltpu

F32 = jnp.float32
BF16 = jnp.bfloat16

NORM_EPS = 1e-6
ROPE_THETA = 10000.0
CHUNK = 64

FOX_HEADS, FOX_DH = 6, 128
MLA_HEADS, MLA_NOPE, MLA_ROPE, MLA_V = 6, 128, 64, 128
MLA_QK_PAD = 256
RET_HEADS, RET_DK, RET_DV = 4, 128, 256
N_BRANCH = 3
CONV_W = 3
LANES = 128
HALO = 16
BIAS_TERMS = 3

FOX_W = FOX_HEADS * FOX_DH
MLA_W = MLA_HEADS * MLA_V
RET_QK_W = RET_HEADS * RET_DK
RET_V_W = RET_HEADS * RET_DV

OFF_FQ, OFF_FK, OFF_FV = 0, FOX_W, 2 * FOX_W
OFF_MKV = 3 * FOX_W
OFF_MQ = OFF_MKV + 256
OFF_RV = OFF_MQ + 512
OFF_RG = OFF_RV + RET_V_W
OFF_RQ = OFF_RG + RET_V_W
OFF_RK = OFF_RQ + RET_QK_W
OFF_MKR = OFF_RK + RET_QK_W
OFF_MKRR = OFF_MKR + LANES
P_WIDTH = OFF_MKRR + LANES

VMEM_LIMIT_BYTES = 56 * 1024 * 1024


def _tile(n, pref, align):
    t = (min(pref, n) // align) * align
    while t > align and n % t:
        t -= align
    assert t >= align and n % t == 0, (n, pref, align)
    return t


def _params(*sem):
    return pltpu.CompilerParams(dimension_semantics=sem, vmem_limit_bytes=VMEM_LIMIT_BYTES)


def _layer_spec(arr, layer, block, index):
    return pl.BlockSpec((None,) + tuple(block), lambda *ids: (layer,) + tuple(index(*ids)))


def _layer_full(arr, layer):
    zeros = (0,) * (arr.ndim - 1)
    return pl.BlockSpec((None,) + arr.shape[1:], lambda *ids: (layer,) + zeros)


def _rms(x, g):
    return x * lax.rsqrt(jnp.mean(x * x, axis=-1, keepdims=True) + NORM_EPS) * g


def _norm_rows(x_ref, g_ref, h_ref, rows, h_off=0):
    rc = _tile(rows, 256, 8)

    def body(c, carry):
        r0 = pl.multiple_of(c * rc, rc)
        h_ref[pl.ds(h_off + r0, rc), :] = _rms(x_ref[pl.ds(r0, rc), :], g_ref[...]).astype(BF16)
        return carry

    lax.fori_loop(0, rows // rc, body, 0)


def _in_proj_kernel(x_ref, g_ref, w_ref, wf_ref, o_ref, f_ref, h_ref):
    @pl.when(pl.program_id(1) == 0)
    def _():
        _norm_rows(x_ref, g_ref, h_ref, x_ref.shape[0])
        f_ref[...] = jnp.dot(h_ref[...], wf_ref[...], preferred_element_type=F32)

    o_ref[...] = jnp.dot(h_ref[...], w_ref[...], preferred_element_type=F32).astype(o_ref.dtype)


def _in_proj(x, g, w, w_ff, layer):
    t, d = x.shape
    n = w.shape[2]
    tm = _tile(t, 1024, 8)
    tn = _tile(n, 1280, LANES)
    return pl.pallas_call(
        _in_proj_kernel,
        grid=(t // tm, n // tn),
        in_specs=[pl.BlockSpec((tm, d), lambda i, j: (i, 0)),
                  _layer_full(g, layer),
                  _layer_spec(w, layer, (d, tn), lambda i, j: (0, j)),
                  _layer_full(w_ff, layer)],
        out_specs=[pl.BlockSpec((tm, tn), lambda i, j: (i, j)),
                   pl.BlockSpec((tm, LANES), lambda i, j: (i, 0))],
        out_shape=[jax.ShapeDtypeStruct((t, n), BF16), jax.ShapeDtypeStruct((t, LANES), F32)],
        scratch_shapes=[pltpu.VMEM((tm, d), BF16)],
        compiler_params=_params("parallel", "arbitrary"),
        name="in_proj",
    )(x, g, w, w_ff)


def _fox_decay_kernel(f_ref, b_ref, tri_ref, o_ref, carry_ref):
    @pl.when(pl.program_id(1) == 0)
    def _():
        carry_ref[...] = jnp.zeros_like(carry_ref)

    f = f_ref[...] + b_ref[...]
    lf = jnp.minimum(f, 0.0) - jnp.log1p(jnp.exp(-jnp.abs(f)))
    hi = lf.astype(BF16)
    r1 = lf - hi.astype(F32)
    mid = r1.astype(BF16)
    lo = (r1 - mid.astype(F32)).astype(BF16)
    tri = tri_ref[...]
    c = (jnp.dot(tri, hi, preferred_element_type=F32)
         + jnp.dot(tri, mid, preferred_element_type=F32)
         + jnp.dot(tri, lo, preferred_element_type=F32)) + carry_ref[0:1, :]
    carry_ref[...] = jnp.broadcast_to(c[-1:, :], carry_ref.shape)
    lane = lax.broadcasted_iota(jnp.int32, (c.shape[0], LANES), 1)
    for h in range(FOX_HEADS):
        rest = jnp.broadcast_to(c[:, h:h + 1], lane.shape) * (-(FOX_DH ** 0.5))
        slab = jnp.zeros(lane.shape, F32)
        for term in range(BIAS_TERMS):
            part = rest.astype(BF16).astype(F32)
            slab = jnp.where(lane == term, part, slab)
            rest = rest - part
        o_ref[:, h * LANES:(h + 1) * LANES] = slab.astype(BF16)


def _fox_decay(f, b_ff, layer, batch, seq):
    t = f.shape[0]
    ts = _tile(seq, 1024, 8)
    nb = seq // ts
    tri = (jnp.arange(ts)[:, None] >= jnp.arange(ts)[None, :]).astype(BF16)
    return pl.pallas_call(
        _fox_decay_kernel,
        grid=(batch, nb),
        in_specs=[pl.BlockSpec((ts, LANES), lambda b, n: (b * nb + n, 0)),
                  _layer_full(b_ff, layer),
                  pl.BlockSpec((ts, ts), lambda b, n: (0, 0))],
        out_specs=pl.BlockSpec((ts, FOX_HEADS * LANES), lambda b, n: (b * nb + n, 0)),
        out_shape=jax.ShapeDtypeStruct((t, FOX_HEADS * LANES), BF16),
        scratch_shapes=[pltpu.VMEM((8, LANES), F32)],
        compiler_params=_params("parallel", "arbitrary"),
        name="fox_decay_cumsum",
    )(f, b_ff, tri)


def _mla_prep_kernel(mq_ref, mkv_ref, kr_ref, krr_ref, gq_ref, gkv_ref, wq_ref, wqr_ref,
                     wk_ref, wv_ref, cos_ref, sin_ref, q_ref, k_ref, v_ref):
    cos = cos_ref[...]
    sin = sin_ref[...]
    cqn = _rms(mq_ref[...].astype(F32), gq_ref[...]).astype(BF16)
    q = jnp.dot(cqn, wq_ref[...], preferred_element_type=F32)
    qrot = jnp.dot(cqn, wqr_ref[...], preferred_element_type=F32)
    ckvn = _rms(mkv_ref[...].astype(F32), gkv_ref[...]).astype(BF16)
    kn = jnp.dot(ckvn, wk_ref[...], preferred_element_type=F32)
    v_ref[...] = jnp.dot(ckvn, wv_ref[...], preferred_element_type=F32).astype(BF16)
    kr = (kr_ref[...].astype(F32) * cos + krr_ref[...].astype(F32) * sin).astype(BF16)
    for h in range(MLA_HEADS):
        lo = h * MLA_QK_PAD
        q_ref[:, lo:lo + LANES] = q[:, lo:lo + LANES].astype(BF16)
        q_ref[:, lo + LANES:lo + 2 * LANES] = (
            q[:, lo + LANES:lo + 2 * LANES] * cos + qrot[:, h * LANES:(h + 1) * LANES] * sin).astype(BF16)
        k_ref[:, lo:lo + LANES] = kn[:, h * LANES:(h + 1) * LANES].astype(BF16)
        k_ref[:, lo + LANES:lo + 2 * LANES] = kr


def _mla_prep(p, gq, gkv, wq, wqr, wk, wv, cos, sin, layer, seq):
    t = p.shape[0]
    tm = _tile(seq, 512, 8)
    ns = seq // tm
    ql, kvl = wq.shape[1], wk.shape[1]
    row = lambda i: (i, 0)
    return pl.pallas_call(
        _mla_prep_kernel,
        grid=(t // tm,),
        in_specs=[pl.BlockSpec((tm, ql), lambda i: (i, OFF_MQ // ql)),
                  pl.BlockSpec((tm, kvl), lambda i: (i, OFF_MKV // kvl)),
                  pl.BlockSpec((tm, LANES), lambda i: (i, OFF_MKR // LANES)),
                  pl.BlockSpec((tm, LANES), lambda i: (i, OFF_MKRR // LANES)),
                  _layer_full(gq, layer), _layer_full(gkv, layer),
                  _layer_full(wq, layer), _layer_full(wqr, layer),
                  _layer_full(wk, layer), _layer_full(wv, layer),
                  pl.BlockSpec((tm, LANES), lambda i: (i % ns, 0)),
                  pl.BlockSpec((tm, LANES), lambda i: (i % ns, 0))],
        out_specs=[pl.BlockSpec((tm, MLA_HEADS * MLA_QK_PAD), row),
                   pl.BlockSpec((tm, MLA_HEADS * MLA_QK_PAD), row),
                   pl.BlockSpec((tm, MLA_W), row)],
        out_shape=[jax.ShapeDtypeStruct((t, MLA_HEADS * MLA_QK_PAD), BF16),
                   jax.ShapeDtypeStruct((t, MLA_HEADS * MLA_QK_PAD), BF16),
                   jax.ShapeDtypeStruct((t, MLA_W), BF16)],
        compiler_params=_params("parallel"),
        name="mla_prep",
    )(p, p, p, p, gq, gkv, wq, wqr, wk, wv, cos, sin)


def _attn_kernel(qi_tab, ki_tab, *refs, heads, dq, dv, coef, has_bias, tq, tk):
    if has_bias:
        q_ref, k_ref, v_ref, rel_ref, kb_ref, o_ref, m_ref, l_ref, acc_ref = refs
    else:
        q_ref, k_ref, v_ref, rel_ref, o_ref, m_ref, l_ref, acc_ref = refs
    step = pl.program_id(1)
    qi = qi_tab[step]
    ki = ki_tab[step]

    @pl.when(ki == 0)
    def _():
        m_ref[...] = jnp.full_like(m_ref, -jnp.inf)
        l_ref[...] = jnp.zeros_like(l_ref)
        acc_ref[...] = jnp.zeros_like(acc_ref)

    def accumulate(masked):
        ones_v = jnp.ones((tk, LANES), BF16)
        if has_bias:
            lane = lax.broadcasted_iota(jnp.int32, (tq, LANES), 1)
            ones_q = jnp.where(lane < BIAS_TERMS, 1.0, 0.0).astype(BF16)
        if masked:
            visible = rel_ref[...] >= (ki * tk - qi * tq)
        for h in range(heads):
            q = q_ref[:, h * dq:(h + 1) * dq]
            k = k_ref[:, h * dq:(h + 1) * dq]
            if has_bias:
                q = jnp.concatenate([q, ones_q], axis=1)
                k = jnp.concatenate([k, kb_ref[:, h * LANES:(h + 1) * LANES]], axis=1)
            s = lax.dot_general(q, k, (((1,), (1,)), ((), ())), preferred_element_type=F32)
            if masked:
                s = jnp.where(visible, s, -jnp.inf)
            m_prev = m_ref[h]
            m_new = jnp.maximum(m_prev, jnp.max(s, axis=1, keepdims=True))
            alpha = jnp.exp2((m_prev - m_new) * coef)
            p = jnp.concatenate(
                [jnp.exp2((s[:, j * LANES:(j + 1) * LANES] - m_new) * coef) for j in range(tk // LANES)],
                axis=1).astype(BF16)
            v1 = jnp.concatenate([v_ref[:, h * dv:(h + 1) * dv], ones_v], axis=1)
            pv = jnp.dot(p, v1, preferred_element_type=F32)
            acc_ref[:, h * dv:(h + 1) * dv] = alpha * acc_ref[:, h * dv:(h + 1) * dv] + pv[:, :dv]
            l_ref[h] = alpha * l_ref[h] + pv[:, dv:]
            m_ref[h] = m_new

    @pl.when(ki != qi)
    def _():
        accumulate(False)

    @pl.when(ki == qi)
    def _():
        accumulate(True)
        for h in range(heads):
            o_ref[:, h * dv:(h + 1) * dv] = (
                acc_ref[:, h * dv:(h + 1) * dv] / l_ref[h]).astype(o_ref.dtype)


def _attention(q_arr, q_cb, k_arr, k_cb, v_arr, v_cb, rel, kbias, *, batch, seq, heads, dq, dv, scale, name):
    assert dv == LANES
    tq = tk = rel.shape[0]
    nq = seq // tq
    pairs = [(qi, ki) for qi in range(nq) for ki in range(qi + 1)]
    qi_tab = jnp.asarray([p[0] for p in pairs], jnp.int32)
    ki_tab = jnp.asarray([p[1] for p in pairs], jnp.int32)
    has_bias = kbias is not None
    in_specs = [pl.BlockSpec((tq, heads * dq), lambda b, s, qt, kt: (b * nq + qt[s], q_cb)),
                pl.BlockSpec((tk, heads * dq), lambda b, s, qt, kt: (b * nq + kt[s], k_cb)),
                pl.BlockSpec((tk, heads * dv), lambda b, s, qt, kt: (b * nq + kt[s], v_cb)),
                pl.BlockSpec((tq, tk), lambda b, s, qt, kt: (0, 0))]
    args = [q_arr, k_arr, v_arr, rel]
    if has_bias:
        in_specs.append(pl.BlockSpec((tk, heads * LANES), lambda b, s, qt, kt: (b * nq + kt[s], 0)))
        args.append(kbias)
    return pl.pallas_call(
        functools.partial(_attn_kernel, heads=heads, dq=dq, dv=dv, coef=scale * math.log2(math.e),
                          has_bias=has_bias, tq=tq, tk=tk),
        grid_spec=pltpu.PrefetchScalarGridSpec(
            num_scalar_prefetch=2,
            grid=(batch, len(pairs)),
            in_specs=in_specs,
            out_specs=pl.BlockSpec((tq, heads * dv), lambda b, s, qt, kt: (b * nq + qt[s], 0)),
            scratch_shapes=[pltpu.VMEM((heads, tq, LANES), F32),
                            pltpu.VMEM((heads, tq, LANES), F32),
                            pltpu.VMEM((tq, heads * dv), F32)]),
        out_shape=jax.ShapeDtypeStruct((batch * seq, heads * dv), BF16),
        compiler_params=_params("parallel", "arbitrary"),
        name=name,
    )(qi_tab, ki_tab, *args)


def _retention_kernel(q_ref, k_ref, v_ref, g_ref, cos_ref, sin_ref, dec_ref, sin_state_ref,
                      cross_ref, o_ref, state_ref, *, block_decay):
    @pl.when(pl.program_id(1) == 0)
    def _():
        state_ref[...] = jnp.zeros_like(state_ref)

    cos = cos_ref[...]
    sin = sin_ref[...]

    def rope(x):
        return x * cos + pltpu.roll(x, RET_DK // 2, axis=1) * sin

    for h in range(RET_HEADS):
        qs = slice(h * RET_DK, (h + 1) * RET_DK)
        vs = slice(h * RET_DV, (h + 1) * RET_DV)
        q = rope(q_ref[:, qs].astype(F32)).astype(BF16)
        kf = rope(k_ref[:, qs].astype(F32)) * (RET_DK ** -0.5)
        v = v_ref[:, vs]
        scores = lax.dot_general(q, kf.astype(BF16), (((1,), (1,)), ((), ())),
                                 preferred_element_type=F32) * dec_ref[h]
        o = jnp.dot(scores.astype(BF16), v, preferred_element_type=F32)
        state = state_ref[h]
        cross = jnp.dot(q, state.astype(BF16), preferred_element_type=F32)
        cd = cross_ref[h]
        o = o + cross * jnp.concatenate([cd, cd], axis=1)
        o = o * lax.rsqrt(jnp.mean(o * o, axis=-1, keepdims=True) + NORM_EPS)
        g = g_ref[:, vs].astype(F32)
        o_ref[:, vs] = (o * (g * jax.nn.sigmoid(g))).astype(o_ref.dtype)
        ks = (kf * sin_state_ref[h]).astype(BF16)
        kv = lax.dot_general(ks, v, (((0,), (0,)), ((), ())), preferred_element_type=F32)
        state_ref[h] = block_decay[h] * state + kv


def _retention(p, cos, sin, dec, sin_state, cross, block_decay, batch, seq):
    blk = dec.shape[1]
    nb = seq // blk
    row = lambda b, n: (b * nb + n, 0)
    const3 = lambda b, n: (0, 0, 0)
    return pl.pallas_call(
        functools.partial(_retention_kernel, block_decay=block_decay),
        grid=(batch, nb),
        in_specs=[pl.BlockSpec((blk, RET_QK_W), lambda b, n: (b * nb + n, OFF_RQ // RET_QK_W)),
                  pl.BlockSpec((blk, RET_QK_W), lambda b, n: (b * nb + n, OFF_RK // RET_QK_W)),
                  pl.BlockSpec((blk, RET_V_W), lambda b, n: (b * nb + n, OFF_RV // RET_V_W)),
                  pl.BlockSpec((blk, RET_V_W), lambda b, n: (b * nb + n, OFF_RG // RET_V_W)),
                  pl.BlockSpec((blk, RET_DK), lambda b, n: (n, 0)),
                  pl.BlockSpec((blk, RET_DK), lambda b, n: (n, 0)),
                  pl.BlockSpec(dec.shape, const3),
                  pl.BlockSpec(sin_state.shape, const3),
                  pl.BlockSpec(cross.shape, const3)],
        out_specs=pl.BlockSpec((blk, RET_V_W), row),
        out_shape=jax.ShapeDtypeStruct((batch * seq, RET_V_W), BF16),
        scratch_shapes=[pltpu.VMEM((RET_HEADS, RET_DK, RET_DV), F32)],
        compiler_params=_params("parallel", "arbitrary"),
        name="retention",
    )(p, p, p, p, cos, sin, dec, sin_state, cross)


def _gated_merge_kernel(x_ref, g_ref, a_ref, b_ref, c_ref, wg0_ref, wg1_ref, wg2_ref,
                        wa_ref, wb_ref, wc_ref, o_ref, h_ref):
    @pl.when(pl.program_id(1) == 0)
    def _():
        _norm_rows(x_ref, g_ref, h_ref, x_ref.shape[0])

    h = h_ref[...]

    def branch(wg_ref, br_ref, w_ref):
        gate = jax.nn.sigmoid(jnp.dot(h, wg_ref[...], preferred_element_type=F32))
        return gate * jnp.dot(br_ref[...], w_ref[...], preferred_element_type=F32)

    m = branch(wg0_ref, a_ref, wa_ref) + branch(wg1_ref, b_ref, wb_ref) + branch(wg2_ref, c_ref, wc_ref)
    o_ref[...] = m.astype(o_ref.dtype)


def _gated_merge(x, g, a, bm, c, w_gates, wa, wb, wc, layer):
    t, d = x.shape
    tm = _tile(t, 512, 8)
    tn = _tile(d, 512, LANES)
    nj = d // tn
    act = lambda arr: pl.BlockSpec((tm, arr.shape[1]), lambda i, j: (i, 0))
    wgt = lambda w: _layer_spec(w, layer, (w.shape[1], tn), lambda i, j: (0, j))
    gate = lambda n: _layer_spec(w_gates, layer, (d, tn), lambda i, j: (0, n * nj + j))
    return pl.pallas_call(
        _gated_merge_kernel,
        grid=(t // tm, nj),
        in_specs=[pl.BlockSpec((tm, d), lambda i, j: (i, 0)), _layer_full(g, layer),
                  act(a), act(bm), act(c), gate(0), gate(1), gate(2), wgt(wa), wgt(wb), wgt(wc)],
        out_specs=pl.BlockSpec((tm, tn), lambda i, j: (i, j)),
        out_shape=jax.ShapeDtypeStruct((t, d), BF16),
        scratch_shapes=[pltpu.VMEM((tm, d), BF16)],
        compiler_params=_params("parallel", "arbitrary"),
        name="gated_merge",
    )(x, g, a, bm, c, w_gates, w_gates, w_gates, wa, wb, wc)


def _mm_res_kernel(a_ref, w_ref, r_ref, o_ref):
    o_ref[...] = r_ref[...] + jnp.dot(a_ref[...], w_ref[...], preferred_element_type=F32)


def _matmul_residual(a, w, r, layer, tm_pref, tn_pref, name):
    t, k = a.shape
    n = w.shape[2]
    tm = _tile(t, tm_pref, 8)
    tn = _tile(n, tn_pref, LANES)
    return pl.pallas_call(
        _mm_res_kernel,
        grid=(n // tn, t // tm),
        in_specs=[pl.BlockSpec((tm, k), lambda j, i: (i, 0)),
                  _layer_spec(w, layer, (k, tn), lambda j, i: (0, j)),
                  pl.BlockSpec((tm, tn), lambda j, i: (i, j))],
        out_specs=pl.BlockSpec((tm, tn), lambda j, i: (i, j)),
        out_shape=jax.ShapeDtypeStruct((t, n), F32),
        compiler_params=_params("parallel", "arbitrary"),
        name=name,
    )(a, w, r)


def _ffn_up_kernel(x_ref, xh_ref, g_ref, wu_ref, wg_ref, cw_ref, cb_ref, o_ref, h_ref, *, seq):
    tm = x_ref.shape[0]

    @pl.when(pl.program_id(1) == 0)
    def _():
        keep = ((pl.program_id(0) * tm) % seq != 0).astype(F32)
        h_ref[0:HALO, :] = (_rms(xh_ref[...], g_ref[...]) * keep).astype(BF16)
        _norm_rows(x_ref, g_ref, h_ref, tm, h_off=HALO)

    u = jnp.dot(h_ref[...], wu_ref[...], preferred_element_type=F32)
    gate = jnp.dot(h_ref[HALO:, :], wg_ref[...], preferred_element_type=F32)
    conv = (cb_ref[...] + cw_ref[2:3, :] * u
            + cw_ref[1:2, :] * pltpu.roll(u, 1, axis=0)
            + cw_ref[0:1, :] * pltpu.roll(u, 2, axis=0))
    o_ref[...] = (jax.nn.gelu(conv[HALO:, :]) * gate).astype(o_ref.dtype)


def _ffn_up(x, g, wu, wg, cw, cb, layer, seq):
    t, d = x.shape
    f = wu.shape[2]
    tm = _tile(seq, 1024, HALO)
    tf = _tile(f, 512, LANES)
    hb = tm // HALO
    col = lambda i, j: (0, j)
    return pl.pallas_call(
        functools.partial(_ffn_up_kernel, seq=seq),
        grid=(t // tm, f // tf),
        in_specs=[pl.BlockSpec((tm, d), lambda i, j: (i, 0)),
                  pl.BlockSpec((HALO, d), lambda i, j: (jnp.maximum(i * hb - 1, 0), 0)),
                  _layer_full(g, layer),
                  _layer_spec(wu, layer, (d, tf), col),
                  _layer_spec(wg, layer, (d, tf), col),
                  _layer_spec(cw, layer, (8, tf), col),
                  _layer_spec(cb, layer, (1, tf), col)],
        out_specs=pl.BlockSpec((tm, tf), lambda i, j: (i, j)),
        out_shape=jax.ShapeDtypeStruct((t, f), BF16),
        scratch_shapes=[pltpu.VMEM((HALO + tm, d), BF16)],
        compiler_params=_params("parallel", "arbitrary"),
        name="ffn_up_conv_gate",
    )(x, x, g, wu, wg, cw, cb)


def _final_norm_kernel(x_ref, g_ref, o_ref):
    o_ref[...] = _rms(x_ref[...], g_ref[...])


def _final_norm(x, g):
    t, d = x.shape
    tm = _tile(t, 512, 8)
    return pl.pallas_call(
        _final_norm_kernel,
        grid=(t // tm,),
        in_specs=[pl.BlockSpec((tm, d), lambda i: (i, 0)), pl.BlockSpec((1, d), lambda i: (0, 0))],
        out_specs=pl.BlockSpec((tm, d), lambda i: (i, 0)),
        out_shape=jax.ShapeDtypeStruct((t, d), F32),
        compiler_params=_params("parallel"),
        name="final_norm",
    )(x, g)


def _rope_tables(seq, d):
    pos = jnp.arange(seq, dtype=F32)
    inv_freq = ROPE_THETA ** (-jnp.arange(0, d, 2, dtype=F32) / d)
    ang = pos[:, None] * inv_freq[None, :]
    return jnp.cos(ang), jnp.sin(ang)


def _retention_tables(blk):
    log_gamma = jnp.log(1.0 - 2.0 ** (-5.0 - jnp.arange(RET_HEADS, dtype=F32)))
    idx = jnp.arange(blk, dtype=F32)
    chunk = jnp.arange(blk) // CHUNK
    dist = jnp.abs(idx[:, None] - idx[None, :])
    dec = jnp.where(chunk[None, :] <= chunk[:, None],
                    jnp.exp(log_gamma[:, None, None] * dist[None]), 0.0)
    sin_state = jnp.exp(log_gamma[:, None] * (blk - 1 - idx)[None, :])
    cross = jnp.exp(log_gamma[:, None] * (idx + 1.0)[None, :])
    bcast = lambda a: jnp.broadcast_to(a[:, :, None], (RET_HEADS, blk, RET_DK))
    block_decay = tuple(math.exp(math.log(1.0 - 2.0 ** (-5.0 - h)) * blk) for h in range(RET_HEADS))
    return dec, bcast(sin_state), bcast(cross), block_decay


def _rotate_half_cols(w):
    half = w.shape[-1] // 2
    return jnp.concatenate([-w[..., half:], w[..., :half]], axis=-1)


def kernel(x, norm1_g, w_in, mla_q_norm_g, mla_kv_norm_g, mla_w_uq, mla_w_ukv, fox_b_f,
           w_br_fox, w_br_mla, w_br_ret, w_out, norm2_g, ffn_w_up, ffn_w_gate,
           ffn_conv_w, ffn_conv_b, ffn_w_down, final_norm_g):
    batch, seq, d = x.shape
    depth = w_in.shape[0]
    t = batch * seq
    q_lora = mla_w_uq.shape[1]
    kv_lora = mla_w_ukv.shape[1]

    widths = (FOX_W, FOX_W, FOX_W, FOX_HEADS, q_lora, kv_lora, MLA_ROPE,
              RET_QK_W, RET_QK_W, RET_V_W, RET_V_W, N_BRANCH * d)
    offs = [0]
    for wd in widths:
        offs.append(offs[-1] + wd)
    seg = lambda k: w_in[:, :, offs[k]:offs[k + 1]]
    zeros = lambda n: jnp.zeros((depth, d, n), w_in.dtype)
    w_p = jnp.concatenate(
        [seg(0), seg(1), seg(2), seg(5), seg(4), seg(9), seg(10), seg(7), seg(8),
         seg(6), zeros(LANES - MLA_ROPE), _rotate_half_cols(seg(6)), zeros(LANES - MLA_ROPE)],
        axis=-1).astype(BF16)
    assert w_p.shape[-1] == P_WIDTH and q_lora == 512 and kv_lora == 256
    w_gates = seg(11).astype(BF16)
    w_ff = jnp.concatenate([seg(3), zeros(LANES - FOX_HEADS)], axis=-1).astype(BF16)
    b_ff = jnp.pad(fox_b_f, ((0, 0), (0, LANES - FOX_HEADS)))[:, None, :]

    uq = mla_w_uq.reshape(depth, q_lora, MLA_HEADS, MLA_NOPE + MLA_ROPE)
    uq_rope = uq[..., MLA_NOPE:]
    pad_q = jnp.zeros((depth, q_lora, MLA_HEADS, MLA_QK_PAD - MLA_NOPE - MLA_ROPE), uq.dtype)
    w_q = jnp.concatenate([uq, pad_q], axis=-1).reshape(depth, q_lora, -1).astype(BF16)
    pad_r = jnp.zeros((depth, q_lora, MLA_HEADS, LANES - MLA_ROPE), uq.dtype)
    w_qr = jnp.concatenate([_rotate_half_cols(uq_rope), pad_r], axis=-1).reshape(depth, q_lora, -1).astype(BF16)
    ukv = mla_w_ukv.reshape(depth, kv_lora, MLA_HEADS, MLA_NOPE + MLA_V)
    w_k = ukv[..., :MLA_NOPE].reshape(depth, kv_lora, -1).astype(BF16)
    w_v = ukv[..., MLA_NOPE:].reshape(depth, kv_lora, -1).astype(BF16)

    w_bf, w_bm, w_br = w_br_fox.astype(BF16), w_br_mla.astype(BF16), w_br_ret.astype(BF16)
    w_o = w_out.astype(BF16)
    w_up, w_gate, w_down = ffn_w_up.astype(BF16), ffn_w_gate.astype(BF16), ffn_w_down.astype(BF16)
    conv_w = jnp.pad(ffn_conv_w, ((0, 0), (0, 8 - CONV_W), (0, 0)))
    conv_b = ffn_conv_b[:, None, :]
    g1, g2 = norm1_g[:, None, :], norm2_g[:, None, :]
    gq, gkv = mla_q_norm_g[:, None, :], mla_kv_norm_g[:, None, :]

    cos64, sin64 = _rope_tables(seq, MLA_ROPE)
    zpad = jnp.zeros((seq, LANES - MLA_ROPE), F32)
    mla_cos = jnp.concatenate([cos64, cos64, zpad], axis=1)
    mla_sin = jnp.concatenate([sin64, sin64, zpad], axis=1)
    cos128, sin128 = _rope_tables(seq, RET_DK)
    ret_cos = jnp.concatenate([cos128, cos128], axis=1)
    ret_sin = jnp.concatenate([-sin128, sin128], axis=1)
    ret_blk = _tile(seq, 512, CHUNK)
    dec, sin_state, cross, block_decay = _retention_tables(ret_blk)

    att_blk = _tile(seq, 512, CHUNK)
    row = jnp.arange(att_blk, dtype=jnp.int32)[:, None]
    col = jnp.arange(att_blk, dtype=jnp.int32)[None, :]
    rel_frame = row - col
    rel_chunk = (row // CHUNK) * CHUNK + (CHUNK - 1) - col

    xf = x.reshape(t, d)
    for i in range(depth):
        p, f_logit = _in_proj(xf, g1, w_p, w_ff, i)
        kbias = _fox_decay(f_logit, b_ff, i, batch, seq)
        a = _attention(p, OFF_FQ // FOX_W, p, OFF_FK // FOX_W, p, OFF_FV // FOX_W, rel_frame, kbias,
                       batch=batch, seq=seq, heads=FOX_HEADS, dq=FOX_DH, dv=FOX_DH,
                       scale=FOX_DH ** -0.5, name="fox_attention")
        qm, km, vm = _mla_prep(p, gq, gkv, w_q, w_qr, w_k, w_v, mla_cos, mla_sin, i, seq)
        bm = _attention(qm, 0, km, 0, vm, 0, rel_chunk, None,
                        batch=batch, seq=seq, heads=MLA_HEADS, dq=MLA_QK_PAD, dv=MLA_V,
                        scale=(MLA_NOPE + MLA_ROPE) ** -0.5, name="mla_attention")
        cr = _retention(p, ret_cos, ret_sin, dec, sin_state, cross, block_decay, batch, seq)
        merged = _gated_merge(xf, g1, a, bm, cr, w_gates, w_bf, w_bm, w_br, i)
        xf = _matmul_residual(merged, w_o, xf, i, 1024, 1024, "out_proj")
        act = _ffn_up(xf, g2, w_up, w_gate, conv_w, conv_b, i, seq)
        xf = _matmul_residual(act, w_down, xf, i, 512, 1024, "ffn_down")
    return _final_norm(xf, final_norm_g[None, :]).reshape(batch, seq, d)
```

```python
import functools
import math

import jax
import jax.numpy as jnp
from jax import lax
from jax.experimental import pallas as pl
from jax.experimental.pallas import tpu as pltpu

F32 = jnp.float32
BF16 = jnp.bfloat16

NORM_EPS = 1e-6
ROPE_THETA = 10000.0
CHUNK = 64

FOX_HEADS, FOX_DH = 6, 128
MLA_HEADS, MLA_NOPE, MLA_ROPE, MLA_V = 6, 128, 64, 128
MLA_QK_PAD = 256
RET_HEADS, RET_DK, RET_DV = 4, 128, 256
N_BRANCH = 3
CONV_W = 3
LANES = 128
HALO = 16
BIAS_TERMS = 3
ATTN_HEAD_GROUPS = 1

FOX_W = FOX_HEADS * FOX_DH
MLA_W = MLA_HEADS * MLA_V
RET_QK_W = RET_HEADS * RET_DK
RET_V_W = RET_HEADS * RET_DV

OFF_FQ, OFF_FK, OFF_FV = 0, FOX_W, 2 * FOX_W
OFF_MKV = 3 * FOX_W
OFF_MQ = OFF_MKV + 256
OFF_RV = OFF_MQ + 512
OFF_RG = OFF_RV + RET_V_W
OFF_RQ = OFF_RG + RET_V_W
OFF_RK = OFF_RQ + RET_QK_W
OFF_MKR = OFF_RK + RET_QK_W
FF_LANE = MLA_ROPE
OFF_MKRR = OFF_MKR + LANES
P_WIDTH = OFF_MKRR + LANES

VMEM_LIMIT_BYTES = 56 * 1024 * 1024


def _tile(n, pref, align):
    t = (min(pref, n) // align) * align
    while t > align and n % t:
        t -= align
    assert t >= align and n % t == 0, (n, pref, align)
    return t


def _params(*sem):
    return pltpu.CompilerParams(dimension_semantics=sem, vmem_limit_bytes=VMEM_LIMIT_BYTES)


def _layer_spec(arr, layer, block, index):
    return pl.BlockSpec((None,) + tuple(block), lambda *ids: (layer,) + tuple(index(*ids)))


def _layer_full(arr, layer):
    zeros = (0,) * (arr.ndim - 1)
    return pl.BlockSpec((None,) + arr.shape[1:], lambda *ids: (layer,) + zeros)


def _rms(x, g):
    return x * lax.rsqrt(jnp.mean(x * x, axis=-1, keepdims=True) + NORM_EPS) * g


def _norm_rows(x_ref, g_ref, h_ref, rows, h_off=0):
    rc = _tile(rows, 256, 8)

    def body(c, carry):
        r0 = pl.multiple_of(c * rc, rc)
        h_ref[pl.ds(h_off + r0, rc), :] = _rms(x_ref[pl.ds(r0, rc), :], g_ref[...]).astype(BF16)
        return carry

    lax.fori_loop(0, rows // rc, body, 0)


def _in_proj_kernel(x_ref, g_ref, w_ref, o_ref, f_ref, h_ref, *, f_tile, f_col):
    @pl.when(pl.program_id(1) == 0)
    def _():
        _norm_rows(x_ref, g_ref, h_ref, x_ref.shape[0])

    y = jnp.dot(h_ref[...], w_ref[...], preferred_element_type=F32)
    o_ref[...] = y.astype(o_ref.dtype)

    @pl.when(pl.program_id(1) == f_tile)
    def _():
        f_ref[...] = y[:, f_col:f_col + LANES]


def _in_proj(x, g, w, layer):
    t, d = x.shape
    n = w.shape[2]
    tm = _tile(t, 1024, 8)
    tn = _tile(n, 1280, LANES)
    return pl.pallas_call(
        functools.partial(_in_proj_kernel, f_tile=OFF_MKR // tn, f_col=OFF_MKR % tn),
        grid=(t // tm, n // tn),
        in_specs=[pl.BlockSpec((tm, d), lambda i, j: (i, 0)),
                  _layer_full(g, layer),
                  _layer_spec(w, layer, (d, tn), lambda i, j: (0, j))],
        out_specs=[pl.BlockSpec((tm, tn), lambda i, j: (i, j)),
                   pl.BlockSpec((tm, LANES), lambda i, j: (i, 0))],
        out_shape=[jax.ShapeDtypeStruct((t, n), BF16), jax.ShapeDtypeStruct((t, LANES), F32)],
        scratch_shapes=[pltpu.VMEM((tm, d), BF16)],
        compiler_params=_params("parallel", "arbitrary"),
        name="in_proj",
    )(x, g, w)


def _fox_decay_kernel(f_ref, b_ref, tri_ref, o_ref, carry_ref):
    @pl.when(pl.program_id(1) == 0)
    def _():
        carry_ref[...] = jnp.zeros_like(carry_ref)

    f = f_ref[...] + b_ref[...]
    lf = jnp.minimum(f, 0.0) - jnp.log1p(jnp.exp(-jnp.abs(f)))
    hi = lf.astype(BF16)
    r1 = lf - hi.astype(F32)
    mid = r1.astype(BF16)
    lo = (r1 - mid.astype(F32)).astype(BF16)
    tri = tri_ref[...]
    c = (jnp.dot(tri, hi, preferred_element_type=F32)
         + jnp.dot(tri, mid, preferred_element_type=F32)
         + jnp.dot(tri, lo, preferred_element_type=F32)) + carry_ref[0:1, :]
    carry_ref[...] = jnp.broadcast_to(c[-1:, :], carry_ref.shape)
    lane = lax.broadcasted_iota(jnp.int32, (c.shape[0], LANES), 1)
    for h in range(FOX_HEADS):
        src = FF_LANE + h
        rest = jnp.broadcast_to(c[:, src:src + 1], lane.shape) * (-(FOX_DH ** 0.5))
        slab = jnp.zeros(lane.shape, F32)
        for term in range(BIAS_TERMS):
            part = rest.astype(BF16).astype(F32)
            slab = jnp.where(lane == term, part, slab)
            rest = rest - part
        o_ref[:, h * LANES:(h + 1) * LANES] = slab.astype(BF16)


def _fox_decay(f, b_ff, layer, batch, seq):
    t = f.shape[0]
    ts = _tile(seq, 1024, 8)
    nb = seq // ts
    tri = (jnp.arange(ts)[:, None] >= jnp.arange(ts)[None, :]).astype(BF16)
    return pl.pallas_call(
        _fox_decay_kernel,
        grid=(batch, nb),
        in_specs=[pl.BlockSpec((ts, LANES), lambda b, n: (b * nb + n, 0)),
                  _layer_full(b_ff, layer),
                  pl.BlockSpec((ts, ts), lambda b, n: (0, 0))],
        out_specs=pl.BlockSpec((ts, FOX_HEADS * LANES), lambda b, n: (b * nb + n, 0)),
        out_shape=jax.ShapeDtypeStruct((t, FOX_HEADS * LANES), BF16),
        scratch_shapes=[pltpu.VMEM((8, LANES), F32)],
        compiler_params=_params("parallel", "arbitrary"),
        name="fox_decay_cumsum",
    )(f, b_ff, tri)


def _mla_prep_kernel(mq_ref, mkv_ref, kr_ref, krr_ref, gq_ref, gkv_ref, wq_ref, wqr_ref,
                     wk_ref, wv_ref, cos_ref, sin_ref, q_ref, k_ref, v_ref):
    cos = cos_ref[...]
    sin = sin_ref[...]
    cqn = _rms(mq_ref[...].astype(F32), gq_ref[...]).astype(BF16)
    q = jnp.dot(cqn, wq_ref[...], preferred_element_type=F32)
    qrot = jnp.dot(cqn, wqr_ref[...], preferred_element_type=F32)
    ckvn = _rms(mkv_ref[...].astype(F32), gkv_ref[...]).astype(BF16)
    kn = jnp.dot(ckvn, wk_ref[...], preferred_element_type=F32)
    v_ref[...] = jnp.dot(ckvn, wv_ref[...], preferred_element_type=F32).astype(BF16)
    kr = (kr_ref[...].astype(F32) * cos + krr_ref[...].astype(F32) * sin).astype(BF16)
    for h in range(MLA_HEADS):
        lo = h * MLA_QK_PAD
        q_ref[:, lo:lo + LANES] = q[:, lo:lo + LANES].astype(BF16)
        q_ref[:, lo + LANES:lo + 2 * LANES] = (
            q[:, lo + LANES:lo + 2 * LANES] * cos + qrot[:, h * LANES:(h + 1) * LANES] * sin).astype(BF16)
        k_ref[:, lo:lo + LANES] = kn[:, h * LANES:(h + 1) * LANES].astype(BF16)
        k_ref[:, lo + LANES:lo + 2 * LANES] = kr


def _mla_prep(p, gq, gkv, wq, wqr, wk, wv, cos, sin, layer, seq):
    t = p.shape[0]
    tm = _tile(seq, 512, 8)
    ns = seq // tm
    ql, kvl = wq.shape[1], wk.shape[1]
    row = lambda i: (i, 0)
    return pl.pallas_call(
        _mla_prep_kernel,
        grid=(t // tm,),
        in_specs=[pl.BlockSpec((tm, ql), lambda i: (i, OFF_MQ // ql)),
                  pl.BlockSpec((tm, kvl), lambda i: (i, OFF_MKV // kvl)),
                  pl.BlockSpec((tm, LANES), lambda i: (i, OFF_MKR // LANES)),
                  pl.BlockSpec((tm, LANES), lambda i: (i, OFF_MKRR // LANES)),
                  _layer_full(gq, layer), _layer_full(gkv, layer),
                  _layer_full(wq, layer), _layer_full(wqr, layer),
                  _layer_full(wk, layer), _layer_full(wv, layer),
                  pl.BlockSpec((tm, LANES), lambda i: (i % ns, 0)),
                  pl.BlockSpec((tm, LANES), lambda i: (i % ns, 0))],
        out_specs=[pl.BlockSpec((tm, MLA_HEADS * MLA_QK_PAD), row),
                   pl.BlockSpec((tm, MLA_HEADS * MLA_QK_PAD), row),
                   pl.BlockSpec((tm, MLA_W), row)],
        out_shape=[jax.ShapeDtypeStruct((t, MLA_HEADS * MLA_QK_PAD), BF16),
                   jax.ShapeDtypeStruct((t, MLA_HEADS * MLA_QK_PAD), BF16),
                   jax.ShapeDtypeStruct((t, MLA_W), BF16)],
        compiler_params=_params("parallel"),
        name="mla_prep",
    )(p, p, p, p, gq, gkv, wq, wqr, wk, wv, cos, sin)


def _attn_kernel(*refs, heads, dq, dv, coef, has_bias, tq):
    if has_bias:
        q_ref, k_ref, v_ref, rel_ref, kb_ref, o_ref, m_ref, l_ref, acc_ref = refs
    else:
        q_ref, k_ref, v_ref, rel_ref, o_ref, m_ref, l_ref, acc_ref = refs
    qi = pl.program_id(2)
    m_ref[...] = jnp.full_like(m_ref, -jnp.inf)
    l_ref[...] = jnp.zeros_like(l_ref)
    acc_ref[...] = jnp.zeros_like(acc_ref)

    def accumulate(k0, visible):
        rows = pl.ds(k0, tq)
        ones_v = jnp.ones((tq, LANES), BF16)
        if has_bias:
            lane = lax.broadcasted_iota(jnp.int32, (tq, LANES), 1)
            ones_q = jnp.where(lane < BIAS_TERMS, 1.0, 0.0).astype(BF16)
        for h in range(heads):
            q = q_ref[:, h * dq:(h + 1) * dq]
            k = k_ref[rows, h * dq:(h + 1) * dq]
            if has_bias:
                q = jnp.concatenate([q, ones_q], axis=1)
                k = jnp.concatenate([k, kb_ref[rows, h * LANES:(h + 1) * LANES]], axis=1)
            s = lax.dot_general(q, k, (((1,), (1,)), ((), ())), preferred_element_type=F32)
            if visible is not None:
                s = jnp.where(visible, s, -jnp.inf)
            m_prev = m_ref[h]
            m_new = jnp.maximum(m_prev, jnp.max(s, axis=1, keepdims=True))
            alpha = jnp.exp2((m_prev - m_new) * coef)
            p = jnp.concatenate(
                [jnp.exp2((s[:, j * LANES:(j + 1) * LANES] - m_new) * coef) for j in range(tq // LANES)],
                axis=1).astype(BF16)
            v1 = jnp.concatenate([v_ref[rows, h * dv:(h + 1) * dv], ones_v], axis=1)
            pv = jnp.dot(p, v1, preferred_element_type=F32)
            acc_ref[:, h * dv:(h + 1) * dv] = alpha * acc_ref[:, h * dv:(h + 1) * dv] + pv[:, :dv]
            l_ref[h] = alpha * l_ref[h] + pv[:, dv:]
            m_ref[h] = m_new

    def below_diagonal(ki, carry):
        accumulate(pl.multiple_of(ki * tq, tq), None)
        return carry

    lax.fori_loop(0, qi, below_diagonal, 0)
    accumulate(pl.multiple_of(qi * tq, tq), rel_ref[...] >= 0)
    for h in range(heads):
        o_ref[:, h * dv:(h + 1) * dv] = (acc_ref[:, h * dv:(h + 1) * dv] / l_ref[h]).astype(o_ref.dtype)


def _attention(q_arr, q_cb, k_arr, k_cb, v_arr, v_cb, rel, kbias, *, batch, seq, heads, dq, dv, scale, name):
    assert dv == LANES and heads % ATTN_HEAD_GROUPS == 0
    tq = rel.shape[0]
    nq = seq // tq
    ng = ATTN_HEAD_GROUPS
    hg = heads // ng
    has_bias = kbias is not None
    in_specs = [pl.BlockSpec((tq, hg * dq), lambda b, g, qi: (b * nq + qi, q_cb * ng + g)),
                pl.BlockSpec((seq, hg * dq), lambda b, g, qi: (b, k_cb * ng + g)),
                pl.BlockSpec((seq, hg * dv), lambda b, g, qi: (b, v_cb * ng + g)),
                pl.BlockSpec((tq, tq), lambda b, g, qi: (0, 0))]
    args = [q_arr, k_arr, v_arr, rel]
    if has_bias:
        in_specs.append(pl.BlockSpec((seq, hg * LANES), lambda b, g, qi: (b, g)))
        args.append(kbias)
    return pl.pallas_call(
        functools.partial(_attn_kernel, heads=hg, dq=dq, dv=dv, coef=scale * math.log2(math.e),
                          has_bias=has_bias, tq=tq),
        grid=(batch, ng, nq),
        in_specs=in_specs,
        out_specs=pl.BlockSpec((tq, hg * dv), lambda b, g, qi: (b * nq + qi, g)),
        out_shape=jax.ShapeDtypeStruct((batch * seq, heads * dv), BF16),
        scratch_shapes=[pltpu.VMEM((hg, tq, LANES), F32),
                        pltpu.VMEM((hg, tq, LANES), F32),
                        pltpu.VMEM((tq, hg * dv), F32)],
        compiler_params=_params("parallel", "parallel", "arbitrary"),
        name=name,
    )(*args)


def _retention_kernel(q_ref, k_ref, v_ref, g_ref, cos_ref, sin_ref, dec_ref, sin_state_ref,
                      cross_ref, o_ref, state_ref, *, block_decay):
    @pl.when(pl.program_id(1) == 0)
    def _():
        state_ref[...] = jnp.zeros_like(state_ref)

    cos = cos_ref[...]
    sin = sin_ref[...]

    def rope(x):
        return x * cos + pltpu.roll(x, RET_DK // 2, axis=1) * sin

    for h in range(RET_HEADS):
        qs = slice(h * RET_DK, (h + 1) * RET_DK)
        vs = slice(h * RET_DV, (h + 1) * RET_DV)
        q = rope(q_ref[:, qs].astype(F32)).astype(BF16)
        kf = rope(k_ref[:, qs].astype(F32)) * (RET_DK ** -0.5)
        v = v_ref[:, vs]
        scores = lax.dot_general(q, kf.astype(BF16), (((1,), (1,)), ((), ())),
                                 preferred_element_type=F32) * dec_ref[h]
        o = jnp.dot(scores.astype(BF16), v, preferred_element_type=F32)
        state = state_ref[h]
        cross = jnp.dot(q, state.astype(BF16), preferred_element_type=F32)
        cd = cross_ref[h]
        o = o + cross * jnp.concatenate([cd, cd], axis=1)
        o = o * lax.rsqrt(jnp.mean(o * o, axis=-1, keepdims=True) + NORM_EPS)
        g = g_ref[:, vs].astype(F32)
        o_ref[:, vs] = (o * (g * jax.nn.sigmoid(g))).astype(o_ref.dtype)
        ks = (kf * sin_state_ref[h]).astype(BF16)
        kv = lax.dot_general(ks, v, (((0,), (0,)), ((), ())), preferred_element_type=F32)
        state_ref[h] = block_decay[h] * state + kv


def _retention(p, cos, sin, dec, sin_state, cross, block_decay, batch, seq):
    blk = dec.shape[1]
    nb = seq // blk
    row = lambda b, n: (b * nb + n, 0)
    const3 = lambda b, n: (0, 0, 0)
    return pl.pallas_call(
        functools.partial(_retention_kernel, block_decay=block_decay),
        grid=(batch, nb),
        in_specs=[pl.BlockSpec((blk, RET_QK_W), lambda b, n: (b * nb + n, OFF_RQ // RET_QK_W)),
                  pl.BlockSpec((blk, RET_QK_W), lambda b, n: (b * nb + n, OFF_RK // RET_QK_W)),
                  pl.BlockSpec((blk, RET_V_W), lambda b, n: (b * nb + n, OFF_RV // RET_V_W)),
                  pl.BlockSpec((blk, RET_V_W), lambda b, n: (b * nb + n, OFF_RG // RET_V_W)),
                  pl.BlockSpec((blk, RET_DK), lambda b, n: (n, 0)),
                  pl.BlockSpec((blk, RET_DK), lambda b, n: (n, 0)),
                  pl.BlockSpec(dec.shape, const3),
                  pl.BlockSpec(sin_state.shape, const3),
                  pl.BlockSpec(cross.shape, const3)],
        out_specs=pl.BlockSpec((blk, RET_V_W), row),
        out_shape=jax.ShapeDtypeStruct((batch * seq, RET_V_W), BF16),
        scratch_shapes=[pltpu.VMEM((RET_HEADS, RET_DK, RET_DV), F32)],
        compiler_params=_params("parallel", "arbitrary"),
        name="retention",
    )(p, p, p, p, cos, sin, dec, sin_state, cross)


def _gated_merge_kernel(x_ref, g_ref, a_ref, b_ref, c_ref, wg0_ref, wg1_ref, wg2_ref,
                        wa_ref, wb_ref, wc_ref, o_ref, h_ref):
    @pl.when(pl.program_id(1) == 0)
    def _():
        _norm_rows(x_ref, g_ref, h_ref, x_ref.shape[0])

    h = h_ref[...]

    def branch(wg_ref, br_ref, w_ref):
        gate = jax.nn.sigmoid(jnp.dot(h, wg_ref[...], preferred_element_type=F32))
        return gate * jnp.dot(br_ref[...], w_ref[...], preferred_element_type=F32)

    m = branch(wg0_ref, a_ref, wa_ref) + branch(wg1_ref, b_ref, wb_ref) + branch(wg2_ref, c_ref, wc_ref)
    o_ref[...] = m.astype(o_ref.dtype)


def _gated_merge(x, g, a, bm, c, w_gates, wa, wb, wc, layer):
    t, d = x.shape
    tm = _tile(t, 512, 8)
    tn = _tile(d, 512, LANES)
    nj = d // tn
    act = lambda arr: pl.BlockSpec((tm, arr.shape[1]), lambda i, j: (i, 0))
    wgt = lambda w: _layer_spec(w, layer, (w.shape[1], tn), lambda i, j: (0, j))
    gate = lambda n: _layer_spec(w_gates, layer, (d, tn), lambda i, j: (0, n * nj + j))
    return pl.pallas_call(
        _gated_merge_kernel,
        grid=(t // tm, nj),
        in_specs=[pl.BlockSpec((tm, d), lambda i, j: (i, 0)), _layer_full(g, layer),
                  act(a), act(bm), act(c), gate(0), gate(1), gate(2), wgt(wa), wgt(wb), wgt(wc)],
        out_specs=pl.BlockSpec((tm, tn), lambda i, j: (i, j)),
        out_shape=jax.ShapeDtypeStruct((t, d), BF16),
        scratch_shapes=[pltpu.VMEM((tm, d), BF16)],
        compiler_params=_params("parallel", "arbitrary"),
        name="gated_merge",
    )(x, g, a, bm, c, w_gates, w_gates, w_gates, wa, wb, wc)


def _mm_res_kernel(a_ref, w_ref, r_ref, o_ref):
    o_ref[...] = r_ref[...] + jnp.dot(a_ref[...], w_ref[...], preferred_element_type=F32)


def _matmul_residual(a, w, r, layer, tm_pref, tn_pref, name):
    t, k = a.shape
    n = w.shape[2]
    tm = _tile(t, tm_pref, 8)
    tn = _tile(n, tn_pref, LANES)
    return pl.pallas_call(
        _mm_res_kernel,
        grid=(n // tn, t // tm),
        in_specs=[pl.BlockSpec((tm, k), lambda j, i: (i, 0)),
                  _layer_spec(w, layer, (k, tn), lambda j, i: (0, j)),
                  pl.BlockSpec((tm, tn), lambda j, i: (i, j))],
        out_specs=pl.BlockSpec((tm, tn), lambda j, i: (i, j)),
        out_shape=jax.ShapeDtypeStruct((t, n), F32),
        compiler_params=_params("parallel", "arbitrary"),
        name=name,
    )(a, w, r)


def _ffn_up_kernel(x_ref, xh_ref, g_ref, wu_ref, wg_ref, cw_ref, cb_ref, o_ref, h_ref, *, seq):
    tm = x_ref.shape[0]

    @pl.when(pl.program_id(1) == 0)
    def _():
        keep = ((pl.program_id(0) * tm) % seq != 0).astype(F32)
        h_ref[0:HALO, :] = (_rms(xh_ref[...], g_ref[...]) * keep).astype(BF16)
        _norm_rows(x_ref, g_ref, h_ref, tm, h_off=HALO)

    u = jnp.dot(h_ref[...], wu_ref[...], preferred_element_type=F32)
    gate = jnp.dot(h_ref[HALO:, :], wg_ref[...], preferred_element_type=F32)
    conv = (cb_ref[...] + cw_ref[2:3, :] * u
            + cw_ref[1:2, :] * pltpu.roll(u, 1, axis=0)
            + cw_ref[0:1, :] * pltpu.roll(u, 2, axis=0))
    o_ref[...] = (jax.nn.gelu(conv[HALO:, :]) * gate).astype(o_ref.dtype)


def _ffn_up(x, g, wu, wg, cw, cb, layer, seq):
    t, d = x.shape
    f = wu.shape[2]
    tm = _tile(seq, 1024, HALO)
    tf = _tile(f, 512, LANES)
    hb = tm // HALO
    col = lambda i, j: (0, j)
    return pl.pallas_call(
        functools.partial(_ffn_up_kernel, seq=seq),
        grid=(t // tm, f // tf),
        in_specs=[pl.BlockSpec((tm, d), lambda i, j: (i, 0)),
                  pl.BlockSpec((HALO, d), lambda i, j: (jnp.maximum(i * hb - 1, 0), 0)),
                  _layer_full(g, layer),
                  _layer_spec(wu, layer, (d, tf), col),
                  _layer_spec(wg, layer, (d, tf), col),
                  _layer_spec(cw, layer, (8, tf), col),
                  _layer_spec(cb, layer, (1, tf), col)],
        out_specs=pl.BlockSpec((tm, tf), lambda i, j: (i, j)),
        out_shape=jax.ShapeDtypeStruct((t, f), BF16),
        scratch_shapes=[pltpu.VMEM((HALO + tm, d), BF16)],
        compiler_params=_params("parallel", "arbitrary"),
        name="ffn_up_conv_gate",
    )(x, x, g, wu, wg, cw, cb)


def _final_norm_kernel(x_ref, g_ref, o_ref):
    o_ref[...] = _rms(x_ref[...], g_ref[...])


def _final_norm(x, g):
    t, d = x.shape
    tm = _tile(t, 512, 8)
    return pl.pallas_call(
        _final_norm_kernel,
        grid=(t // tm,),
        in_specs=[pl.BlockSpec((tm, d), lambda i: (i, 0)), pl.BlockSpec((1, d), lambda i: (0, 0))],
        out_specs=pl.BlockSpec((tm, d), lambda i: (i, 0)),
        out_shape=jax.ShapeDtypeStruct((t, d), F32),
        compiler_params=_params("parallel"),
        name="final_norm",
    )(x, g)


def _rope_tables(seq, d):
    pos = jnp.arange(seq, dtype=F32)
    inv_freq = ROPE_THETA ** (-jnp.arange(0, d, 2, dtype=F32) / d)
    ang = pos[:, None] * inv_freq[None, :]
    return jnp.cos(ang), jnp.sin(ang)


def _retention_tables(blk):
    log_gamma = jnp.log(1.0 - 2.0 ** (-5.0 - jnp.arange(RET_HEADS, dtype=F32)))
    idx = jnp.arange(blk, dtype=F32)
    chunk = jnp.arange(blk) // CHUNK
    dist = jnp.abs(idx[:, None] - idx[None, :])
    dec = jnp.where(chunk[None, :] <= chunk[:, None],
                    jnp.exp(log_gamma[:, None, None] * dist[None]), 0.0)
    sin_state = jnp.exp(log_gamma[:, None] * (blk - 1 - idx)[None, :])
    cross = jnp.exp(log_gamma[:, None] * (idx + 1.0)[None, :])
    bcast = lambda a: jnp.broadcast_to(a[:, :, None], (RET_HEADS, blk, RET_DK))
    block_decay = tuple(math.exp(math.log(1.0 - 2.0 ** (-5.0 - h)) * blk) for h in range(RET_HEADS))
    return dec, bcast(sin_state), bcast(cross), block_decay


def _rotate_half_cols(w):
    half = w.shape[-1] // 2
    return jnp.concatenate([-w[..., half:], w[..., :half]], axis=-1)


def kernel(x, norm1_g, w_in, mla_q_norm_g, mla_kv_norm_g, mla_w_uq, mla_w_ukv, fox_b_f,
           w_br_fox, w_br_mla, w_br_ret, w_out, norm2_g, ffn_w_up, ffn_w_gate,
           ffn_conv_w, ffn_conv_b, ffn_w_down, final_norm_g):
    batch, seq, d = x.shape
    depth = w_in.shape[0]
    t = batch * seq
    q_lora = mla_w_uq.shape[1]
    kv_lora = mla_w_ukv.shape[1]

    widths = (FOX_W, FOX_W, FOX_W, FOX_HEADS, q_lora, kv_lora, MLA_ROPE,
              RET_QK_W, RET_QK_W, RET_V_W, RET_V_W, N_BRANCH * d)
    offs = [0]
    for wd in widths:
        offs.append(offs[-1] + wd)
    w_in_b = w_in.astype(BF16)
    seg = lambda k: w_in_b[:, :, offs[k]:offs[k + 1]]
    zeros = lambda n: jnp.zeros((depth, d, n), BF16)
    w_p = jnp.concatenate(
        [seg(0), seg(1), seg(2), seg(5), seg(4), seg(9), seg(10), seg(7), seg(8),
         seg(6), seg(3), zeros(LANES - MLA_ROPE - FOX_HEADS),
         _rotate_half_cols(seg(6)), zeros(LANES - MLA_ROPE)], axis=-1)
    assert w_p.shape[-1] == P_WIDTH and q_lora == 512 and kv_lora == 256
    w_gates = seg(11)
    b_ff = jnp.pad(fox_b_f, ((0, 0), (FF_LANE, LANES - FF_LANE - FOX_HEADS)))[:, None, :]

    uq = mla_w_uq.reshape(depth, q_lora, MLA_HEADS, MLA_NOPE + MLA_ROPE)
    uq_rope = uq[..., MLA_NOPE:]
    pad_q = jnp.zeros((depth, q_lora, MLA_HEADS, MLA_QK_PAD - MLA_NOPE - MLA_ROPE), uq.dtype)
    w_q = jnp.concatenate([uq, pad_q], axis=-1).reshape(depth, q_lora, -1).astype(BF16)
    pad_r = jnp.zeros((depth, q_lora, MLA_HEADS, LANES - MLA_ROPE), uq.dtype)
    w_qr = jnp.concatenate([_rotate_half_cols(uq_rope), pad_r], axis=-1).reshape(depth, q_lora, -1).astype(BF16)
    ukv = mla_w_ukv.reshape(depth, kv_lora, MLA_HEADS, MLA_NOPE + MLA_V)
    w_k = ukv[..., :MLA_NOPE].reshape(depth, kv_lora, -1).astype(BF16)
    w_v = ukv[..., MLA_NOPE:].reshape(depth, kv_lora, -1).astype(BF16)

    w_bf, w_bm, w_br = w_br_fox.astype(BF16), w_br_mla.astype(BF16), w_br_ret.astype(BF16)
    w_o = w_out.astype(BF16)
    w_up, w_gate, w_down = ffn_w_up.astype(BF16), ffn_w_gate.astype(BF16), ffn_w_down.astype(BF16)
    conv_w = jnp.pad(ffn_conv_w, ((0, 0), (0, 8 - CONV_W), (0, 0)))
    conv_b = ffn_conv_b[:, None, :]
    g1, g2 = norm1_g[:, None, :], norm2_g[:, None, :]
    gq, gkv = mla_q_norm_g[:, None, :], mla_kv_norm_g[:, None, :]

    cos64, sin64 = _rope_tables(seq, MLA_ROPE)
    zpad = jnp.zeros((seq, LANES - MLA_ROPE), F32)
    mla_cos = jnp.concatenate([cos64, cos64, zpad], axis=1)
    mla_sin = jnp.concatenate([sin64, sin64, zpad], axis=1)
    cos128, sin128 = _rope_tables(seq, RET_DK)
    ret_cos = jnp.concatenate([cos128, cos128], axis=1)
    ret_sin = jnp.concatenate([-sin128, sin128], axis=1)
    ret_blk = _tile(seq, 512, CHUNK)
    dec, sin_state, cross, block_decay = _retention_tables(ret_blk)

    att_blk = _tile(seq, 512, CHUNK)
    row = jnp.arange(att_blk, dtype=jnp.int32)[:, None]
    col = jnp.arange(att_blk, dtype=jnp.int32)[None, :]
    rel_frame = row - col
    rel_chunk = (row // CHUNK) * CHUNK + (CHUNK - 1) - col

    xf = x.reshape(t, d)
    for i in range(depth):
        p, f_logit = _in_proj(xf, g1, w_p, i)
        kbias = _fox_decay(f_logit, b_ff, i, batch, seq)
        a = _attention(p, OFF_FQ // FOX_W, p, OFF_FK // FOX_W, p, OFF_FV // FOX_W, rel_frame, kbias,
                       batch=batch, seq=seq, heads=FOX_HEADS, dq=FOX_DH, dv=FOX_DH,
                       scale=FOX_DH ** -0.5, name="fox_attention")
        qm, km, vm = _mla_prep(p, gq, gkv, w_q, w_qr, w_k, w_v, mla_cos, mla_sin, i, seq)
        bm = _attention(qm, 0, km, 0, vm, 0, rel_chunk, None,
                        batch=batch, seq=seq, heads=MLA_HEADS, dq=MLA_QK_PAD, dv=MLA_V,
                        scale=(MLA_NOPE + MLA_ROPE) ** -0.5, name="mla_attention")
        cr = _retention(p, ret_cos, ret_sin, dec, sin_state, cross, block_decay, batch, seq)
        merged = _gated_merge(xf, g1, a, bm, cr, w_gates, w_bf, w_bm, w_br, i)
        xf = _matmul_residual(merged, w_o, xf, i, 1024, 1024, "out_proj")
        act = _ffn_up(xf, g2, w_up, w_gate, conv_w, conv_b, i, seq)
        xf = _matmul_residual(act, w_down, xf, i, 512, 1024, "ffn_down")
    return _final_norm(xf, final_norm_g[None, :]).reshape(batch, seq, d)
```

```python
import functools
import math

import jax
import jax.numpy as jnp
from jax import lax
from jax.experimental import pallas as pl
from jax.experimental.pallas import tpu as pltpu

F32 = jnp.float32
BF16 = jnp.bfloat16

NORM_EPS = 1e-6
ROPE_THETA = 10000.0
CHUNK = 64

FOX_HEADS, FOX_DH = 6, 128
MLA_HEADS, MLA_NOPE, MLA_ROPE, MLA_V = 6, 128, 64, 128
MLA_QK_PAD = 256
RET_HEADS, RET_DK, RET_DV = 4, 128, 256
N_BRANCH = 3
CONV_W = 3
LANES = 128
HALO = 16
BIAS_TERMS = 3
ATTN_HEAD_GROUPS = 1

FOX_W = FOX_HEADS * FOX_DH
MLA_W = MLA_HEADS * MLA_V
RET_QK_W = RET_HEADS * RET_DK
RET_V_W = RET_HEADS * RET_DV

OFF_FQ, OFF_FK, OFF_FV = 0, FOX_W, 2 * FOX_W
OFF_MKV = 3 * FOX_W
OFF_MQ = OFF_MKV + 256
OFF_RV = OFF_MQ + 512
OFF_RG = OFF_RV + RET_V_W
OFF_RQ = OFF_RG + RET_V_W
OFF_RK = OFF_RQ + RET_QK_W
OFF_MKR = OFF_RK + RET_QK_W
FF_LANE = MLA_ROPE
OFF_MKRR = OFF_MKR + LANES
P_WIDTH = OFF_MKRR + LANES

VMEM_LIMIT_BYTES = 56 * 1024 * 1024


def _tile(n, pref, align):
    t = (min(pref, n) // align) * align
    while t > align and n % t:
        t -= align
    assert t >= align and n % t == 0, (n, pref, align)
    return t


def _params(*sem):
    return pltpu.CompilerParams(dimension_semantics=sem, vmem_limit_bytes=VMEM_LIMIT_BYTES)


def _layer_spec(arr, layer, block, index):
    return pl.BlockSpec((None,) + tuple(block), lambda *ids: (layer,) + tuple(index(*ids)))


def _layer_full(arr, layer):
    zeros = (0,) * (arr.ndim - 1)
    return pl.BlockSpec((None,) + arr.shape[1:], lambda *ids: (layer,) + zeros)


def _rms(x, g):
    return x * lax.rsqrt(jnp.mean(x * x, axis=-1, keepdims=True) + NORM_EPS) * g


def _norm_rows(x_ref, g_ref, h_ref, rows, h_off=0):
    rc = _tile(rows, 256, 8)

    def body(c, carry):
        r0 = pl.multiple_of(c * rc, rc)
        h_ref[pl.ds(h_off + r0, rc), :] = _rms(x_ref[pl.ds(r0, rc), :], g_ref[...]).astype(BF16)
        return carry

    lax.fori_loop(0, rows // rc, body, 0)


def _in_proj_kernel(x_ref, g_ref, w_ref, o_ref, f_ref, h_ref, *, f_tile, f_col):
    @pl.when(pl.program_id(1) == 0)
    def _():
        _norm_rows(x_ref, g_ref, h_ref, x_ref.shape[0])

    y = jnp.dot(h_ref[...], w_ref[...], preferred_element_type=F32)
    o_ref[...] = y.astype(o_ref.dtype)

    @pl.when(pl.program_id(1) == f_tile)
    def _():
        f_ref[...] = y[:, f_col:f_col + LANES]


def _in_proj(x, g, w, layer):
    t, d = x.shape
    n = w.shape[2]
    tm = _tile(t, 1024, 8)
    tn = _tile(n, 1280, LANES)
    return pl.pallas_call(
        functools.partial(_in_proj_kernel, f_tile=OFF_MKR // tn, f_col=OFF_MKR % tn),
        grid=(t // tm, n // tn),
        in_specs=[pl.BlockSpec((tm, d), lambda i, j: (i, 0)),
                  _layer_full(g, layer),
                  _layer_spec(w, layer, (d, tn), lambda i, j: (0, j))],
        out_specs=[pl.BlockSpec((tm, tn), lambda i, j: (i, j)),
                   pl.BlockSpec((tm, LANES), lambda i, j: (i, 0)),
                   pl.BlockSpec((tm, d), lambda i, j: (i, 0))],
        out_shape=[jax.ShapeDtypeStruct((t, n), BF16), jax.ShapeDtypeStruct((t, LANES), F32),
                   jax.ShapeDtypeStruct((t, d), BF16)],
        compiler_params=_params("parallel", "arbitrary"),
        name="in_proj",
    )(x, g, w)


def _fox_decay_kernel(f_ref, b_ref, tri_ref, o_ref, carry_ref):
    @pl.when(pl.program_id(1) == 0)
    def _():
        carry_ref[...] = jnp.zeros_like(carry_ref)

    f = f_ref[...] + b_ref[...]
    lf = jnp.minimum(f, 0.0) - jnp.log1p(jnp.exp(-jnp.abs(f)))
    hi = lf.astype(BF16)
    r1 = lf - hi.astype(F32)
    mid = r1.astype(BF16)
    lo = (r1 - mid.astype(F32)).astype(BF16)
    tri = tri_ref[...]
    c = (jnp.dot(tri, hi, preferred_element_type=F32)
         + jnp.dot(tri, mid, preferred_element_type=F32)
         + jnp.dot(tri, lo, preferred_element_type=F32)) + carry_ref[0:1, :]
    carry_ref[...] = jnp.broadcast_to(c[-1:, :], carry_ref.shape)
    lane = lax.broadcasted_iota(jnp.int32, (c.shape[0], LANES), 1)
    for h in range(FOX_HEADS):
        src = FF_LANE + h
        rest = jnp.broadcast_to(c[:, src:src + 1], lane.shape) * (-(FOX_DH ** 0.5))
        slab = jnp.zeros(lane.shape, F32)
        for term in range(BIAS_TERMS):
            part = rest.astype(BF16).astype(F32)
            slab = jnp.where(lane == term, part, slab)
            rest = rest - part
        o_ref[:, h * LANES:(h + 1) * LANES] = slab.astype(BF16)


def _fox_decay(f, b_ff, layer, batch, seq):
    t = f.shape[0]
    ts = _tile(seq, 1024, 8)
    nb = seq // ts
    tri = (jnp.arange(ts)[:, None] >= jnp.arange(ts)[None, :]).astype(BF16)
    return pl.pallas_call(
        _fox_decay_kernel,
        grid=(batch, nb),
        in_specs=[pl.BlockSpec((ts, LANES), lambda b, n: (b * nb + n, 0)),
                  _layer_full(b_ff, layer),
                  pl.BlockSpec((ts, ts), lambda b, n: (0, 0))],
        out_specs=pl.BlockSpec((ts, FOX_HEADS * LANES), lambda b, n: (b * nb + n, 0)),
        out_shape=jax.ShapeDtypeStruct((t, FOX_HEADS * LANES), BF16),
        scratch_shapes=[pltpu.VMEM((8, LANES), F32)],
        compiler_params=_params("parallel", "arbitrary"),
        name="fox_decay_cumsum",
    )(f, b_ff, tri)


def _mla_prep_kernel(mq_ref, mkv_ref, kr_ref, krr_ref, gq_ref, gkv_ref, wq_ref, wqr_ref,
                     wk_ref, wv_ref, cos_ref, sin_ref, q_ref, k_ref, v_ref):
    cos = cos_ref[...]
    sin = sin_ref[...]
    cqn = _rms(mq_ref[...].astype(F32), gq_ref[...]).astype(BF16)
    q = jnp.dot(cqn, wq_ref[...], preferred_element_type=F32)
    qrot = jnp.dot(cqn, wqr_ref[...], preferred_element_type=F32)
    ckvn = _rms(mkv_ref[...].astype(F32), gkv_ref[...]).astype(BF16)
    kn = jnp.dot(ckvn, wk_ref[...], preferred_element_type=F32)
    v_ref[...] = jnp.dot(ckvn, wv_ref[...], preferred_element_type=F32).astype(BF16)
    kr = (kr_ref[...].astype(F32) * cos + krr_ref[...].astype(F32) * sin).astype(BF16)
    for h in range(MLA_HEADS):
        lo = h * MLA_QK_PAD
        q_ref[:, lo:lo + LANES] = q[:, lo:lo + LANES].astype(BF16)
        q_ref[:, lo + LANES:lo + 2 * LANES] = (
            q[:, lo + LANES:lo + 2 * LANES] * cos + qrot[:, h * LANES:(h + 1) * LANES] * sin).astype(BF16)
        k_ref[:, lo:lo + LANES] = kn[:, h * LANES:(h + 1) * LANES].astype(BF16)
        k_ref[:, lo + LANES:lo + 2 * LANES] = kr


def _mla_prep(p, gq, gkv, wq, wqr, wk, wv, cos, sin, layer, seq):
    t = p.shape[0]
    tm = _tile(seq, 512, 8)
    ns = seq // tm
    ql, kvl = wq.shape[1], wk.shape[1]
    row = lambda i: (i, 0)
    return pl.pallas_call(
        _mla_prep_kernel,
        grid=(t // tm,),
        in_specs=[pl.BlockSpec((tm, ql), lambda i: (i, OFF_MQ // ql)),
                  pl.BlockSpec((tm, kvl), lambda i: (i, OFF_MKV // kvl)),
                  pl.BlockSpec((tm, LANES), lambda i: (i, OFF_MKR // LANES)),
                  pl.BlockSpec((tm, LANES), lambda i: (i, OFF_MKRR // LANES)),
                  _layer_full(gq, layer), _layer_full(gkv, layer),
                  _layer_full(wq, layer), _layer_full(wqr, layer),
                  _layer_full(wk, layer), _layer_full(wv, layer),
                  pl.BlockSpec((tm, LANES), lambda i: (i % ns, 0)),
                  pl.BlockSpec((tm, LANES), lambda i: (i % ns, 0))],
        out_specs=[pl.BlockSpec((tm, MLA_HEADS * MLA_QK_PAD), row),
                   pl.BlockSpec((tm, MLA_HEADS * MLA_QK_PAD), row),
                   pl.BlockSpec((tm, MLA_W), row)],
        out_shape=[jax.ShapeDtypeStruct((t, MLA_HEADS * MLA_QK_PAD), BF16),
                   jax.ShapeDtypeStruct((t, MLA_HEADS * MLA_QK_PAD), BF16),
                   jax.ShapeDtypeStruct((t, MLA_W), BF16)],
        compiler_params=_params("parallel"),
        name="mla_prep",
    )(p, p, p, p, gq, gkv, wq, wqr, wk, wv, cos, sin)


def _attn_kernel(*refs, heads, dq, dv, coef, has_bias, tq):
    if has_bias:
        q_ref, k_ref, v_ref, rel_ref, kb_ref, o_ref, m_ref, l_ref, acc_ref = refs
    else:
        q_ref, k_ref, v_ref, rel_ref, o_ref, m_ref, l_ref, acc_ref = refs
    qi = pl.program_id(2)

    def accumulate(k0, visible, first):
        rows = pl.ds(k0, tq)
        ones_v = jnp.ones((tq, LANES), BF16)
        if has_bias:
            lane = lax.broadcasted_iota(jnp.int32, (tq, LANES), 1)
            ones_q = jnp.where(lane < BIAS_TERMS, 1.0, 0.0).astype(BF16)
        for h in range(heads):
            q = q_ref[:, h * dq:(h + 1) * dq]
            k = k_ref[rows, h * dq:(h + 1) * dq]
            if has_bias:
                q = jnp.concatenate([q, ones_q], axis=1)
                k = jnp.concatenate([k, kb_ref[rows, h * LANES:(h + 1) * LANES]], axis=1)
            s = lax.dot_general(q, k, (((1,), (1,)), ((), ())), preferred_element_type=F32)
            if visible is not None:
                s = jnp.where(visible, s, -jnp.inf)
            m_new = jnp.broadcast_to(jnp.max(s, axis=1, keepdims=True), (tq, LANES))
            if not first:
                m_prev = m_ref[h]
                m_new = jnp.maximum(m_prev, m_new)
                alpha = jnp.exp2((m_prev - m_new) * coef)
            p = jnp.concatenate(
                [jnp.exp2((s[:, j * LANES:(j + 1) * LANES] - m_new) * coef) for j in range(tq // LANES)],
                axis=1).astype(BF16)
            v1 = jnp.concatenate([v_ref[rows, h * dv:(h + 1) * dv], ones_v], axis=1)
            pv = jnp.dot(p, v1, preferred_element_type=F32)
            if first:
                acc_ref[:, h * dv:(h + 1) * dv] = pv[:, :dv]
                l_ref[h] = pv[:, dv:]
            else:
                acc_ref[:, h * dv:(h + 1) * dv] = alpha * acc_ref[:, h * dv:(h + 1) * dv] + pv[:, :dv]
                l_ref[h] = alpha * l_ref[h] + pv[:, dv:]
            m_ref[h] = m_new

    accumulate(pl.multiple_of(qi * tq, tq), rel_ref[...] >= 0, True)

    def below_diagonal(ki, carry):
        accumulate(pl.multiple_of(ki * tq, tq), None, False)
        return carry

    lax.fori_loop(0, qi, below_diagonal, 0)
    for h in range(heads):
        o_ref[:, h * dv:(h + 1) * dv] = (acc_ref[:, h * dv:(h + 1) * dv] / l_ref[h]).astype(o_ref.dtype)


def _attention(q_arr, q_cb, k_arr, k_cb, v_arr, v_cb, rel, kbias, *, batch, seq, heads, dq, dv, scale, name):
    assert dv == LANES and heads % ATTN_HEAD_GROUPS == 0
    tq = rel.shape[0]
    nq = seq // tq
    ng = ATTN_HEAD_GROUPS
    hg = heads // ng
    has_bias = kbias is not None
    in_specs = [pl.BlockSpec((tq, hg * dq), lambda b, g, qi: (b * nq + qi, q_cb * ng + g)),
                pl.BlockSpec((seq, hg * dq), lambda b, g, qi: (b, k_cb * ng + g)),
                pl.BlockSpec((seq, hg * dv), lambda b, g, qi: (b, v_cb * ng + g)),
                pl.BlockSpec((tq, tq), lambda b, g, qi: (0, 0))]
    args = [q_arr, k_arr, v_arr, rel]
    if has_bias:
        in_specs.append(pl.BlockSpec((seq, hg * LANES), lambda b, g, qi: (b, g)))
        args.append(kbias)
    return pl.pallas_call(
        functools.partial(_attn_kernel, heads=hg, dq=dq, dv=dv, coef=scale * math.log2(math.e),
                          has_bias=has_bias, tq=tq),
        grid=(batch, ng, nq),
        in_specs=in_specs,
        out_specs=pl.BlockSpec((tq, hg * dv), lambda b, g, qi: (b * nq + qi, g)),
        out_shape=jax.ShapeDtypeStruct((batch * seq, heads * dv), BF16),
        scratch_shapes=[pltpu.VMEM((hg, tq, LANES), F32),
                        pltpu.VMEM((hg, tq, LANES), F32),
                        pltpu.VMEM((tq, hg * dv), F32)],
        compiler_params=_params("parallel", "parallel", "arbitrary"),
        name=name,
    )(*args)


def _retention_kernel(q_ref, k_ref, v_ref, g_ref, cos_ref, sin_ref, dec_ref, sin_state_ref,
                      cross_ref, o_ref, state_ref, *, block_decay):
    @pl.when(pl.program_id(1) == 0)
    def _():
        state_ref[...] = jnp.zeros_like(state_ref)

    cos = cos_ref[...]
    sin = sin_ref[...]

    def rope(x):
        return x * cos + pltpu.roll(x, RET_DK // 2, axis=1) * sin

    for h in range(RET_HEADS):
        qs = slice(h * RET_DK, (h + 1) * RET_DK)
        vs = slice(h * RET_DV, (h + 1) * RET_DV)
        q = rope(q_ref[:, qs].astype(F32)).astype(BF16)
        kf = rope(k_ref[:, qs].astype(F32)) * (RET_DK ** -0.5)
        v = v_ref[:, vs]
        scores = lax.dot_general(q, kf.astype(BF16), (((1,), (1,)), ((), ())),
                                 preferred_element_type=F32) * dec_ref[h]
        o = jnp.dot(scores.astype(BF16), v, preferred_element_type=F32)
        state = state_ref[h]
        cross = jnp.dot(q, state.astype(BF16), preferred_element_type=F32)
        cd = cross_ref[h]
        o = o + cross * jnp.concatenate([cd, cd], axis=1)
        o = o * lax.rsqrt(jnp.mean(o * o, axis=-1, keepdims=True) + NORM_EPS)
        g = g_ref[:, vs].astype(F32)
        o_ref[:, vs] = (o * (g * jax.nn.sigmoid(g))).astype(o_ref.dtype)
        ks = (kf * sin_state_ref[h]).astype(BF16)
        kv = lax.dot_general(ks, v, (((0,), (0,)), ((), ())), preferred_element_type=F32)
        state_ref[h] = block_decay[h] * state + kv


def _retention(p, cos, sin, dec, sin_state, cross, block_decay, batch, seq):
    blk = dec.shape[1]
    nb = seq // blk
    row = lambda b, n: (b * nb + n, 0)
    const3 = lambda b, n: (0, 0, 0)
    return pl.pallas_call(
        functools.partial(_retention_kernel, block_decay=block_decay),
        grid=(batch, nb),
        in_specs=[pl.BlockSpec((blk, RET_QK_W), lambda b, n: (b * nb + n, OFF_RQ // RET_QK_W)),
                  pl.BlockSpec((blk, RET_QK_W), lambda b, n: (b * nb + n, OFF_RK // RET_QK_W)),
                  pl.BlockSpec((blk, RET_V_W), lambda b, n: (b * nb + n, OFF_RV // RET_V_W)),
                  pl.BlockSpec((blk, RET_V_W), lambda b, n: (b * nb + n, OFF_RG // RET_V_W)),
                  pl.BlockSpec((blk, RET_DK), lambda b, n: (n, 0)),
                  pl.BlockSpec((blk, RET_DK), lambda b, n: (n, 0)),
                  pl.BlockSpec(dec.shape, const3),
                  pl.BlockSpec(sin_state.shape, const3),
                  pl.BlockSpec(cross.shape, const3)],
        out_specs=pl.BlockSpec((blk, RET_V_W), row),
        out_shape=jax.ShapeDtypeStruct((batch * seq, RET_V_W), BF16),
        scratch_shapes=[pltpu.VMEM((RET_HEADS, RET_DK, RET_DV), F32)],
        compiler_params=_params("parallel", "arbitrary"),
        name="retention",
    )(p, p, p, p, cos, sin, dec, sin_state, cross)


def _gated_merge_kernel(h_ref, a_ref, b_ref, c_ref, wg0_ref, wg1_ref, wg2_ref,
                        wa_ref, wb_ref, wc_ref, o_ref):
    h = h_ref[...]

    def branch(wg_ref, br_ref, w_ref):
        gate = jax.nn.sigmoid(jnp.dot(h, wg_ref[...], preferred_element_type=F32))
        return gate * jnp.dot(br_ref[...], w_ref[...], preferred_element_type=F32)

    m = branch(wg0_ref, a_ref, wa_ref)
    m = m + branch(wg1_ref, b_ref, wb_ref)
    m = m + branch(wg2_ref, c_ref, wc_ref)
    o_ref[...] = m.astype(o_ref.dtype)


def _gated_merge(h, a, bm, c, w_gates, wa, wb, wc, layer):
    t, d = h.shape
    tm = _tile(t, 1024, 8)
    tn = _tile(d, 512, LANES)
    nj = d // tn
    act = lambda arr: pl.BlockSpec((tm, arr.shape[1]), lambda i, j: (i, 0))
    wgt = lambda w: _layer_spec(w, layer, (w.shape[1], tn), lambda i, j: (0, j))
    gate = lambda n: _layer_spec(w_gates, layer, (d, tn), lambda i, j: (0, n * nj + j))
    return pl.pallas_call(
        _gated_merge_kernel,
        grid=(t // tm, nj),
        in_specs=[act(h), act(a), act(bm), act(c), gate(0), gate(1), gate(2), wgt(wa), wgt(wb), wgt(wc)],
        out_specs=pl.BlockSpec((tm, tn), lambda i, j: (i, j)),
        out_shape=jax.ShapeDtypeStruct((t, d), BF16),
        compiler_params=_params("parallel", "arbitrary"),
        name="gated_merge",
    )(h, a, bm, c, w_gates, w_gates, w_gates, wa, wb, wc)


def _mm_res_kernel(a_ref, w_ref, r_ref, o_ref):
    o_ref[...] = r_ref[...] + jnp.dot(a_ref[...], w_ref[...], preferred_element_type=F32)


def _matmul_residual(a, w, r, layer, tm_pref, tn_pref, name):
    t, k = a.shape
    n = w.shape[2]
    tm = _tile(t, tm_pref, 8)
    tn = _tile(n, tn_pref, LANES)
    return pl.pallas_call(
        _mm_res_kernel,
        grid=(n // tn, t // tm),
        in_specs=[pl.BlockSpec((tm, k), lambda j, i: (i, 0)),
                  _layer_spec(w, layer, (k, tn), lambda j, i: (0, j)),
                  pl.BlockSpec((tm, tn), lambda j, i: (i, j))],
        out_specs=pl.BlockSpec((tm, tn), lambda j, i: (i, j)),
        out_shape=jax.ShapeDtypeStruct((t, n), F32),
        compiler_params=_params("parallel", "arbitrary"),
        name=name,
    )(a, w, r)


def _ffn_up_kernel(x_ref, xh_ref, g_ref, wu_ref, wg_ref, cw_ref, cb_ref, o_ref, h_ref, *, seq):
    tm = x_ref.shape[0]

    @pl.when(pl.program_id(1) == 0)
    def _():
        keep = ((pl.program_id(0) * tm) % seq != 0).astype(F32)
        h_ref[0:HALO, :] = (_rms(xh_ref[...], g_ref[...]) * keep).astype(BF16)
        _norm_rows(x_ref, g_ref, h_ref, tm, h_off=HALO)

    u = jnp.dot(h_ref[...], wu_ref[...], preferred_element_type=F32)
    gate = jnp.dot(h_ref[HALO:, :], wg_ref[...], preferred_element_type=F32)
    conv = (cb_ref[...] + cw_ref[2:3, :] * u
            + cw_ref[1:2, :] * pltpu.roll(u, 1, axis=0)
            + cw_ref[0:1, :] * pltpu.roll(u, 2, axis=0))
    o_ref[...] = (jax.nn.gelu(conv[HALO:, :]) * gate).astype(o_ref.dtype)


def _ffn_up(x, g, wu, wg, cw, cb, layer, seq):
    t, d = x.shape
    f = wu.shape[2]
    tm = _tile(seq, 1024, HALO)
    tf = _tile(f, 512, LANES)
    hb = tm // HALO
    col = lambda i, j: (0, j)
    return pl.pallas_call(
        functools.partial(_ffn_up_kernel, seq=seq),
        grid=(t // tm, f // tf),
        in_specs=[pl.BlockSpec((tm, d), lambda i, j: (i, 0)),
                  pl.BlockSpec((HALO, d), lambda i, j: (jnp.maximum(i * hb - 1, 0), 0)),
                  _layer_full(g, layer),
                  _layer_spec(wu, layer, (d, tf), col),
                  _layer_spec(wg, layer, (d, tf), col),
                  _layer_spec(cw, layer, (8, tf), col),
                  _layer_spec(cb, layer, (1, tf), col)],
        out_specs=pl.BlockSpec((tm, tf), lambda i, j: (i, j)),
        out_shape=jax.ShapeDtypeStruct((t, f), BF16),
        scratch_shapes=[pltpu.VMEM((HALO + tm, d), BF16)],
        compiler_params=_params("parallel", "arbitrary"),
        name="ffn_up_conv_gate",
    )(x, x, g, wu, wg, cw, cb)


def _final_norm_kernel(x_ref, g_ref, o_ref):
    o_ref[...] = _rms(x_ref[...], g_ref[...])


def _final_norm(x, g):
    t, d = x.shape
    tm = _tile(t, 512, 8)
    return pl.pallas_call(
        _final_norm_kernel,
        grid=(t // tm,),
        in_specs=[pl.BlockSpec((tm, d), lambda i: (i, 0)), pl.BlockSpec((1, d), lambda i: (0, 0))],
        out_specs=pl.BlockSpec((tm, d), lambda i: (i, 0)),
        out_shape=jax.ShapeDtypeStruct((t, d), F32),
        compiler_params=_params("parallel"),
        name="final_norm",
    )(x, g)


def _rope_tables(seq, d):
    pos = jnp.arange(seq, dtype=F32)
    inv_freq = ROPE_THETA ** (-jnp.arange(0, d, 2, dtype=F32) / d)
    ang = pos[:, None] * inv_freq[None, :]
    return jnp.cos(ang), jnp.sin(ang)


def _retention_tables(blk):
    log_gamma = jnp.log(1.0 - 2.0 ** (-5.0 - jnp.arange(RET_HEADS, dtype=F32)))
    idx = jnp.arange(blk, dtype=F32)
    chunk = jnp.arange(blk) // CHUNK
    dist = jnp.abs(idx[:, None] - idx[None, :])
    dec = jnp.where(chunk[None, :] <= chunk[:, None],
                    jnp.exp(log_gamma[:, None, None] * dist[None]), 0.0)
    sin_state = jnp.exp(log_gamma[:, None] * (blk - 1 - idx)[None, :])
    cross = jnp.exp(log_gamma[:, None] * (idx + 1.0)[None, :])
    bcast = lambda a: jnp.broadcast_to(a[:, :, None], (RET_HEADS, blk, RET_DK))
    block_decay = tuple(math.exp(math.log(1.0 - 2.0 ** (-5.0 - h)) * blk) for h in range(RET_HEADS))
    return dec, bcast(sin_state), bcast(cross), block_decay


def _rotate_half_cols(w):
    half = w.shape[-1] // 2
    return jnp.concatenate([-w[..., half:], w[..., :half]], axis=-1)


def kernel(x, norm1_g, w_in, mla_q_norm_g, mla_kv_norm_g, mla_w_uq, mla_w_ukv, fox_b_f,
           w_br_fox, w_br_mla, w_br_ret, w_out, norm2_g, ffn_w_up, ffn_w_gate,
           ffn_conv_w, ffn_conv_b, ffn_w_down, final_norm_g):
    batch, seq, d = x.shape
    depth = w_in.shape[0]
    t = batch * seq
    q_lora = mla_w_uq.shape[1]
    kv_lora = mla_w_ukv.shape[1]

    widths = (FOX_W, FOX_W, FOX_W, FOX_HEADS, q_lora, kv_lora, MLA_ROPE,
              RET_QK_W, RET_QK_W, RET_V_W, RET_V_W, N_BRANCH * d)
    offs = [0]
    for wd in widths:
        offs.append(offs[-1] + wd)
    w_in_b = w_in.astype(BF16)
    seg = lambda k: w_in_b[:, :, offs[k]:offs[k + 1]]
    zeros = lambda n: jnp.zeros((depth, d, n), BF16)
    w_p = jnp.concatenate(
        [seg(0), seg(1), seg(2), seg(5), seg(4), seg(9), seg(10), seg(7), seg(8),
         seg(6), seg(3), zeros(LANES - MLA_ROPE - FOX_HEADS),
         _rotate_half_cols(seg(6)), zeros(LANES - MLA_ROPE)], axis=-1)
    assert w_p.shape[-1] == P_WIDTH and q_lora == 512 and kv_lora == 256
    w_gates = seg(11)
    b_ff = jnp.pad(fox_b_f, ((0, 0), (FF_LANE, LANES - FF_LANE - FOX_HEADS)))[:, None, :]

    uq = mla_w_uq.reshape(depth, q_lora, MLA_HEADS, MLA_NOPE + MLA_ROPE)
    uq_rope = uq[..., MLA_NOPE:]
    pad_q = jnp.zeros((depth, q_lora, MLA_HEADS, MLA_QK_PAD - MLA_NOPE - MLA_ROPE), uq.dtype)
    w_q = jnp.concatenate([uq, pad_q], axis=-1).reshape(depth, q_lora, -1).astype(BF16)
    pad_r = jnp.zeros((depth, q_lora, MLA_HEADS, LANES - MLA_ROPE), uq.dtype)
    w_qr = jnp.concatenate([_rotate_half_cols(uq_rope), pad_r], axis=-1).reshape(depth, q_lora, -1).astype(BF16)
    ukv = mla_w_ukv.reshape(depth, kv_lora, MLA_HEADS, MLA_NOPE + MLA_V)
    w_k = ukv[..., :MLA_NOPE].reshape(depth, kv_lora, -1).astype(BF16)
    w_v = ukv[..., MLA_NOPE:].reshape(depth, kv_lora, -1).astype(BF16)

    w_bf, w_bm, w_br = w_br_fox.astype(BF16), w_br_mla.astype(BF16), w_br_ret.astype(BF16)
    w_o = w_out.astype(BF16)
    w_up, w_gate, w_down = ffn_w_up.astype(BF16), ffn_w_gate.astype(BF16), ffn_w_down.astype(BF16)
    conv_w = jnp.pad(ffn_conv_w, ((0, 0), (0, 8 - CONV_W), (0, 0)))
    conv_b = ffn_conv_b[:, None, :]
    g1, g2 = norm1_g[:, None, :], norm2_g[:, None, :]
    gq, gkv = mla_q_norm_g[:, None, :], mla_kv_norm_g[:, None, :]

    cos64, sin64 = _rope_tables(seq, MLA_ROPE)
    zpad = jnp.zeros((seq, LANES - MLA_ROPE), F32)
    mla_cos = jnp.concatenate([cos64, cos64, zpad], axis=1)
    mla_sin = jnp.concatenate([sin64, sin64, zpad], axis=1)
    cos128, sin128 = _rope_tables(seq, RET_DK)
    ret_cos = jnp.concatenate([cos128, cos128], axis=1)
    ret_sin = jnp.concatenate([-sin128, sin128], axis=1)
    ret_blk = _tile(seq, 512, CHUNK)
    dec, sin_state, cross, block_decay = _retention_tables(ret_blk)

    att_blk = _tile(seq, 512, CHUNK)
    row = jnp.arange(att_blk, dtype=jnp.int32)[:, None]
    col = jnp.arange(att_blk, dtype=jnp.int32)[None, :]
    rel_frame = row - col
    rel_chunk = (row // CHUNK) * CHUNK + (CHUNK - 1) - col

    xf = x.reshape(t, d)
    for i in range(depth):
        p, f_logit, h1 = _in_proj(xf, g1, w_p, i)
        kbias = _fox_decay(f_logit, b_ff, i, batch, seq)
        a = _attention(p, OFF_FQ // FOX_W, p, OFF_FK // FOX_W, p, OFF_FV // FOX_W, rel_frame, kbias,
                       batch=batch, seq=seq, heads=FOX_HEADS, dq=FOX_DH, dv=FOX_DH,
                       scale=FOX_DH ** -0.5, name="fox_attention")
        qm, km, vm = _mla_prep(p, gq, gkv, w_q, w_qr, w_k, w_v, mla_cos, mla_sin, i, seq)
        bm = _attention(qm, 0, km, 0, vm, 0, rel_chunk, None,
                        batch=batch, seq=seq, heads=MLA_HEADS, dq=MLA_QK_PAD, dv=MLA_V,
                        scale=(MLA_NOPE + MLA_ROPE) ** -0.5, name="mla_attention")
        cr = _retention(p, ret_cos, ret_sin, dec, sin_state, cross, block_decay, batch, seq)
        merged = _gated_merge(h1, a, bm, cr, w_gates, w_bf, w_bm, w_br, i)
        xf = _matmul_residual(merged, w_o, xf, i, 1024, 1024, "out_proj")
        act = _ffn_up(xf, g2, w_up, w_gate, conv_w, conv_b, i, seq)
        xf = _matmul_residual(act, w_down, xf, i, 512, 1024, "ffn_down")
    return _final_norm(xf, final_norm_g[None, :]).reshape(batch, seq, d)
```

```python
import functools
import math

import jax
import jax.numpy as jnp
from jax import lax
from jax.experimental import pallas as pl
from jax.experimental.pallas import tpu as pltpu

F32 = jnp.float32
BF16 = jnp.bfloat16

NORM_EPS = 1e-6
ROPE_THETA = 10000.0
CHUNK = 64

FOX_HEADS, FOX_DH = 6, 128
MLA_HEADS, MLA_NOPE, MLA_ROPE, MLA_V = 6, 128, 64, 128
MLA_QK_PAD = 256
RET_HEADS, RET_DK, RET_DV = 4, 128, 256
N_BRANCH = 3
CONV_W = 3
LANES = 128
HALO = 16
BIAS_TERMS = 3
ATTN_HEAD_GROUPS = 1

FOX_W = FOX_HEADS * FOX_DH
MLA_W = MLA_HEADS * MLA_V
RET_QK_W = RET_HEADS * RET_DK
RET_V_W = RET_HEADS * RET_DV

OFF_FQ, OFF_FK, OFF_FV = 0, FOX_W, 2 * FOX_W
OFF_MKV = 3 * FOX_W
OFF_MQ = OFF_MKV + 256
OFF_RV = OFF_MQ + 512
OFF_RG = OFF_RV + RET_V_W
OFF_RQ = OFF_RG + RET_V_W
OFF_RK = OFF_RQ + RET_QK_W
OFF_MKR = OFF_RK + RET_QK_W
FF_LANE = MLA_ROPE
OFF_MKRR = OFF_MKR + LANES
P_WIDTH = OFF_MKRR + LANES

VMEM_LIMIT_BYTES = 56 * 1024 * 1024


def _tile(n, pref, align):
    t = (min(pref, n) // align) * align
    while t > align and n % t:
        t -= align
    assert t >= align and n % t == 0, (n, pref, align)
    return t


def _params(*sem):
    return pltpu.CompilerParams(dimension_semantics=sem, vmem_limit_bytes=VMEM_LIMIT_BYTES)


def _layer_spec(arr, layer, block, index):
    return pl.BlockSpec((None,) + tuple(block), lambda *ids: (layer,) + tuple(index(*ids)))


def _layer_full(arr, layer):
    zeros = (0,) * (arr.ndim - 1)
    return pl.BlockSpec((None,) + arr.shape[1:], lambda *ids: (layer,) + zeros)


def _rms(x, g):
    return x * lax.rsqrt(jnp.mean(x * x, axis=-1, keepdims=True) + NORM_EPS) * g


def _norm_rows(x_ref, g_ref, h_ref, rows, h_off=0):
    rc = _tile(rows, 256, 8)

    def body(c, carry):
        r0 = pl.multiple_of(c * rc, rc)
        h_ref[pl.ds(h_off + r0, rc), :] = _rms(x_ref[pl.ds(r0, rc), :], g_ref[...]).astype(BF16)
        return carry

    lax.fori_loop(0, rows // rc, body, 0)


def _cast_kernel(x_ref, o_ref):
    o_ref[...] = x_ref[...].astype(o_ref.dtype)


def _cast_rows(w, dtype):
    layers, rows, cols = w.shape
    tr = _tile(rows, 256, 8)
    spec = pl.BlockSpec((None, tr, cols), lambda a, b: (a, b, 0))
    return pl.pallas_call(
        _cast_kernel,
        grid=(layers, rows // tr),
        in_specs=[spec],
        out_specs=spec,
        out_shape=jax.ShapeDtypeStruct(w.shape, dtype),
        compiler_params=_params("parallel", "parallel"),
        name="cast_w_in",
    )(w)


def _in_proj_kernel(x_ref, g_ref, w_ref, o_ref, f_ref, h_ref, *, f_tile, f_col):
    @pl.when(pl.program_id(1) == 0)
    def _():
        _norm_rows(x_ref, g_ref, h_ref, x_ref.shape[0])

    y = jnp.dot(h_ref[...], w_ref[...], preferred_element_type=F32)
    o_ref[...] = y.astype(o_ref.dtype)

    @pl.when(pl.program_id(1) == f_tile)
    def _():
        f_ref[...] = y[:, f_col:f_col + LANES]


def _in_proj(x, g, w, layer):
    t, d = x.shape
    n = w.shape[2]
    tm = _tile(t, 1024, 8)
    tn = _tile(n, 1280, LANES)
    return pl.pallas_call(
        functools.partial(_in_proj_kernel, f_tile=OFF_MKR // tn, f_col=OFF_MKR % tn),
        grid=(t // tm, n // tn),
        in_specs=[pl.BlockSpec((tm, d), lambda i, j: (i, 0)),
                  _layer_full(g, layer),
                  _layer_spec(w, layer, (d, tn), lambda i, j: (0, j))],
        out_specs=[pl.BlockSpec((tm, tn), lambda i, j: (i, j)),
                   pl.BlockSpec((tm, LANES), lambda i, j: (i, 0)),
                   pl.BlockSpec((tm, d), lambda i, j: (i, 0))],
        out_shape=[jax.ShapeDtypeStruct((t, n), BF16), jax.ShapeDtypeStruct((t, LANES), F32),
                   jax.ShapeDtypeStruct((t, d), BF16)],
        compiler_params=_params("parallel", "arbitrary"),
        name="in_proj",
    )(x, g, w)


def _fox_decay_kernel(f_ref, b_ref, tri_ref, o_ref, carry_ref):
    @pl.when(pl.program_id(1) == 0)
    def _():
        carry_ref[...] = jnp.zeros_like(carry_ref)

    f = f_ref[...] + b_ref[...]
    lf = jnp.minimum(f, 0.0) - jnp.log1p(jnp.exp(-jnp.abs(f)))
    hi = lf.astype(BF16)
    r1 = lf - hi.astype(F32)
    mid = r1.astype(BF16)
    lo = (r1 - mid.astype(F32)).astype(BF16)
    tri = tri_ref[...]
    c = (jnp.dot(tri, hi, preferred_element_type=F32)
         + jnp.dot(tri, mid, preferred_element_type=F32)
         + jnp.dot(tri, lo, preferred_element_type=F32)) + carry_ref[0:1, :]
    carry_ref[...] = jnp.broadcast_to(c[-1:, :], carry_ref.shape)
    lane = lax.broadcasted_iota(jnp.int32, (c.shape[0], LANES), 1)
    for h in range(FOX_HEADS):
        src = FF_LANE + h
        rest = jnp.broadcast_to(c[:, src:src + 1], lane.shape) * (-(FOX_DH ** 0.5))
        slab = jnp.zeros(lane.shape, F32)
        for term in range(BIAS_TERMS):
            part = rest.astype(BF16).astype(F32)
            slab = jnp.where(lane == term, part, slab)
            rest = rest - part
        o_ref[:, h * LANES:(h + 1) * LANES] = slab.astype(BF16)


def _fox_decay(f, b_ff, layer, batch, seq):
    t = f.shape[0]
    ts = _tile(seq, 1024, 8)
    nb = seq // ts
    tri = (jnp.arange(ts)[:, None] >= jnp.arange(ts)[None, :]).astype(BF16)
    return pl.pallas_call(
        _fox_decay_kernel,
        grid=(batch, nb),
        in_specs=[pl.BlockSpec((ts, LANES), lambda b, n: (b * nb + n, 0)),
                  _layer_full(b_ff, layer),
                  pl.BlockSpec((ts, ts), lambda b, n: (0, 0))],
        out_specs=pl.BlockSpec((ts, FOX_HEADS * LANES), lambda b, n: (b * nb + n, 0)),
        out_shape=jax.ShapeDtypeStruct((t, FOX_HEADS * LANES), BF16),
        scratch_shapes=[pltpu.VMEM((8, LANES), F32)],
        compiler_params=_params("parallel", "arbitrary"),
        name="fox_decay_cumsum",
    )(f, b_ff, tri)


def _mla_prep_kernel(mq_ref, mkv_ref, kr_ref, krr_ref, gq_ref, gkv_ref, wq_ref, wqr_ref,
                     wk_ref, wv_ref, cos_ref, sin_ref, q_ref, k_ref, v_ref):
    cos = cos_ref[...]
    sin = sin_ref[...]
    cqn = _rms(mq_ref[...].astype(F32), gq_ref[...]).astype(BF16)
    q = jnp.dot(cqn, wq_ref[...], preferred_element_type=F32)
    qrot = jnp.dot(cqn, wqr_ref[...], preferred_element_type=F32)
    ckvn = _rms(mkv_ref[...].astype(F32), gkv_ref[...]).astype(BF16)
    kn = jnp.dot(ckvn, wk_ref[...], preferred_element_type=F32)
    v_ref[...] = jnp.dot(ckvn, wv_ref[...], preferred_element_type=F32).astype(BF16)
    kr = (kr_ref[...].astype(F32) * cos + krr_ref[...].astype(F32) * sin).astype(BF16)
    for h in range(MLA_HEADS):
        lo = h * MLA_QK_PAD
        q_ref[:, lo:lo + LANES] = q[:, lo:lo + LANES].astype(BF16)
        q_ref[:, lo + LANES:lo + 2 * LANES] = (
            q[:, lo + LANES:lo + 2 * LANES] * cos + qrot[:, h * LANES:(h + 1) * LANES] * sin).astype(BF16)
        k_ref[:, lo:lo + LANES] = kn[:, h * LANES:(h + 1) * LANES].astype(BF16)
        k_ref[:, lo + LANES:lo + 2 * LANES] = kr


def _mla_prep(p, gq, gkv, wq, wqr, wk, wv, cos, sin, layer, seq):
    t = p.shape[0]
    tm = _tile(seq, 512, 8)
    ns = seq // tm
    ql, kvl = wq.shape[1], wk.shape[1]
    row = lambda i: (i, 0)
    return pl.pallas_call(
        _mla_prep_kernel,
        grid=(t // tm,),
        in_specs=[pl.BlockSpec((tm, ql), lambda i: (i, OFF_MQ // ql)),
                  pl.BlockSpec((tm, kvl), lambda i: (i, OFF_MKV // kvl)),
                  pl.BlockSpec((tm, LANES), lambda i: (i, OFF_MKR // LANES)),
                  pl.BlockSpec((tm, LANES), lambda i: (i, OFF_MKRR // LANES)),
                  _layer_full(gq, layer), _layer_full(gkv, layer),
                  _layer_full(wq, layer), _layer_full(wqr, layer),
                  _layer_full(wk, layer), _layer_full(wv, layer),
                  pl.BlockSpec((tm, LANES), lambda i: (i % ns, 0)),
                  pl.BlockSpec((tm, LANES), lambda i: (i % ns, 0))],
        out_specs=[pl.BlockSpec((tm, MLA_HEADS * MLA_QK_PAD), row),
                   pl.BlockSpec((tm, MLA_HEADS * MLA_QK_PAD), row),
                   pl.BlockSpec((tm, MLA_W), row)],
        out_shape=[jax.ShapeDtypeStruct((t, MLA_HEADS * MLA_QK_PAD), BF16),
                   jax.ShapeDtypeStruct((t, MLA_HEADS * MLA_QK_PAD), BF16),
                   jax.ShapeDtypeStruct((t, MLA_W), BF16)],
        compiler_params=_params("parallel"),
        name="mla_prep",
    )(p, p, p, p, gq, gkv, wq, wqr, wk, wv, cos, sin)


def _attn_kernel(*refs, heads, dq, dv, coef, has_bias, tq):
    if has_bias:
        q_ref, k_ref, v_ref, rel_ref, kb_ref, o_ref, m_ref, l_ref, acc_ref = refs
    else:
        q_ref, k_ref, v_ref, rel_ref, o_ref, m_ref, l_ref, acc_ref = refs
    qi = pl.program_id(2)

    def accumulate(k0, visible, first):
        rows = pl.ds(k0, tq)
        ones_v = jnp.ones((tq, LANES), BF16)
        if has_bias:
            lane = lax.broadcasted_iota(jnp.int32, (tq, LANES), 1)
            ones_q = jnp.where(lane < BIAS_TERMS, 1.0, 0.0).astype(BF16)
        for h in range(heads):
            q = q_ref[:, h * dq:(h + 1) * dq]
            k = k_ref[rows, h * dq:(h + 1) * dq]
            if has_bias:
                q = jnp.concatenate([q, ones_q], axis=1)
                k = jnp.concatenate([k, kb_ref[rows, h * LANES:(h + 1) * LANES]], axis=1)
            s = lax.dot_general(q, k, (((1,), (1,)), ((), ())), preferred_element_type=F32)
            if visible is not None:
                s = jnp.where(visible, s, -jnp.inf)
            m_new = jnp.broadcast_to(jnp.max(s, axis=1, keepdims=True), (tq, LANES))
            if not first:
                m_prev = m_ref[h]
                m_new = jnp.maximum(m_prev, m_new)
                alpha = jnp.exp2((m_prev - m_new) * coef)
            p = jnp.concatenate(
                [jnp.exp2((s[:, j * LANES:(j + 1) * LANES] - m_new) * coef) for j in range(tq // LANES)],
                axis=1).astype(BF16)
            v1 = jnp.concatenate([v_ref[rows, h * dv:(h + 1) * dv], ones_v], axis=1)
            pv = jnp.dot(p, v1, preferred_element_type=F32)
            if first:
                acc_ref[:, h * dv:(h + 1) * dv] = pv[:, :dv]
                l_ref[h] = pv[:, dv:]
            else:
                acc_ref[:, h * dv:(h + 1) * dv] = alpha * acc_ref[:, h * dv:(h + 1) * dv] + pv[:, :dv]
                l_ref[h] = alpha * l_ref[h] + pv[:, dv:]
            m_ref[h] = m_new

    accumulate(pl.multiple_of(qi * tq, tq), rel_ref[...] >= 0, True)

    def below_diagonal_pair(pi, carry):
        k0 = pl.multiple_of(pi * (2 * tq), 2 * tq)
        accumulate(k0, None, False)
        accumulate(pl.multiple_of(k0 + tq, tq), None, False)
        return carry

    lax.fori_loop(0, qi // 2, below_diagonal_pair, 0)

    @pl.when(qi % 2 == 1)
    def _():
        accumulate(pl.multiple_of((qi - 1) * tq, tq), None, False)

    for h in range(heads):
        o_ref[:, h * dv:(h + 1) * dv] = (acc_ref[:, h * dv:(h + 1) * dv] / l_ref[h]).astype(o_ref.dtype)


def _attention(q_arr, q_cb, k_arr, k_cb, v_arr, v_cb, rel, kbias, *, batch, seq, heads, dq, dv, scale, name):
    assert dv == LANES and heads % ATTN_HEAD_GROUPS == 0
    tq = rel.shape[0]
    nq = seq // tq
    ng = ATTN_HEAD_GROUPS
    hg = heads // ng
    has_bias = kbias is not None
    in_specs = [pl.BlockSpec((tq, hg * dq), lambda b, g, qi: (b * nq + qi, q_cb * ng + g)),
                pl.BlockSpec((seq, hg * dq), lambda b, g, qi: (b, k_cb * ng + g)),
                pl.BlockSpec((seq, hg * dv), lambda b, g, qi: (b, v_cb * ng + g)),
                pl.BlockSpec((tq, tq), lambda b, g, qi: (0, 0))]
    args = [q_arr, k_arr, v_arr, rel]
    if has_bias:
        in_specs.append(pl.BlockSpec((seq, hg * LANES), lambda b, g, qi: (b, g)))
        args.append(kbias)
    return pl.pallas_call(
        functools.partial(_attn_kernel, heads=hg, dq=dq, dv=dv, coef=scale * math.log2(math.e),
                          has_bias=has_bias, tq=tq),
        grid=(batch, ng, nq),
        in_specs=in_specs,
        out_specs=pl.BlockSpec((tq, hg * dv), lambda b, g, qi: (b * nq + qi, g)),
        out_shape=jax.ShapeDtypeStruct((batch * seq, heads * dv), BF16),
        scratch_shapes=[pltpu.VMEM((hg, tq, LANES), F32),
                        pltpu.VMEM((hg, tq, LANES), F32),
                        pltpu.VMEM((tq, hg * dv), F32)],
        compiler_params=_params("parallel", "parallel", "arbitrary"),
        name=name,
    )(*args)


def _retention_kernel(q_ref, k_ref, v_ref, g_ref, cos_ref, sin_ref, dec_ref, sin_state_ref,
                      cross_ref, o_ref, state_ref, *, block_decay):
    @pl.when(pl.program_id(1) == 0)
    def _():
        state_ref[...] = jnp.zeros_like(state_ref)

    cos = cos_ref[...]
    sin = sin_ref[...]

    def rope(x):
        return x * cos + pltpu.roll(x, RET_DK // 2, axis=1) * sin

    for h in range(RET_HEADS):
        qs = slice(h * RET_DK, (h + 1) * RET_DK)
        vs = slice(h * RET_DV, (h + 1) * RET_DV)
        q = rope(q_ref[:, qs].astype(F32)).astype(BF16)
        kf = rope(k_ref[:, qs].astype(F32)) * (RET_DK ** -0.5)
        v = v_ref[:, vs]
        scores = lax.dot_general(q, kf.astype(BF16), (((1,), (1,)), ((), ())),
                                 preferred_element_type=F32) * dec_ref[h]
        o = jnp.dot(scores.astype(BF16), v, preferred_element_type=F32)
        state = state_ref[h]
        cross = jnp.dot(q, state.astype(BF16), preferred_element_type=F32)
        cd = cross_ref[h]
        o = o + cross * jnp.concatenate([cd, cd], axis=1)
        o = o * lax.rsqrt(jnp.mean(o * o, axis=-1, keepdims=True) + NORM_EPS)
        g = g_ref[:, vs].astype(F32)
        o_ref[:, vs] = (o * (g * jax.nn.sigmoid(g))).astype(o_ref.dtype)
        ks = (kf * sin_state_ref[h]).astype(BF16)
        kv = lax.dot_general(ks, v, (((0,), (0,)), ((), ())), preferred_element_type=F32)
        state_ref[h] = block_decay[h] * state + kv


def _retention(p, cos, sin, dec, sin_state, cross, block_decay, batch, seq):
    blk = dec.shape[1]
    nb = seq // blk
    row = lambda b, n: (b * nb + n, 0)
    const3 = lambda b, n: (0, 0, 0)
    return pl.pallas_call(
        functools.partial(_retention_kernel, block_decay=block_decay),
        grid=(batch, nb),
        in_specs=[pl.BlockSpec((blk, RET_QK_W), lambda b, n: (b * nb + n, OFF_RQ // RET_QK_W)),
                  pl.BlockSpec((blk, RET_QK_W), lambda b, n: (b * nb + n, OFF_RK // RET_QK_W)),
                  pl.BlockSpec((blk, RET_V_W), lambda b, n: (b * nb + n, OFF_RV // RET_V_W)),
                  pl.BlockSpec((blk, RET_V_W), lambda b, n: (b * nb + n, OFF_RG // RET_V_W)),
                  pl.BlockSpec((blk, RET_DK), lambda b, n: (n, 0)),
                  pl.BlockSpec((blk, RET_DK), lambda b, n: (n, 0)),
                  pl.BlockSpec(dec.shape, const3),
                  pl.BlockSpec(sin_state.shape, const3),
                  pl.BlockSpec(cross.shape, const3)],
        out_specs=pl.BlockSpec((blk, RET_V_W), row),
        out_shape=jax.ShapeDtypeStruct((batch * seq, RET_V_W), BF16),
        scratch_shapes=[pltpu.VMEM((RET_HEADS, RET_DK, RET_DV), F32)],
        compiler_params=_params("parallel", "arbitrary"),
        name="retention",
    )(p, p, p, p, cos, sin, dec, sin_state, cross)


def _gated_merge_kernel(h_ref, a_ref, b_ref, c_ref, wg0_ref, wg1_ref, wg2_ref,
                        wa_ref, wb_ref, wc_ref, o_ref):
    h = h_ref[...]

    def branch(wg_ref, br_ref, w_ref):
        gate = jax.nn.sigmoid(jnp.dot(h, wg_ref[...], preferred_element_type=F32))
        return gate * jnp.dot(br_ref[...], w_ref[...], preferred_element_type=F32)

    m = branch(wg0_ref, a_ref, wa_ref)
    m = m + branch(wg1_ref, b_ref, wb_ref)
    m = m + branch(wg2_ref, c_ref, wc_ref)
    o_ref[...] = m.astype(o_ref.dtype)


def _gated_merge(h, a, bm, c, w_gates, wa, wb, wc, layer):
    t, d = h.shape
    tm = _tile(t, 1024, 8)
    tn = _tile(d, 512, LANES)
    nj = d // tn
    act = lambda arr: pl.BlockSpec((tm, arr.shape[1]), lambda i, j: (i, 0))
    wgt = lambda w: _layer_spec(w, layer, (w.shape[1], tn), lambda i, j: (0, j))
    gate = lambda n: _layer_spec(w_gates, layer, (d, tn), lambda i, j: (0, n * nj + j))
    return pl.pallas_call(
        _gated_merge_kernel,
        grid=(t // tm, nj),
        in_specs=[act(h), act(a), act(bm), act(c), gate(0), gate(1), gate(2), wgt(wa), wgt(wb), wgt(wc)],
        out_specs=pl.BlockSpec((tm, tn), lambda i, j: (i, j)),
        out_shape=jax.ShapeDtypeStruct((t, d), BF16),
        compiler_params=_params("parallel", "arbitrary"),
        name="gated_merge",
    )(h, a, bm, c, w_gates, w_gates, w_gates, wa, wb, wc)


def _mm_res_kernel(a_ref, w_ref, r_ref, o_ref):
    o_ref[...] = r_ref[...] + jnp.dot(a_ref[...], w_ref[...], preferred_element_type=F32)


def _matmul_residual(a, w, r, layer, tm_pref, tn_pref, name):
    t, k = a.shape
    n = w.shape[2]
    tm = _tile(t, tm_pref, 8)
    tn = _tile(n, tn_pref, LANES)
    return pl.pallas_call(
        _mm_res_kernel,
        grid=(n // tn, t // tm),
        in_specs=[pl.BlockSpec((tm, k), lambda j, i: (i, 0)),
                  _layer_spec(w, layer, (k, tn), lambda j, i: (0, j)),
                  pl.BlockSpec((tm, tn), lambda j, i: (i, j))],
        out_specs=pl.BlockSpec((tm, tn), lambda j, i: (i, j)),
        out_shape=jax.ShapeDtypeStruct((t, n), F32),
        compiler_params=_params("parallel", "arbitrary"),
        name=name,
    )(a, w, r)


def _ffn_up_kernel(x_ref, xh_ref, g_ref, wu_ref, wg_ref, cw_ref, cb_ref, o_ref, h_ref, *, seq):
    tm = x_ref.shape[0]

    @pl.when(pl.program_id(1) == 0)
    def _():
        keep = ((pl.program_id(0) * tm) % seq != 0).astype(F32)
        h_ref[0:HALO, :] = (_rms(xh_ref[...], g_ref[...]) * keep).astype(BF16)
        _norm_rows(x_ref, g_ref, h_ref, tm, h_off=HALO)

    u = jnp.dot(h_ref[...], wu_ref[...], preferred_element_type=F32)
    gate = jnp.dot(h_ref[HALO:, :], wg_ref[...], preferred_element_type=F32)
    conv = (cb_ref[...] + cw_ref[2:3, :] * u
            + cw_ref[1:2, :] * pltpu.roll(u, 1, axis=0)
            + cw_ref[0:1, :] * pltpu.roll(u, 2, axis=0))
    o_ref[...] = (jax.nn.gelu(conv[HALO:, :]) * gate).astype(o_ref.dtype)


def _ffn_up(x, g, wu, wg, cw, cb, layer, seq):
    t, d = x.shape
    f = wu.shape[2]
    tm = _tile(seq, 1024, HALO)
    tf = _tile(f, 512, LANES)
    hb = tm // HALO
    col = lambda i, j: (0, j)
    return pl.pallas_call(
        functools.partial(_ffn_up_kernel, seq=seq),
        grid=(t // tm, f // tf),
        in_specs=[pl.BlockSpec((tm, d), lambda i, j: (i, 0)),
                  pl.BlockSpec((HALO, d), lambda i, j: (jnp.maximum(i * hb - 1, 0), 0)),
                  _layer_full(g, layer),
                  _layer_spec(wu, layer, (d, tf), col),
                  _layer_spec(wg, layer, (d, tf), col),
                  _layer_spec(cw, layer, (8, tf), col),
                  _layer_spec(cb, layer, (1, tf), col)],
        out_specs=pl.BlockSpec((tm, tf), lambda i, j: (i, j)),
        out_shape=jax.ShapeDtypeStruct((t, f), BF16),
        scratch_shapes=[pltpu.VMEM((HALO + tm, d), BF16)],
        compiler_params=_params("parallel", "arbitrary"),
        name="ffn_up_conv_gate",
    )(x, x, g, wu, wg, cw, cb)


def _final_norm_kernel(x_ref, g_ref, o_ref):
    o_ref[...] = _rms(x_ref[...], g_ref[...])


def _final_norm(x, g):
    t, d = x.shape
    tm = _tile(t, 512, 8)
    return pl.pallas_call(
        _final_norm_kernel,
        grid=(t // tm,),
        in_specs=[pl.BlockSpec((tm, d), lambda i: (i, 0)), pl.BlockSpec((1, d), lambda i: (0, 0))],
        out_specs=pl.BlockSpec((tm, d), lambda i: (i, 0)),
        out_shape=jax.ShapeDtypeStruct((t, d), F32),
        compiler_params=_params("parallel"),
        name="final_norm",
    )(x, g)


def _rope_tables(seq, d):
    pos = jnp.arange(seq, dtype=F32)
    inv_freq = ROPE_THETA ** (-jnp.arange(0, d, 2, dtype=F32) / d)
    ang = pos[:, None] * inv_freq[None, :]
    return jnp.cos(ang), jnp.sin(ang)


def _retention_tables(blk):
    log_gamma = jnp.log(1.0 - 2.0 ** (-5.0 - jnp.arange(RET_HEADS, dtype=F32)))
    idx = jnp.arange(blk, dtype=F32)
    chunk = jnp.arange(blk) // CHUNK
    dist = jnp.abs(idx[:, None] - idx[None, :])
    dec = jnp.where(chunk[None, :] <= chunk[:, None],
                    jnp.exp(log_gamma[:, None, None] * dist[None]), 0.0)
    sin_state = jnp.exp(log_gamma[:, None] * (blk - 1 - idx)[None, :])
    cross = jnp.exp(log_gamma[:, None] * (idx + 1.0)[None, :])
    bcast = lambda a: jnp.broadcast_to(a[:, :, None], (RET_HEADS, blk, RET_DK))
    block_decay = tuple(math.exp(math.log(1.0 - 2.0 ** (-5.0 - h)) * blk) for h in range(RET_HEADS))
    return dec, bcast(sin_state), bcast(cross), block_decay


def _rotate_half_cols(w):
    half = w.shape[-1] // 2
    return jnp.concatenate([-w[..., half:], w[..., :half]], axis=-1)


def kernel(x, norm1_g, w_in, mla_q_norm_g, mla_kv_norm_g, mla_w_uq, mla_w_ukv, fox_b_f,
           w_br_fox, w_br_mla, w_br_ret, w_out, norm2_g, ffn_w_up, ffn_w_gate,
           ffn_conv_w, ffn_conv_b, ffn_w_down, final_norm_g):
    batch, seq, d = x.shape
    depth = w_in.shape[0]
    t = batch * seq
    q_lora = mla_w_uq.shape[1]
    kv_lora = mla_w_ukv.shape[1]

    widths = (FOX_W, FOX_W, FOX_W, FOX_HEADS, q_lora, kv_lora, MLA_ROPE,
              RET_QK_W, RET_QK_W, RET_V_W, RET_V_W, N_BRANCH * d)
    offs = [0]
    for wd in widths:
        offs.append(offs[-1] + wd)
    w_in_b = _cast_rows(w_in, BF16)
    seg = lambda k: w_in_b[:, :, offs[k]:offs[k + 1]]
    zeros = lambda n: jnp.zeros((depth, d, n), BF16)
    w_p = jnp.concatenate(
        [seg(0), seg(1), seg(2), seg(5), seg(4), seg(9), seg(10), seg(7), seg(8),
         seg(6), seg(3), zeros(LANES - MLA_ROPE - FOX_HEADS),
         _rotate_half_cols(seg(6)), zeros(LANES - MLA_ROPE)], axis=-1)
    assert w_p.shape[-1] == P_WIDTH and q_lora == 512 and kv_lora == 256
    w_gates = seg(11)
    b_ff = jnp.pad(fox_b_f, ((0, 0), (FF_LANE, LANES - FF_LANE - FOX_HEADS)))[:, None, :]

    uq = mla_w_uq.reshape(depth, q_lora, MLA_HEADS, MLA_NOPE + MLA_ROPE)
    uq_rope = uq[..., MLA_NOPE:]
    pad_q = jnp.zeros((depth, q_lora, MLA_HEADS, MLA_QK_PAD - MLA_NOPE - MLA_ROPE), uq.dtype)
    w_q = jnp.concatenate([uq, pad_q], axis=-1).reshape(depth, q_lora, -1).astype(BF16)
    pad_r = jnp.zeros((depth, q_lora, MLA_HEADS, LANES - MLA_ROPE), uq.dtype)
    w_qr = jnp.concatenate([_rotate_half_cols(uq_rope), pad_r], axis=-1).reshape(depth, q_lora, -1).astype(BF16)
    ukv = mla_w_ukv.reshape(depth, kv_lora, MLA_HEADS, MLA_NOPE + MLA_V)
    w_k = ukv[..., :MLA_NOPE].reshape(depth, kv_lora, -1).astype(BF16)
    w_v = ukv[..., MLA_NOPE:].reshape(depth, kv_lora, -1).astype(BF16)

    w_bf, w_bm, w_br = w_br_fox.astype(BF16), w_br_mla.astype(BF16), w_br_ret.astype(BF16)
    w_o = w_out.astype(BF16)
    w_up, w_gate, w_down = ffn_w_up.astype(BF16), ffn_w_gate.astype(BF16), ffn_w_down.astype(BF16)
    conv_w = jnp.pad(ffn_conv_w, ((0, 0), (0, 8 - CONV_W), (0, 0)))
    conv_b = ffn_conv_b[:, None, :]
    g1, g2 = norm1_g[:, None, :], norm2_g[:, None, :]
    gq, gkv = mla_q_norm_g[:, None, :], mla_kv_norm_g[:, None, :]

    cos64, sin64 = _rope_tables(seq, MLA_ROPE)
    zpad = jnp.zeros((seq, LANES - MLA_ROPE), F32)
    mla_cos = jnp.concatenate([cos64, cos64, zpad], axis=1)
    mla_sin = jnp.concatenate([sin64, sin64, zpad], axis=1)
    cos128, sin128 = _rope_tables(seq, RET_DK)
    ret_cos = jnp.concatenate([cos128, cos128], axis=1)
    ret_sin = jnp.concatenate([-sin128, sin128], axis=1)
    ret_blk = _tile(seq, 512, CHUNK)
    dec, sin_state, cross, block_decay = _retention_tables(ret_blk)

    att_blk = _tile(seq, 512, CHUNK)
    row = jnp.arange(att_blk, dtype=jnp.int32)[:, None]
    col = jnp.arange(att_blk, dtype=jnp.int32)[None, :]
    rel_frame = row - col
    rel_chunk = (row // CHUNK) * CHUNK + (CHUNK - 1) - col

    xf = x.reshape(t, d)
    for i in range(depth):
        p, f_logit, h1 = _in_proj(xf, g1, w_p, i)
        kbias = _fox_decay(f_logit, b_ff, i, batch, seq)
        a = _attention(p, OFF_FQ // FOX_W, p, OFF_FK // FOX_W, p, OFF_FV // FOX_W, rel_frame, kbias,
                       batch=batch, seq=seq, heads=FOX_HEADS, dq=FOX_DH, dv=FOX_DH,
                       scale=FOX_DH ** -0.5, name="fox_attention")
        qm, km, vm = _mla_prep(p, gq, gkv, w_q, w_qr, w_k, w_v, mla_cos, mla_sin, i, seq)
        bm = _attention(qm, 0, km, 0, vm, 0, rel_chunk, None,
                        batch=batch, seq=seq, heads=MLA_HEADS, dq=MLA_QK_PAD, dv=MLA_V,
                        scale=(MLA_NOPE + MLA_ROPE) ** -0.5, name="mla_attention")
        cr = _retention(p, ret_cos, ret_sin, dec, sin_state, cross, block_decay, batch, seq)
        merged = _gated_merge(h1, a, bm, cr, w_gates, w_bf, w_bm, w_br, i)
        xf = _matmul_residual(merged, w_o, xf, i, 1024, 1024, "out_proj")
        act = _ffn_up(xf, g2, w_up, w_gate, conv_w, conv_b, i, seq)
        xf = _matmul_residual(act, w_down, xf, i, 512, 1024, "ffn_down")
    return _final_norm(xf, final_norm_g[None, :]).reshape(batch, seq, d)
```

```python
import functools
import math

import jax
import jax.numpy as jnp
from jax import lax
from jax.experimental import pallas as pl
from jax.experimental.pallas import tpu as pltpu

F32 = jnp.float32
BF16 = jnp.bfloat16

NORM_EPS = 1e-6
ROPE_THETA = 10000.0
CHUNK = 64

FOX_HEADS, FOX_DH = 6, 128
MLA_HEADS, MLA_NOPE, MLA_ROPE, MLA_V = 6, 128, 64, 128
MLA_QK_PAD = 256
RET_HEADS, RET_DK, RET_DV = 4, 128, 256
N_BRANCH = 3
CONV_W = 3
LANES = 128
HALO = 16
BIAS_TERMS = 3
ATTN_HEAD_GROUPS = 1

FOX_W = FOX_HEADS * FOX_DH
MLA_W = MLA_HEADS * MLA_V
RET_QK_W = RET_HEADS * RET_DK
RET_V_W = RET_HEADS * RET_DV

OFF_FQ, OFF_FK, OFF_FV = 0, FOX_W, 2 * FOX_W
OFF_MKV = 3 * FOX_W
OFF_MQ = OFF_MKV + 256
OFF_RV = OFF_MQ + 512
OFF_RG = OFF_RV + RET_V_W
OFF_RQ = OFF_RG + RET_V_W
OFF_RK = OFF_RQ + RET_QK_W
OFF_MKR = OFF_RK + RET_QK_W
FF_LANE = MLA_ROPE
OFF_MKRR = OFF_MKR + LANES
P_WIDTH = OFF_MKRR + LANES

VMEM_LIMIT_BYTES = 56 * 1024 * 1024


def _tile(n, pref, align):
    t = (min(pref, n) // align) * align
    while t > align and n % t:
        t -= align
    assert t >= align and n % t == 0, (n, pref, align)
    return t


def _params(*sem):
    return pltpu.CompilerParams(dimension_semantics=sem, vmem_limit_bytes=VMEM_LIMIT_BYTES)


def _layer_spec(arr, layer, block, index):
    return pl.BlockSpec((None,) + tuple(block), lambda *ids: (layer,) + tuple(index(*ids)))


def _layer_full(arr, layer):
    zeros = (0,) * (arr.ndim - 1)
    return pl.BlockSpec((None,) + arr.shape[1:], lambda *ids: (layer,) + zeros)


def _rms(x, g):
    return x * lax.rsqrt(jnp.mean(x * x, axis=-1, keepdims=True) + NORM_EPS) * g


def _norm_rows(x_ref, g_ref, h_ref, rows, h_off=0):
    rc = _tile(rows, 256, 8)

    def body(c, carry):
        r0 = pl.multiple_of(c * rc, rc)
        h_ref[pl.ds(h_off + r0, rc), :] = _rms(x_ref[pl.ds(r0, rc), :], g_ref[...]).astype(BF16)
        return carry

    lax.fori_loop(0, rows // rc, body, 0)


def _dot_nt(a, b):
    return lax.dot_general(a, b, (((1,), (1,)), ((), ())), preferred_element_type=F32)


def _in_proj_kernel(x_ref, g_ref, w_ref, o_ref, f_ref, h_ref, *, f_tile, f_col):
    @pl.when(pl.program_id(1) == 0)
    def _():
        _norm_rows(x_ref, g_ref, h_ref, x_ref.shape[0])

    y = _dot_nt(h_ref[...], w_ref[...])
    o_ref[...] = y.astype(o_ref.dtype)

    @pl.when(pl.program_id(1) == f_tile)
    def _():
        f_ref[...] = y[:, f_col:f_col + LANES]


def _in_proj(x, g, w, layer):
    t, d = x.shape
    n = w.shape[1]
    tm = _tile(t, 1024, 8)
    tn = _tile(n, 1280, LANES)
    return pl.pallas_call(
        functools.partial(_in_proj_kernel, f_tile=OFF_MKR // tn, f_col=OFF_MKR % tn),
        grid=(t // tm, n // tn),
        in_specs=[pl.BlockSpec((tm, d), lambda i, j: (i, 0)),
                  _layer_full(g, layer),
                  _layer_spec(w, layer, (tn, d), lambda i, j: (j, 0))],
        out_specs=[pl.BlockSpec((tm, tn), lambda i, j: (i, j)),
                   pl.BlockSpec((tm, LANES), lambda i, j: (i, 0)),
                   pl.BlockSpec((tm, d), lambda i, j: (i, 0))],
        out_shape=[jax.ShapeDtypeStruct((t, n), BF16), jax.ShapeDtypeStruct((t, LANES), F32),
                   jax.ShapeDtypeStruct((t, d), BF16)],
        compiler_params=_params("parallel", "arbitrary"),
        name="in_proj",
    )(x, g, w)


def _fox_decay_kernel(f_ref, b_ref, tri_ref, o_ref, carry_ref):
    @pl.when(pl.program_id(1) == 0)
    def _():
        carry_ref[...] = jnp.zeros_like(carry_ref)

    f = f_ref[...] + b_ref[...]
    lf = jnp.minimum(f, 0.0) - jnp.log1p(jnp.exp(-jnp.abs(f)))
    hi = lf.astype(BF16)
    r1 = lf - hi.astype(F32)
    mid = r1.astype(BF16)
    lo = (r1 - mid.astype(F32)).astype(BF16)
    tri = tri_ref[...]
    c = (jnp.dot(tri, hi, preferred_element_type=F32)
         + jnp.dot(tri, mid, preferred_element_type=F32)
         + jnp.dot(tri, lo, preferred_element_type=F32)) + carry_ref[0:1, :]
    carry_ref[...] = jnp.broadcast_to(c[-1:, :], carry_ref.shape)
    lane = lax.broadcasted_iota(jnp.int32, (c.shape[0], LANES), 1)
    for h in range(FOX_HEADS):
        src = FF_LANE + h
        rest = jnp.broadcast_to(c[:, src:src + 1], lane.shape) * (-(FOX_DH ** 0.5))
        slab = jnp.zeros(lane.shape, F32)
        for term in range(BIAS_TERMS):
            part = rest.astype(BF16).astype(F32)
            slab = jnp.where(lane == term, part, slab)
            rest = rest - part
        o_ref[:, h * LANES:(h + 1) * LANES] = slab.astype(BF16)


def _fox_decay(f, b_ff, layer, batch, seq):
    t = f.shape[0]
    ts = _tile(seq, 1024, 8)
    nb = seq // ts
    tri = (jnp.arange(ts)[:, None] >= jnp.arange(ts)[None, :]).astype(BF16)
    return pl.pallas_call(
        _fox_decay_kernel,
        grid=(batch, nb),
        in_specs=[pl.BlockSpec((ts, LANES), lambda b, n: (b * nb + n, 0)),
                  _layer_full(b_ff, layer),
                  pl.BlockSpec((ts, ts), lambda b, n: (0, 0))],
        out_specs=pl.BlockSpec((ts, FOX_HEADS * LANES), lambda b, n: (b * nb + n, 0)),
        out_shape=jax.ShapeDtypeStruct((t, FOX_HEADS * LANES), BF16),
        scratch_shapes=[pltpu.VMEM((8, LANES), F32)],
        compiler_params=_params("parallel", "arbitrary"),
        name="fox_decay_cumsum",
    )(f, b_ff, tri)


def _mla_prep_kernel(mq_ref, mkv_ref, kr_ref, krr_ref, gq_ref, gkv_ref, wq_ref, wqr_ref,
                     wk_ref, wv_ref, cos_ref, sin_ref, q_ref, k_ref, v_ref):
    cos = cos_ref[...]
    sin = sin_ref[...]
    cqn = _rms(mq_ref[...].astype(F32), gq_ref[...]).astype(BF16)
    q = jnp.dot(cqn, wq_ref[...], preferred_element_type=F32)
    qrot = jnp.dot(cqn, wqr_ref[...], preferred_element_type=F32)
    ckvn = _rms(mkv_ref[...].astype(F32), gkv_ref[...]).astype(BF16)
    kn = jnp.dot(ckvn, wk_ref[...], preferred_element_type=F32)
    v_ref[...] = jnp.dot(ckvn, wv_ref[...], preferred_element_type=F32).astype(BF16)
    kr = (kr_ref[...].astype(F32) * cos + krr_ref[...].astype(F32) * sin).astype(BF16)
    for h in range(MLA_HEADS):
        lo = h * MLA_QK_PAD
        q_ref[:, lo:lo + LANES] = q[:, lo:lo + LANES].astype(BF16)
        q_ref[:, lo + LANES:lo + 2 * LANES] = (
            q[:, lo + LANES:lo + 2 * LANES] * cos + qrot[:, h * LANES:(h + 1) * LANES] * sin).astype(BF16)
        k_ref[:, lo:lo + LANES] = kn[:, h * LANES:(h + 1) * LANES].astype(BF16)
        k_ref[:, lo + LANES:lo + 2 * LANES] = kr


def _mla_prep(p, gq, gkv, wq, wqr, wk, wv, cos, sin, layer, seq):
    t = p.shape[0]
    tm = _tile(seq, 512, 8)
    ns = seq // tm
    ql, kvl = wq.shape[1], wk.shape[1]
    row = lambda i: (i, 0)
    return pl.pallas_call(
        _mla_prep_kernel,
        grid=(t // tm,),
        in_specs=[pl.BlockSpec((tm, ql), lambda i: (i, OFF_MQ // ql)),
                  pl.BlockSpec((tm, kvl), lambda i: (i, OFF_MKV // kvl)),
                  pl.BlockSpec((tm, LANES), lambda i: (i, OFF_MKR // LANES)),
                  pl.BlockSpec((tm, LANES), lambda i: (i, OFF_MKRR // LANES)),
                  _layer_full(gq, layer), _layer_full(gkv, layer),
                  _layer_full(wq, layer), _layer_full(wqr, layer),
                  _layer_full(wk, layer), _layer_full(wv, layer),
                  pl.BlockSpec((tm, LANES), lambda i: (i % ns, 0)),
                  pl.BlockSpec((tm, LANES), lambda i: (i % ns, 0))],
        out_specs=[pl.BlockSpec((tm, MLA_HEADS * MLA_QK_PAD), row),
                   pl.BlockSpec((tm, MLA_HEADS * MLA_QK_PAD), row),
                   pl.BlockSpec((tm, MLA_W), row)],
        out_shape=[jax.ShapeDtypeStruct((t, MLA_HEADS * MLA_QK_PAD), BF16),
                   jax.ShapeDtypeStruct((t, MLA_HEADS * MLA_QK_PAD), BF16),
                   jax.ShapeDtypeStruct((t, MLA_W), BF16)],
        compiler_params=_params("parallel"),
        name="mla_prep",
    )(p, p, p, p, gq, gkv, wq, wqr, wk, wv, cos, sin)


def _attn_kernel(*refs, heads, dq, dv, coef, has_bias, tq):
    if has_bias:
        q_ref, k_ref, v_ref, rel_ref, kb_ref, o_ref, m_ref, l_ref, acc_ref = refs
    else:
        q_ref, k_ref, v_ref, rel_ref, o_ref, m_ref, l_ref, acc_ref = refs
    qi = pl.program_id(2)

    def accumulate(k0, visible, first):
        rows = pl.ds(k0, tq)
        ones_v = jnp.ones((tq, LANES), BF16)
        if has_bias:
            lane = lax.broadcasted_iota(jnp.int32, (tq, LANES), 1)
            ones_q = jnp.where(lane < BIAS_TERMS, 1.0, 0.0).astype(BF16)
        for h in range(heads):
            q = q_ref[:, h * dq:(h + 1) * dq]
            k = k_ref[rows, h * dq:(h + 1) * dq]
            if has_bias:
                q = jnp.concatenate([q, ones_q], axis=1)
                k = jnp.concatenate([k, kb_ref[rows, h * LANES:(h + 1) * LANES]], axis=1)
            s = lax.dot_general(q, k, (((1,), (1,)), ((), ())), preferred_element_type=F32)
            if visible is not None:
                s = jnp.where(visible, s, -jnp.inf)
            m_new = jnp.broadcast_to(jnp.max(s, axis=1, keepdims=True), (tq, LANES))
            if not first:
                m_prev = m_ref[h]
                m_new = jnp.maximum(m_prev, m_new)
                alpha = jnp.exp2((m_prev - m_new) * coef)
            p = jnp.concatenate(
                [jnp.exp2((s[:, j * LANES:(j + 1) * LANES] - m_new) * coef) for j in range(tq // LANES)],
                axis=1).astype(BF16)
            v1 = jnp.concatenate([v_ref[rows, h * dv:(h + 1) * dv], ones_v], axis=1)
            pv = jnp.dot(p, v1, preferred_element_type=F32)
            if first:
                acc_ref[:, h * dv:(h + 1) * dv] = pv[:, :dv]
                l_ref[h] = pv[:, dv:]
            else:
                acc_ref[:, h * dv:(h + 1) * dv] = alpha * acc_ref[:, h * dv:(h + 1) * dv] + pv[:, :dv]
                l_ref[h] = alpha * l_ref[h] + pv[:, dv:]
            m_ref[h] = m_new

    accumulate(pl.multiple_of(qi * tq, tq), rel_ref[...] >= 0, True)

    def below_diagonal_pair(pi, carry):
        k0 = pl.multiple_of(pi * (2 * tq), 2 * tq)
        accumulate(k0, None, False)
        accumulate(pl.multiple_of(k0 + tq, tq), None, False)
        return carry

    lax.fori_loop(0, qi // 2, below_diagonal_pair, 0)

    @pl.when(qi % 2 == 1)
    def _():
        accumulate(pl.multiple_of((qi - 1) * tq, tq), None, False)

    for h in range(heads):
        o_ref[:, h * dv:(h + 1) * dv] = (acc_ref[:, h * dv:(h + 1) * dv] / l_ref[h]).astype(o_ref.dtype)


def _attention(q_arr, q_cb, k_arr, k_cb, v_arr, v_cb, rel, kbias, *, batch, seq, heads, dq, dv, scale, name):
    assert dv == LANES and heads % ATTN_HEAD_GROUPS == 0
    tq = rel.shape[0]
    nq = seq // tq
    ng = ATTN_HEAD_GROUPS
    hg = heads // ng
    has_bias = kbias is not None
    in_specs = [pl.BlockSpec((tq, hg * dq), lambda b, g, qi: (b * nq + qi, q_cb * ng + g)),
                pl.BlockSpec((seq, hg * dq), lambda b, g, qi: (b, k_cb * ng + g)),
                pl.BlockSpec((seq, hg * dv), lambda b, g, qi: (b, v_cb * ng + g)),
                pl.BlockSpec((tq, tq), lambda b, g, qi: (0, 0))]
    args = [q_arr, k_arr, v_arr, rel]
    if has_bias:
        in_specs.append(pl.BlockSpec((seq, hg * LANES), lambda b, g, qi: (b, g)))
        args.append(kbias)
    return pl.pallas_call(
        functools.partial(_attn_kernel, heads=hg, dq=dq, dv=dv, coef=scale * math.log2(math.e),
                          has_bias=has_bias, tq=tq),
        grid=(batch, ng, nq),
        in_specs=in_specs,
        out_specs=pl.BlockSpec((tq, hg * dv), lambda b, g, qi: (b * nq + qi, g)),
        out_shape=jax.ShapeDtypeStruct((batch * seq, heads * dv), BF16),
        scratch_shapes=[pltpu.VMEM((hg, tq, LANES), F32),
                        pltpu.VMEM((hg, tq, LANES), F32),
                        pltpu.VMEM((tq, hg * dv), F32)],
        compiler_params=_params("parallel", "parallel", "arbitrary"),
        name=name,
    )(*args)


def _retention_kernel(q_ref, k_ref, v_ref, g_ref, cos_ref, sin_ref, dec_ref, sin_state_ref,
                      cross_ref, o_ref, state_ref, *, block_decay):
    @pl.when(pl.program_id(1) == 0)
    def _():
        state_ref[...] = jnp.zeros_like(state_ref)

    cos = cos_ref[...]
    sin = sin_ref[...]

    def rope(x):
        return x * cos + pltpu.roll(x, RET_DK // 2, axis=1) * sin

    for h in range(RET_HEADS):
        qs = slice(h * RET_DK, (h + 1) * RET_DK)
        vs = slice(h * RET_DV, (h + 1) * RET_DV)
        q = rope(q_ref[:, qs].astype(F32)).astype(BF16)
        kf = rope(k_ref[:, qs].astype(F32)) * (RET_DK ** -0.5)
        v = v_ref[:, vs]
        scores = lax.dot_general(q, kf.astype(BF16), (((1,), (1,)), ((), ())),
                                 preferred_element_type=F32) * dec_ref[h]
        o = jnp.dot(scores.astype(BF16), v, preferred_element_type=F32)
        state = state_ref[h]
        cross = jnp.dot(q, state.astype(BF16), preferred_element_type=F32)
        cd = cross_ref[h]
        o = o + cross * jnp.concatenate([cd, cd], axis=1)
        o = o * lax.rsqrt(jnp.mean(o * o, axis=-1, keepdims=True) + NORM_EPS)
        g = g_ref[:, vs].astype(F32)
        o_ref[:, vs] = (o * (g * jax.nn.sigmoid(g))).astype(o_ref.dtype)
        ks = (kf * sin_state_ref[h]).astype(BF16)
        kv = lax.dot_general(ks, v, (((0,), (0,)), ((), ())), preferred_element_type=F32)
        state_ref[h] = block_decay[h] * state + kv


def _retention(p, cos, sin, dec, sin_state, cross, block_decay, batch, seq):
    blk = dec.shape[1]
    nb = seq // blk
    row = lambda b, n: (b * nb + n, 0)
    const3 = lambda b, n: (0, 0, 0)
    return pl.pallas_call(
        functools.partial(_retention_kernel, block_decay=block_decay),
        grid=(batch, nb),
        in_specs=[pl.BlockSpec((blk, RET_QK_W), lambda b, n: (b * nb + n, OFF_RQ // RET_QK_W)),
                  pl.BlockSpec((blk, RET_QK_W), lambda b, n: (b * nb + n, OFF_RK // RET_QK_W)),
                  pl.BlockSpec((blk, RET_V_W), lambda b, n: (b * nb + n, OFF_RV // RET_V_W)),
                  pl.BlockSpec((blk, RET_V_W), lambda b, n: (b * nb + n, OFF_RG // RET_V_W)),
                  pl.BlockSpec((blk, RET_DK), lambda b, n: (n, 0)),
                  pl.BlockSpec((blk, RET_DK), lambda b, n: (n, 0)),
                  pl.BlockSpec(dec.shape, const3),
                  pl.BlockSpec(sin_state.shape, const3),
                  pl.BlockSpec(cross.shape, const3)],
        out_specs=pl.BlockSpec((blk, RET_V_W), row),
        out_shape=jax.ShapeDtypeStruct((batch * seq, RET_V_W), BF16),
        scratch_shapes=[pltpu.VMEM((RET_HEADS, RET_DK, RET_DV), F32)],
        compiler_params=_params("parallel", "arbitrary"),
        name="retention",
    )(p, p, p, p, cos, sin, dec, sin_state, cross)


def _gated_merge_kernel(h_ref, a_ref, b_ref, c_ref, wg0_ref, wg1_ref, wg2_ref,
                        wa_ref, wb_ref, wc_ref, o_ref):
    h = h_ref[...]

    def branch(wg_ref, br_ref, w_ref):
        gate = jax.nn.sigmoid(_dot_nt(h, wg_ref[...]))
        return gate * jnp.dot(br_ref[...], w_ref[...], preferred_element_type=F32)

    m = branch(wg0_ref, a_ref, wa_ref)
    m = m + branch(wg1_ref, b_ref, wb_ref)
    m = m + branch(wg2_ref, c_ref, wc_ref)
    o_ref[...] = m.astype(o_ref.dtype)


def _gated_merge(h, a, bm, c, w_gates, wa, wb, wc, layer):
    t, d = h.shape
    tm = _tile(t, 1024, 8)
    tn = _tile(d, 512, LANES)
    nj = d // tn
    act = lambda arr: pl.BlockSpec((tm, arr.shape[1]), lambda i, j: (i, 0))
    wgt = lambda w: _layer_spec(w, layer, (w.shape[1], tn), lambda i, j: (0, j))
    gate = lambda n: _layer_spec(w_gates, layer, (tn, d), lambda i, j: (n * nj + j, 0))
    return pl.pallas_call(
        _gated_merge_kernel,
        grid=(t // tm, nj),
        in_specs=[act(h), act(a), act(bm), act(c), gate(0), gate(1), gate(2), wgt(wa), wgt(wb), wgt(wc)],
        out_specs=pl.BlockSpec((tm, tn), lambda i, j: (i, j)),
        out_shape=jax.ShapeDtypeStruct((t, d), BF16),
        compiler_params=_params("parallel", "arbitrary"),
        name="gated_merge",
    )(h, a, bm, c, w_gates, w_gates, w_gates, wa, wb, wc)


def _mm_res_kernel(a_ref, w_ref, r_ref, o_ref):
    o_ref[...] = r_ref[...] + jnp.dot(a_ref[...], w_ref[...], preferred_element_type=F32)


def _matmul_residual(a, w, r, layer, tm_pref, tn_pref, name):
    t, k = a.shape
    n = w.shape[2]
    tm = _tile(t, tm_pref, 8)
    tn = _tile(n, tn_pref, LANES)
    return pl.pallas_call(
        _mm_res_kernel,
        grid=(n // tn, t // tm),
        in_specs=[pl.BlockSpec((tm, k), lambda j, i: (i, 0)),
                  _layer_spec(w, layer, (k, tn), lambda j, i: (0, j)),
                  pl.BlockSpec((tm, tn), lambda j, i: (i, j))],
        out_specs=pl.BlockSpec((tm, tn), lambda j, i: (i, j)),
        out_shape=jax.ShapeDtypeStruct((t, n), F32),
        compiler_params=_params("parallel", "arbitrary"),
        name=name,
    )(a, w, r)


def _ffn_up_kernel(x_ref, xh_ref, g_ref, wu_ref, wg_ref, cw_ref, cb_ref, o_ref, h_ref, *, seq):
    tm = x_ref.shape[0]

    @pl.when(pl.program_id(1) == 0)
    def _():
        keep = ((pl.program_id(0) * tm) % seq != 0).astype(F32)
        h_ref[0:HALO, :] = (_rms(xh_ref[...], g_ref[...]) * keep).astype(BF16)
        _norm_rows(x_ref, g_ref, h_ref, tm, h_off=HALO)

    u = jnp.dot(h_ref[...], wu_ref[...], preferred_element_type=F32)
    gate = jnp.dot(h_ref[HALO:, :], wg_ref[...], preferred_element_type=F32)
    conv = (cb_ref[...] + cw_ref[2:3, :] * u
            + cw_ref[1:2, :] * pltpu.roll(u, 1, axis=0)
            + cw_ref[0:1, :] * pltpu.roll(u, 2, axis=0))
    o_ref[...] = (jax.nn.gelu(conv[HALO:, :]) * gate).astype(o_ref.dtype)


def _ffn_up(x, g, wu, wg, cw, cb, layer, seq):
    t, d = x.shape
    f = wu.shape[2]
    tm = _tile(seq, 1024, HALO)
    tf = _tile(f, 512, LANES)
    hb = tm // HALO
    col = lambda i, j: (0, j)
    return pl.pallas_call(
        functools.partial(_ffn_up_kernel, seq=seq),
        grid=(t // tm, f // tf),
        in_specs=[pl.BlockSpec((tm, d), lambda i, j: (i, 0)),
                  pl.BlockSpec((HALO, d), lambda i, j: (jnp.maximum(i * hb - 1, 0), 0)),
                  _layer_full(g, layer),
                  _layer_spec(wu, layer, (d, tf), col),
                  _layer_spec(wg, layer, (d, tf), col),
                  _layer_spec(cw, layer, (8, tf), col),
                  _layer_spec(cb, layer, (1, tf), col)],
        out_specs=pl.BlockSpec((tm, tf), lambda i, j: (i, j)),
        out_shape=jax.ShapeDtypeStruct((t, f), BF16),
        scratch_shapes=[pltpu.VMEM((HALO + tm, d), BF16)],
        compiler_params=_params("parallel", "arbitrary"),
        name="ffn_up_conv_gate",
    )(x, x, g, wu, wg, cw, cb)


def _final_norm_kernel(x_ref, g_ref, o_ref):
    o_ref[...] = _rms(x_ref[...], g_ref[...])


def _final_norm(x, g):
    t, d = x.shape
    tm = _tile(t, 512, 8)
    return pl.pallas_call(
        _final_norm_kernel,
        grid=(t // tm,),
        in_specs=[pl.BlockSpec((tm, d), lambda i: (i, 0)), pl.BlockSpec((1, d), lambda i: (0, 0))],
        out_specs=pl.BlockSpec((tm, d), lambda i: (i, 0)),
        out_shape=jax.ShapeDtypeStruct((t, d), F32),
        compiler_params=_params("parallel"),
        name="final_norm",
    )(x, g)


def _rope_tables(seq, d):
    pos = jnp.arange(seq, dtype=F32)
    inv_freq = ROPE_THETA ** (-jnp.arange(0, d, 2, dtype=F32) / d)
    ang = pos[:, None] * inv_freq[None, :]
    return jnp.cos(ang), jnp.sin(ang)


def _retention_tables(blk):
    log_gamma = jnp.log(1.0 - 2.0 ** (-5.0 - jnp.arange(RET_HEADS, dtype=F32)))
    idx = jnp.arange(blk, dtype=F32)
    chunk = jnp.arange(blk) // CHUNK
    dist = jnp.abs(idx[:, None] - idx[None, :])
    dec = jnp.where(chunk[None, :] <= chunk[:, None],
                    jnp.exp(log_gamma[:, None, None] * dist[None]), 0.0)
    sin_state = jnp.exp(log_gamma[:, None] * (blk - 1 - idx)[None, :])
    cross = jnp.exp(log_gamma[:, None] * (idx + 1.0)[None, :])
    bcast = lambda a: jnp.broadcast_to(a[:, :, None], (RET_HEADS, blk, RET_DK))
    block_decay = tuple(math.exp(math.log(1.0 - 2.0 ** (-5.0 - h)) * blk) for h in range(RET_HEADS))
    return dec, bcast(sin_state), bcast(cross), block_decay


def _rotate_half_cols(w):
    half = w.shape[-1] // 2
    return jnp.concatenate([-w[..., half:], w[..., :half]], axis=-1)


def kernel(x, norm1_g, w_in, mla_q_norm_g, mla_kv_norm_g, mla_w_uq, mla_w_ukv, fox_b_f,
           w_br_fox, w_br_mla, w_br_ret, w_out, norm2_g, ffn_w_up, ffn_w_gate,
           ffn_conv_w, ffn_conv_b, ffn_w_down, final_norm_g):
    batch, seq, d = x.shape
    depth = w_in.shape[0]
    t = batch * seq
    q_lora = mla_w_uq.shape[1]
    kv_lora = mla_w_ukv.shape[1]

    widths = (FOX_W, FOX_W, FOX_W, FOX_HEADS, q_lora, kv_lora, MLA_ROPE,
              RET_QK_W, RET_QK_W, RET_V_W, RET_V_W, N_BRANCH * d)
    offs = [0]
    for wd in widths:
        offs.append(offs[-1] + wd)
    w_in_t = jnp.swapaxes(w_in, 1, 2).astype(BF16)
    seg = lambda k: w_in_t[:, offs[k]:offs[k + 1], :]
    zeros = lambda n: jnp.zeros((depth, n, d), BF16)
    rope_t = seg(6)
    w_p = jnp.concatenate(
        [seg(0), seg(1), seg(2), seg(5), seg(4), seg(9), seg(10), seg(7), seg(8),
         rope_t, seg(3), zeros(LANES - MLA_ROPE - FOX_HEADS),
         -rope_t[:, MLA_ROPE // 2:], rope_t[:, :MLA_ROPE // 2], zeros(LANES - MLA_ROPE)], axis=1)
    assert w_p.shape[1] == P_WIDTH and q_lora == 512 and kv_lora == 256
    w_gates = seg(11)
    b_ff = jnp.pad(fox_b_f, ((0, 0), (FF_LANE, LANES - FF_LANE - FOX_HEADS)))[:, None, :]

    uq = mla_w_uq.reshape(depth, q_lora, MLA_HEADS, MLA_NOPE + MLA_ROPE)
    uq_rope = uq[..., MLA_NOPE:]
    pad_q = jnp.zeros((depth, q_lora, MLA_HEADS, MLA_QK_PAD - MLA_NOPE - MLA_ROPE), uq.dtype)
    w_q = jnp.concatenate([uq, pad_q], axis=-1).reshape(depth, q_lora, -1).astype(BF16)
    pad_r = jnp.zeros((depth, q_lora, MLA_HEADS, LANES - MLA_ROPE), uq.dtype)
    w_qr = jnp.concatenate([_rotate_half_cols(uq_rope), pad_r], axis=-1).reshape(depth, q_lora, -1).astype(BF16)
    ukv = mla_w_ukv.reshape(depth, kv_lora, MLA_HEADS, MLA_NOPE + MLA_V)
    w_k = ukv[..., :MLA_NOPE].reshape(depth, kv_lora, -1).astype(BF16)
    w_v = ukv[..., MLA_NOPE:].reshape(depth, kv_lora, -1).astype(BF16)

    w_bf, w_bm, w_br = w_br_fox.astype(BF16), w_br_mla.astype(BF16), w_br_ret.astype(BF16)
    w_o = w_out.astype(BF16)
    w_up, w_gate, w_down = ffn_w_up.astype(BF16), ffn_w_gate.astype(BF16), ffn_w_down.astype(BF16)
    conv_w = jnp.pad(ffn_conv_w, ((0, 0), (0, 8 - CONV_W), (0, 0)))
    conv_b = ffn_conv_b[:, None, :]
    g1, g2 = norm1_g[:, None, :], norm2_g[:, None, :]
    gq, gkv = mla_q_norm_g[:, None, :], mla_kv_norm_g[:, None, :]

    cos64, sin64 = _rope_tables(seq, MLA_ROPE)
    zpad = jnp.zeros((seq, LANES - MLA_ROPE), F32)
    mla_cos = jnp.concatenate([cos64, cos64, zpad], axis=1)
    mla_sin = jnp.concatenate([sin64, sin64, zpad], axis=1)
    cos128, sin128 = _rope_tables(seq, RET_DK)
    ret_cos = jnp.concatenate([cos128, cos128], axis=1)
    ret_sin = jnp.concatenate([-sin128, sin128], axis=1)
    ret_blk = _tile(seq, 512, CHUNK)
    dec, sin_state, cross, block_decay = _retention_tables(ret_blk)

    att_blk = _tile(seq, 512, CHUNK)
    row = jnp.arange(att_blk, dtype=jnp.int32)[:, None]
    col = jnp.arange(att_blk, dtype=jnp.int32)[None, :]
    rel_frame = row - col
    rel_chunk = (row // CHUNK) * CHUNK + (CHUNK - 1) - col

    xf = x.reshape(t, d)
    for i in range(depth):
        p, f_logit, h1 = _in_proj(xf, g1, w_p, i)
        kbias = _fox_decay(f_logit, b_ff, i, batch, seq)
        a = _attention(p, OFF_FQ // FOX_W, p, OFF_FK // FOX_W, p, OFF_FV // FOX_W, rel_frame, kbias,
                       batch=batch, seq=seq, heads=FOX_HEADS, dq=FOX_DH, dv=FOX_DH,
                       scale=FOX_DH ** -0.5, name="fox_attention")
        qm, km, vm = _mla_prep(p, gq, gkv, w_q, w_qr, w_k, w_v, mla_cos, mla_sin, i, seq)
        bm = _attention(qm, 0, km, 0, vm, 0, rel_chunk, None,
                        batch=batch, seq=seq, heads=MLA_HEADS, dq=MLA_QK_PAD, dv=MLA_V,
                        scale=(MLA_NOPE + MLA_ROPE) ** -0.5, name="mla_attention")
        cr = _retention(p, ret_cos, ret_sin, dec, sin_state, cross, block_decay, batch, seq)
        merged = _gated_merge(h1, a, bm, cr, w_gates, w_bf, w_bm, w_br, i)
        xf = _matmul_residual(merged, w_o, xf, i, 1024, 1024, "out_proj")
        act = _ffn_up(xf, g2, w_up, w_gate, conv_w, conv_b, i, seq)
        xf = _matmul_residual(act, w_down, xf, i, 512, 1024, "ffn_down")
    return _final_norm(xf, final_norm_g[None, :]).reshape(batch, seq, d)
```

```python
import functools
import math

import jax
import jax.numpy as jnp
from jax import lax
from jax.experimental import pallas as pl
from jax.experimental.pallas import tpu as pltpu

F32 = jnp.float32
BF16 = jnp.bfloat16

NORM_EPS = 1e-6
ROPE_THETA = 10000.0
CHUNK = 64

FOX_HEADS, FOX_DH = 6, 128
MLA_HEADS, MLA_NOPE, MLA_ROPE, MLA_V = 6, 128, 64, 128
MLA_QK_PAD = 256
RET_HEADS, RET_DK, RET_DV = 4, 128, 256
N_BRANCH = 3
CONV_W = 3
LANES = 128
HALO = 16
BIAS_TERMS = 3
ATTN_HEAD_GROUPS = 1

FOX_W = FOX_HEADS * FOX_DH
MLA_W = MLA_HEADS * MLA_V
RET_QK_W = RET_HEADS * RET_DK
RET_V_W = RET_HEADS * RET_DV

OFF_FQ, OFF_FK, OFF_FV = 0, FOX_W, 2 * FOX_W
OFF_MKV = 3 * FOX_W
OFF_MQ = OFF_MKV + 256
OFF_RV = OFF_MQ + 512
OFF_RG = OFF_RV + RET_V_W
OFF_RQ = OFF_RG + RET_V_W
OFF_RK = OFF_RQ + RET_QK_W
OFF_MKR = OFF_RK + RET_QK_W
FF_LANE = MLA_ROPE
OFF_MKRR = OFF_MKR + LANES
P_WIDTH = OFF_MKRR + LANES

VMEM_LIMIT_BYTES = 56 * 1024 * 1024


def _tile(n, pref, align):
    t = (min(pref, n) // align) * align
    while t > align and n % t:
        t -= align
    assert t >= align and n % t == 0, (n, pref, align)
    return t


def _params(*sem):
    return pltpu.CompilerParams(dimension_semantics=sem, vmem_limit_bytes=VMEM_LIMIT_BYTES)


def _layer_spec(arr, layer, block, index):
    return pl.BlockSpec((None,) + tuple(block), lambda *ids: (layer,) + tuple(index(*ids)))


def _layer_full(arr, layer):
    zeros = (0,) * (arr.ndim - 1)
    return pl.BlockSpec((None,) + arr.shape[1:], lambda *ids: (layer,) + zeros)


def _rms(x, g):
    return x * lax.rsqrt(jnp.mean(x * x, axis=-1, keepdims=True) + NORM_EPS) * g


def _norm_rows(x_ref, g_ref, h_ref, rows, h_off=0):
    rc = _tile(rows, 256, 8)

    def body(c, carry):
        r0 = pl.multiple_of(c * rc, rc)
        h_ref[pl.ds(h_off + r0, rc), :] = _rms(x_ref[pl.ds(r0, rc), :], g_ref[...]).astype(BF16)
        return carry

    lax.fori_loop(0, rows // rc, body, 0)


def _dot_nt(a, b):
    return lax.dot_general(a, b, (((1,), (1,)), ((), ())), preferred_element_type=F32)


def _in_proj_kernel(x_ref, g_ref, w_ref, o_ref, f_ref, h_ref, *, f_tile, f_col):
    @pl.when(pl.program_id(1) == 0)
    def _():
        _norm_rows(x_ref, g_ref, h_ref, x_ref.shape[0])

    y = _dot_nt(h_ref[...], w_ref[...])
    o_ref[...] = y.astype(o_ref.dtype)

    @pl.when(pl.program_id(1) == f_tile)
    def _():
        f_ref[...] = y[:, f_col:f_col + LANES]


def _in_proj(x, g, w, layer):
    t, d = x.shape
    n = w.shape[1]
    tm = _tile(t, 1024, 8)
    tn = _tile(n, 1280, LANES)
    return pl.pallas_call(
        functools.partial(_in_proj_kernel, f_tile=OFF_MKR // tn, f_col=OFF_MKR % tn),
        grid=(t // tm, n // tn),
        in_specs=[pl.BlockSpec((tm, d), lambda i, j: (i, 0)),
                  _layer_full(g, layer),
                  _layer_spec(w, layer, (tn, d), lambda i, j: (j, 0))],
        out_specs=[pl.BlockSpec((tm, tn), lambda i, j: (i, j)),
                   pl.BlockSpec((tm, LANES), lambda i, j: (i, 0)),
                   pl.BlockSpec((tm, d), lambda i, j: (i, 0))],
        out_shape=[jax.ShapeDtypeStruct((t, n), BF16), jax.ShapeDtypeStruct((t, LANES), F32),
                   jax.ShapeDtypeStruct((t, d), BF16)],
        compiler_params=_params("parallel", "arbitrary"),
        name="in_proj",
    )(x, g, w)


def _fox_decay_kernel(f_ref, b_ref, tri_ref, o_ref, carry_ref):
    @pl.when(pl.program_id(1) == 0)
    def _():
        carry_ref[...] = jnp.zeros_like(carry_ref)

    f = f_ref[...] + b_ref[...]
    lf = jnp.minimum(f, 0.0) - jnp.log1p(jnp.exp(-jnp.abs(f)))
    lane = lax.broadcasted_iota(jnp.int32, lf.shape, 1)
    hi = lf.astype(BF16).astype(F32)
    r1 = lf - hi
    mid = r1.astype(BF16).astype(F32)
    lo = r1 - mid
    packed = jnp.where(lane < FF_LANE + FOX_HEADS, hi,
                       jnp.where(lane < FF_LANE + 2 * FOX_HEADS, pltpu.roll(mid, FOX_HEADS, axis=1),
                                 pltpu.roll(lo, 2 * FOX_HEADS, axis=1)))
    cp = jnp.dot(tri_ref[...], packed.astype(BF16), preferred_element_type=F32)
    c = (cp + pltpu.roll(cp, LANES - FOX_HEADS, axis=1)
         + pltpu.roll(cp, LANES - 2 * FOX_HEADS, axis=1)) + carry_ref[0:1, :]
    carry_ref[...] = jnp.broadcast_to(c[-1:, :], carry_ref.shape)
    for h in range(FOX_HEADS):
        src = FF_LANE + h
        rest = jnp.broadcast_to(c[:, src:src + 1], lane.shape) * (-(FOX_DH ** 0.5))
        slab = jnp.zeros(lane.shape, F32)
        for term in range(BIAS_TERMS):
            part = rest.astype(BF16).astype(F32)
            slab = jnp.where(lane == term, part, slab)
            rest = rest - part
        o_ref[:, h * LANES:(h + 1) * LANES] = slab.astype(BF16)


def _fox_decay(f, b_ff, layer, batch, seq):
    t = f.shape[0]
    ts = _tile(seq, 1024, 8)
    nb = seq // ts
    tri = (jnp.arange(ts)[:, None] >= jnp.arange(ts)[None, :]).astype(BF16)
    return pl.pallas_call(
        _fox_decay_kernel,
        grid=(batch, nb),
        in_specs=[pl.BlockSpec((ts, LANES), lambda b, n: (b * nb + n, 0)),
                  _layer_full(b_ff, layer),
                  pl.BlockSpec((ts, ts), lambda b, n: (0, 0))],
        out_specs=pl.BlockSpec((ts, FOX_HEADS * LANES), lambda b, n: (b * nb + n, 0)),
        out_shape=jax.ShapeDtypeStruct((t, FOX_HEADS * LANES), BF16),
        scratch_shapes=[pltpu.VMEM((8, LANES), F32)],
        compiler_params=_params("parallel", "arbitrary"),
        name="fox_decay_cumsum",
    )(f, b_ff, tri)


def _mla_prep_kernel(mq_ref, mkv_ref, kr_ref, krr_ref, gq_ref, gkv_ref, wq_ref, wqr_ref,
                     wk_ref, wv_ref, cos_ref, sin_ref, q_ref, k_ref, v_ref):
    cos = cos_ref[...]
    sin = sin_ref[...]
    cqn = _rms(mq_ref[...].astype(F32), gq_ref[...]).astype(BF16)
    q = jnp.dot(cqn, wq_ref[...], preferred_element_type=F32)
    qrot = jnp.dot(cqn, wqr_ref[...], preferred_element_type=F32)
    ckvn = _rms(mkv_ref[...].astype(F32), gkv_ref[...]).astype(BF16)
    kn = jnp.dot(ckvn, wk_ref[...], preferred_element_type=F32)
    v_ref[...] = jnp.dot(ckvn, wv_ref[...], preferred_element_type=F32).astype(BF16)
    kr = (kr_ref[...].astype(F32) * cos + krr_ref[...].astype(F32) * sin).astype(BF16)
    for h in range(MLA_HEADS):
        lo = h * MLA_QK_PAD
        q_ref[:, lo:lo + LANES] = q[:, lo:lo + LANES].astype(BF16)
        q_ref[:, lo + LANES:lo + 2 * LANES] = (
            q[:, lo + LANES:lo + 2 * LANES] * cos + qrot[:, h * LANES:(h + 1) * LANES] * sin).astype(BF16)
        k_ref[:, lo:lo + LANES] = kn[:, h * LANES:(h + 1) * LANES].astype(BF16)
        k_ref[:, lo + LANES:lo + 2 * LANES] = kr


def _mla_prep(p, gq, gkv, wq, wqr, wk, wv, cos, sin, layer, seq):
    t = p.shape[0]
    tm = _tile(seq, 512, 8)
    ns = seq // tm
    ql, kvl = wq.shape[1], wk.shape[1]
    row = lambda i: (i, 0)
    return pl.pallas_call(
        _mla_prep_kernel,
        grid=(t // tm,),
        in_specs=[pl.BlockSpec((tm, ql), lambda i: (i, OFF_MQ // ql)),
                  pl.BlockSpec((tm, kvl), lambda i: (i, OFF_MKV // kvl)),
                  pl.BlockSpec((tm, LANES), lambda i: (i, OFF_MKR // LANES)),
                  pl.BlockSpec((tm, LANES), lambda i: (i, OFF_MKRR // LANES)),
                  _layer_full(gq, layer), _layer_full(gkv, layer),
                  _layer_full(wq, layer), _layer_full(wqr, layer),
                  _layer_full(wk, layer), _layer_full(wv, layer),
                  pl.BlockSpec((tm, LANES), lambda i: (i % ns, 0)),
                  pl.BlockSpec((tm, LANES), lambda i: (i % ns, 0))],
        out_specs=[pl.BlockSpec((tm, MLA_HEADS * MLA_QK_PAD), row),
                   pl.BlockSpec((tm, MLA_HEADS * MLA_QK_PAD), row),
                   pl.BlockSpec((tm, MLA_W), row)],
        out_shape=[jax.ShapeDtypeStruct((t, MLA_HEADS * MLA_QK_PAD), BF16),
                   jax.ShapeDtypeStruct((t, MLA_HEADS * MLA_QK_PAD), BF16),
                   jax.ShapeDtypeStruct((t, MLA_W), BF16)],
        compiler_params=_params("parallel"),
        name="mla_prep",
    )(p, p, p, p, gq, gkv, wq, wqr, wk, wv, cos, sin)


def _attn_kernel(*refs, heads, dq, dv, coef, has_bias, tq):
    if has_bias:
        q_ref, k_ref, v_ref, rel_ref, kb_ref, o_ref, m_ref, l_ref, acc_ref = refs
    else:
        q_ref, k_ref, v_ref, rel_ref, o_ref, m_ref, l_ref, acc_ref = refs
    qi = pl.program_id(2)

    def accumulate(k0, visible, first):
        rows = pl.ds(k0, tq)
        ones_v = jnp.ones((tq, LANES), BF16)
        if has_bias:
            lane = lax.broadcasted_iota(jnp.int32, (tq, LANES), 1)
            ones_q = jnp.where(lane < BIAS_TERMS, 1.0, 0.0).astype(BF16)
        for h in range(heads):
            q = q_ref[:, h * dq:(h + 1) * dq]
            k = k_ref[rows, h * dq:(h + 1) * dq]
            if has_bias:
                q = jnp.concatenate([q, ones_q], axis=1)
                k = jnp.concatenate([k, kb_ref[rows, h * LANES:(h + 1) * LANES]], axis=1)
            s = lax.dot_general(q, k, (((1,), (1,)), ((), ())), preferred_element_type=F32)
            if visible is not None:
                s = jnp.where(visible, s, -jnp.inf)
            m_new = jnp.broadcast_to(jnp.max(s, axis=1, keepdims=True), (tq, LANES))
            if not first:
                m_prev = m_ref[h]
                m_new = jnp.maximum(m_prev, m_new)
                alpha = jnp.exp2((m_prev - m_new) * coef)
            p = jnp.concatenate(
                [jnp.exp2((s[:, j * LANES:(j + 1) * LANES] - m_new) * coef) for j in range(tq // LANES)],
                axis=1).astype(BF16)
            v1 = jnp.concatenate([v_ref[rows, h * dv:(h + 1) * dv], ones_v], axis=1)
            pv = jnp.dot(p, v1, preferred_element_type=F32)
            if first:
                acc_ref[:, h * dv:(h + 1) * dv] = pv[:, :dv]
                l_ref[h] = pv[:, dv:]
            else:
                acc_ref[:, h * dv:(h + 1) * dv] = alpha * acc_ref[:, h * dv:(h + 1) * dv] + pv[:, :dv]
                l_ref[h] = alpha * l_ref[h] + pv[:, dv:]
            m_ref[h] = m_new

    accumulate(pl.multiple_of(qi * tq, tq), rel_ref[...] >= 0, True)

    def below_diagonal_pair(pi, carry):
        k0 = pl.multiple_of(pi * (2 * tq), 2 * tq)
        accumulate(k0, None, False)
        accumulate(pl.multiple_of(k0 + tq, tq), None, False)
        return carry

    lax.fori_loop(0, qi // 2, below_diagonal_pair, 0)

    @pl.when(qi % 2 == 1)
    def _():
        accumulate(pl.multiple_of((qi - 1) * tq, tq), None, False)

    for h in range(heads):
        o_ref[:, h * dv:(h + 1) * dv] = (acc_ref[:, h * dv:(h + 1) * dv] / l_ref[h]).astype(o_ref.dtype)


def _attention(q_arr, q_cb, k_arr, k_cb, v_arr, v_cb, rel, kbias, *, batch, seq, heads, dq, dv, scale, name):
    assert dv == LANES and heads % ATTN_HEAD_GROUPS == 0
    tq = rel.shape[0]
    nq = seq // tq
    ng = ATTN_HEAD_GROUPS
    hg = heads // ng
    has_bias = kbias is not None
    in_specs = [pl.BlockSpec((tq, hg * dq), lambda b, g, qi: (b * nq + qi, q_cb * ng + g)),
                pl.BlockSpec((seq, hg * dq), lambda b, g, qi: (b, k_cb * ng + g)),
                pl.BlockSpec((seq, hg * dv), lambda b, g, qi: (b, v_cb * ng + g)),
                pl.BlockSpec((tq, tq), lambda b, g, qi: (0, 0))]
    args = [q_arr, k_arr, v_arr, rel]
    if has_bias:
        in_specs.append(pl.BlockSpec((seq, hg * LANES), lambda b, g, qi: (b, g)))
        args.append(kbias)
    return pl.pallas_call(
        functools.partial(_attn_kernel, heads=hg, dq=dq, dv=dv, coef=scale * math.log2(math.e),
                          has_bias=has_bias, tq=tq),
        grid=(batch, ng, nq),
        in_specs=in_specs,
        out_specs=pl.BlockSpec((tq, hg * dv), lambda b, g, qi: (b * nq + qi, g)),
        out_shape=jax.ShapeDtypeStruct((batch * seq, heads * dv), BF16),
        scratch_shapes=[pltpu.VMEM((hg, tq, LANES), F32),
                        pltpu.VMEM((hg, tq, LANES), F32),
                        pltpu.VMEM((tq, hg * dv), F32)],
        compiler_params=_params("parallel", "parallel", "arbitrary"),
        name=name,
    )(*args)


def _retention_kernel(q_ref, k_ref, v_ref, g_ref, cos_ref, sin_ref, dec_ref, sin_state_ref,
                      cross_ref, o_ref, state_ref, *, block_decay):
    @pl.when(pl.program_id(1) == 0)
    def _():
        state_ref[...] = jnp.zeros_like(state_ref)

    cos = cos_ref[...]
    sin = sin_ref[...]

    def rope(x):
        return x * cos + pltpu.roll(x, RET_DK // 2, axis=1) * sin

    for h in range(RET_HEADS):
        qs = slice(h * RET_DK, (h + 1) * RET_DK)
        vs = slice(h * RET_DV, (h + 1) * RET_DV)
        q = rope(q_ref[:, qs].astype(F32)).astype(BF16)
        kf = rope(k_ref[:, qs].astype(F32)) * (RET_DK ** -0.5)
        v = v_ref[:, vs]
        scores = lax.dot_general(q, kf.astype(BF16), (((1,), (1,)), ((), ())),
                                 preferred_element_type=F32) * dec_ref[h]
        o = jnp.dot(scores.astype(BF16), v, preferred_element_type=F32)
        state = state_ref[h]
        cross = jnp.dot(q, state.astype(BF16), preferred_element_type=F32)
        cd = cross_ref[h]
        o = o + cross * jnp.concatenate([cd, cd], axis=1)
        o = o * lax.rsqrt(jnp.mean(o * o, axis=-1, keepdims=True) + NORM_EPS)
        g = g_ref[:, vs].astype(F32)
        o_ref[:, vs] = (o * (g * jax.nn.sigmoid(g))).astype(o_ref.dtype)
        ks = (kf * sin_state_ref[h]).astype(BF16)
        kv = lax.dot_general(ks, v, (((0,), (0,)), ((), ())), preferred_element_type=F32)
        state_ref[h] = block_decay[h] * state + kv


def _retention(p, cos, sin, dec, sin_state, cross, block_decay, batch, seq):
    blk = dec.shape[1]
    nb = seq // blk
    row = lambda b, n: (b * nb + n, 0)
    const3 = lambda b, n: (0, 0, 0)
    return pl.pallas_call(
        functools.partial(_retention_kernel, block_decay=block_decay),
        grid=(batch, nb),
        in_specs=[pl.BlockSpec((blk, RET_QK_W), lambda b, n: (b * nb + n, OFF_RQ // RET_QK_W)),
                  pl.BlockSpec((blk, RET_QK_W), lambda b, n: (b * nb + n, OFF_RK // RET_QK_W)),
                  pl.BlockSpec((blk, RET_V_W), lambda b, n: (b * nb + n, OFF_RV // RET_V_W)),
                  pl.BlockSpec((blk, RET_V_W), lambda b, n: (b * nb + n, OFF_RG // RET_V_W)),
                  pl.BlockSpec((blk, RET_DK), lambda b, n: (n, 0)),
                  pl.BlockSpec((blk, RET_DK), lambda b, n: (n, 0)),
                  pl.BlockSpec(dec.shape, const3),
                  pl.BlockSpec(sin_state.shape, const3),
                  pl.BlockSpec(cross.shape, const3)],
        out_specs=pl.BlockSpec((blk, RET_V_W), row),
        out_shape=jax.ShapeDtypeStruct((batch * seq, RET_V_W), BF16),
        scratch_shapes=[pltpu.VMEM((RET_HEADS, RET_DK, RET_DV), F32)],
        compiler_params=_params("parallel", "arbitrary"),
        name="retention",
    )(p, p, p, p, cos, sin, dec, sin_state, cross)


def _gated_merge_kernel(h_ref, a_ref, b_ref, c_ref, wg0_ref, wg1_ref, wg2_ref,
                        wa_ref, wb_ref, wc_ref, o_ref):
    h = h_ref[...]

    def branch(wg_ref, br_ref, w_ref):
        gate = jax.nn.sigmoid(_dot_nt(h, wg_ref[...]))
        return gate * jnp.dot(br_ref[...], w_ref[...], preferred_element_type=F32)

    m = branch(wg0_ref, a_ref, wa_ref)
    m = m + branch(wg1_ref, b_ref, wb_ref)
    m = m + branch(wg2_ref, c_ref, wc_ref)
    o_ref[...] = m.astype(o_ref.dtype)


def _gated_merge(h, a, bm, c, w_gates, wa, wb, wc, layer):
    t, d = h.shape
    tm = _tile(t, 1024, 8)
    tn = _tile(d, 512, LANES)
    nj = d // tn
    act = lambda arr: pl.BlockSpec((tm, arr.shape[1]), lambda i, j: (i, 0))
    wgt = lambda w: _layer_spec(w, layer, (w.shape[1], tn), lambda i, j: (0, j))
    gate = lambda n: _layer_spec(w_gates, layer, (tn, d), lambda i, j: (n * nj + j, 0))
    return pl.pallas_call(
        _gated_merge_kernel,
        grid=(t // tm, nj),
        in_specs=[act(h), act(a), act(bm), act(c), gate(0), gate(1), gate(2), wgt(wa), wgt(wb), wgt(wc)],
        out_specs=pl.BlockSpec((tm, tn), lambda i, j: (i, j)),
        out_shape=jax.ShapeDtypeStruct((t, d), BF16),
        compiler_params=_params("parallel", "arbitrary"),
        name="gated_merge",
    )(h, a, bm, c, w_gates, w_gates, w_gates, wa, wb, wc)


def _mm_res_kernel(a_ref, w_ref, r_ref, o_ref):
    o_ref[...] = r_ref[...] + jnp.dot(a_ref[...], w_ref[...], preferred_element_type=F32)


def _matmul_residual(a, w, r, layer, tm_pref, tn_pref, name):
    t, k = a.shape
    n = w.shape[2]
    tm = _tile(t, tm_pref, 8)
    tn = _tile(n, tn_pref, LANES)
    return pl.pallas_call(
        _mm_res_kernel,
        grid=(n // tn, t // tm),
        in_specs=[pl.BlockSpec((tm, k), lambda j, i: (i, 0)),
                  _layer_spec(w, layer, (k, tn), lambda j, i: (0, j)),
                  pl.BlockSpec((tm, tn), lambda j, i: (i, j))],
        out_specs=pl.BlockSpec((tm, tn), lambda j, i: (i, j)),
        out_shape=jax.ShapeDtypeStruct((t, n), F32),
        compiler_params=_params("parallel", "arbitrary"),
        name=name,
    )(a, w, r)


def _ffn_up_kernel(x_ref, xh_ref, g_ref, wu_ref, wg_ref, cw_ref, cb_ref, o_ref, h_ref, *, seq):
    tm = x_ref.shape[0]

    @pl.when(pl.program_id(1) == 0)
    def _():
        keep = ((pl.program_id(0) * tm) % seq != 0).astype(F32)
        h_ref[0:HALO, :] = (_rms(xh_ref[...], g_ref[...]) * keep).astype(BF16)
        _norm_rows(x_ref, g_ref, h_ref, tm, h_off=HALO)

    u = jnp.dot(h_ref[...], wu_ref[...], preferred_element_type=F32)
    gate = jnp.dot(h_ref[HALO:, :], wg_ref[...], preferred_element_type=F32)
    conv = (cb_ref[...] + cw_ref[2:3, :] * u
            + cw_ref[1:2, :] * pltpu.roll(u, 1, axis=0)
            + cw_ref[0:1, :] * pltpu.roll(u, 2, axis=0))
    o_ref[...] = (jax.nn.gelu(conv[HALO:, :]) * gate).astype(o_ref.dtype)


def _ffn_up(x, g, wu, wg, cw, cb, layer, seq):
    t, d = x.shape
    f = wu.shape[2]
    tm = _tile(seq, 1024, HALO)
    tf = _tile(f, 512, LANES)
    hb = tm // HALO
    col = lambda i, j: (0, j)
    return pl.pallas_call(
        functools.partial(_ffn_up_kernel, seq=seq),
        grid=(t // tm, f // tf),
        in_specs=[pl.BlockSpec((tm, d), lambda i, j: (i, 0)),
                  pl.BlockSpec((HALO, d), lambda i, j: (jnp.maximum(i * hb - 1, 0), 0)),
                  _layer_full(g, layer),
                  _layer_spec(wu, layer, (d, tf), col),
                  _layer_spec(wg, layer, (d, tf), col),
                  _layer_spec(cw, layer, (8, tf), col),
                  _layer_spec(cb, layer, (1, tf), col)],
        out_specs=pl.BlockSpec((tm, tf), lambda i, j: (i, j)),
        out_shape=jax.ShapeDtypeStruct((t, f), BF16),
        scratch_shapes=[pltpu.VMEM((HALO + tm, d), BF16)],
        compiler_params=_params("parallel", "arbitrary"),
        name="ffn_up_conv_gate",
    )(x, x, g, wu, wg, cw, cb)


def _final_norm_kernel(x_ref, g_ref, o_ref):
    o_ref[...] = _rms(x_ref[...], g_ref[...])


def _final_norm(x, g):
    t, d = x.shape
    tm = _tile(t, 512, 8)
    return pl.pallas_call(
        _final_norm_kernel,
        grid=(t // tm,),
        in_specs=[pl.BlockSpec((tm, d), lambda i: (i, 0)), pl.BlockSpec((1, d), lambda i: (0, 0))],
        out_specs=pl.BlockSpec((tm, d), lambda i: (i, 0)),
        out_shape=jax.ShapeDtypeStruct((t, d), F32),
        compiler_params=_params("parallel"),
        name="final_norm",
    )(x, g)


def _rope_tables(seq, d):
    pos = jnp.arange(seq, dtype=F32)
    inv_freq = ROPE_THETA ** (-jnp.arange(0, d, 2, dtype=F32) / d)
    ang = pos[:, None] * inv_freq[None, :]
    return jnp.cos(ang), jnp.sin(ang)


def _retention_tables(blk):
    log_gamma = jnp.log(1.0 - 2.0 ** (-5.0 - jnp.arange(RET_HEADS, dtype=F32)))
    idx = jnp.arange(blk, dtype=F32)
    chunk = jnp.arange(blk) // CHUNK
    dist = jnp.abs(idx[:, None] - idx[None, :])
    dec = jnp.where(chunk[None, :] <= chunk[:, None],
                    jnp.exp(log_gamma[:, None, None] * dist[None]), 0.0)
    sin_state = jnp.exp(log_gamma[:, None] * (blk - 1 - idx)[None, :])
    cross = jnp.exp(log_gamma[:, None] * (idx + 1.0)[None, :])
    bcast = lambda a: jnp.broadcast_to(a[:, :, None], (RET_HEADS, blk, RET_DK))
    block_decay = tuple(math.exp(math.log(1.0 - 2.0 ** (-5.0 - h)) * blk) for h in range(RET_HEADS))
    return dec, bcast(sin_state), bcast(cross), block_decay


def _rotate_half_cols(w):
    half = w.shape[-1] // 2
    return jnp.concatenate([-w[..., half:], w[..., :half]], axis=-1)


def kernel(x, norm1_g, w_in, mla_q_norm_g, mla_kv_norm_g, mla_w_uq, mla_w_ukv, fox_b_f,
           w_br_fox, w_br_mla, w_br_ret, w_out, norm2_g, ffn_w_up, ffn_w_gate,
           ffn_conv_w, ffn_conv_b, ffn_w_down, final_norm_g):
    batch, seq, d = x.shape
    depth = w_in.shape[0]
    t = batch * seq
    q_lora = mla_w_uq.shape[1]
    kv_lora = mla_w_ukv.shape[1]

    widths = (FOX_W, FOX_W, FOX_W, FOX_HEADS, q_lora, kv_lora, MLA_ROPE,
              RET_QK_W, RET_QK_W, RET_V_W, RET_V_W, N_BRANCH * d)
    offs = [0]
    for wd in widths:
        offs.append(offs[-1] + wd)
    w_in_t = jnp.swapaxes(w_in, 1, 2).astype(BF16)
    seg = lambda k: w_in_t[:, offs[k]:offs[k + 1], :]
    zeros = lambda n: jnp.zeros((depth, n, d), BF16)
    rope_t = seg(6)
    tail = jnp.concatenate(
        [rope_t, seg(3), zeros(LANES - MLA_ROPE - FOX_HEADS),
         -rope_t[:, MLA_ROPE // 2:], rope_t[:, :MLA_ROPE // 2], zeros(LANES - MLA_ROPE)], axis=1)
    w_p = jnp.concatenate(
        [seg(0), seg(1), seg(2), seg(5), seg(4), seg(9), seg(10), seg(7), seg(8), tail], axis=1)
    assert w_p.shape[1] == P_WIDTH and q_lora == 512 and kv_lora == 256
    w_gates = seg(11)
    b_ff = jnp.pad(fox_b_f, ((0, 0), (FF_LANE, LANES - FF_LANE - FOX_HEADS)))[:, None, :]

    uq = mla_w_uq.reshape(depth, q_lora, MLA_HEADS, MLA_NOPE + MLA_ROPE)
    uq_rope = uq[..., MLA_NOPE:]
    pad_q = jnp.zeros((depth, q_lora, MLA_HEADS, MLA_QK_PAD - MLA_NOPE - MLA_ROPE), uq.dtype)
    w_q = jnp.concatenate([uq, pad_q], axis=-1).reshape(depth, q_lora, -1).astype(BF16)
    pad_r = jnp.zeros((depth, q_lora, MLA_HEADS, LANES - MLA_ROPE), uq.dtype)
    w_qr = jnp.concatenate([_rotate_half_cols(uq_rope), pad_r], axis=-1).reshape(depth, q_lora, -1).astype(BF16)
    ukv = mla_w_ukv.reshape(depth, kv_lora, MLA_HEADS, MLA_NOPE + MLA_V)
    w_k = ukv[..., :MLA_NOPE].reshape(depth, kv_lora, -1).astype(BF16)
    w_v = ukv[..., MLA_NOPE:].reshape(depth, kv_lora, -1).astype(BF16)

    w_bf, w_bm, w_br = w_br_fox.astype(BF16), w_br_mla.astype(BF16), w_br_ret.astype(BF16)
    w_o = w_out.astype(BF16)
    w_up, w_gate, w_down = ffn_w_up.astype(BF16), ffn_w_gate.astype(BF16), ffn_w_down.astype(BF16)
    conv_w = jnp.pad(ffn_conv_w, ((0, 0), (0, 8 - CONV_W), (0, 0)))
    conv_b = ffn_conv_b[:, None, :]
    g1, g2 = norm1_g[:, None, :], norm2_g[:, None, :]
    gq, gkv = mla_q_norm_g[:, None, :], mla_kv_norm_g[:, None, :]

    cos64, sin64 = _rope_tables(seq, MLA_ROPE)
    zpad = jnp.zeros((seq, LANES - MLA_ROPE), F32)
    mla_cos = jnp.concatenate([cos64, cos64, zpad], axis=1)
    mla_sin = jnp.concatenate([sin64, sin64, zpad], axis=1)
    cos128, sin128 = _rope_tables(seq, RET_DK)
    ret_cos = jnp.concatenate([cos128, cos128], axis=1)
    ret_sin = jnp.concatenate([-sin128, sin128], axis=1)
    ret_blk = _tile(seq, 512, CHUNK)
    dec, sin_state, cross, block_decay = _retention_tables(ret_blk)

    att_blk = _tile(seq, 512, CHUNK)
    row = jnp.arange(att_blk, dtype=jnp.int32)[:, None]
    col = jnp.arange(att_blk, dtype=jnp.int32)[None, :]
    rel_frame = row - col
    rel_chunk = (row // CHUNK) * CHUNK + (CHUNK - 1) - col

    xf = x.reshape(t, d)
    for i in range(depth):
        p, f_logit, h1 = _in_proj(xf, g1, w_p, i)
        kbias = _fox_decay(f_logit, b_ff, i, batch, seq)
        a = _attention(p, OFF_FQ // FOX_W, p, OFF_FK // FOX_W, p, OFF_FV // FOX_W, rel_frame, kbias,
                       batch=batch, seq=seq, heads=FOX_HEADS, dq=FOX_DH, dv=FOX_DH,
                       scale=FOX_DH ** -0.5, name="fox_attention")
        qm, km, vm = _mla_prep(p, gq, gkv, w_q, w_qr, w_k, w_v, mla_cos, mla_sin, i, seq)
        bm = _attention(qm, 0, km, 0, vm, 0, rel_chunk, None,
                        batch=batch, seq=seq, heads=MLA_HEADS, dq=MLA_QK_PAD, dv=MLA_V,
                        scale=(MLA_NOPE + MLA_ROPE) ** -0.5, name="mla_attention")
        cr = _retention(p, ret_cos, ret_sin, dec, sin_state, cross, block_decay, batch, seq)
        merged = _gated_merge(h1, a, bm, cr, w_gates, w_bf, w_bm, w_br, i)
        xf = _matmul_residual(merged, w_o, xf, i, 1024, 1024, "out_proj")
        act = _ffn_up(xf, g2, w_up, w_gate, conv_w, conv_b, i, seq)
        xf = _matmul_residual(act, w_down, xf, i, 512, 1024, "ffn_down")
    return _final_norm(xf, final_norm_g[None, :]).reshape(batch, seq, d)
```

```python
import functools
import math

import jax
import jax.numpy as jnp
from jax import lax
from jax.experimental import pallas as pl
from jax.experimental.pallas import tpu as pltpu

F32 = jnp.float32
BF16 = jnp.bfloat16

NORM_EPS = 1e-6
ROPE_THETA = 10000.0
CHUNK = 64

FOX_HEADS, FOX_DH = 6, 128
MLA_HEADS, MLA_NOPE, MLA_ROPE, MLA_V = 6, 128, 64, 128
MLA_QK_PAD = 256
RET_HEADS, RET_DK, RET_DV = 4, 128, 256
N_BRANCH = 3
CONV_W = 3
LANES = 128
HALO = 16
BIAS_TERMS = 3
ATTN_HEAD_GROUPS = 1

FOX_W = FOX_HEADS * FOX_DH
MLA_W = MLA_HEADS * MLA_V
RET_QK_W = RET_HEADS * RET_DK
RET_V_W = RET_HEADS * RET_DV

OFF_RQ = 0
OFF_RK = OFF_RQ + RET_QK_W
OFF_RV = OFF_RK + RET_QK_W
OFF_RG = OFF_RV + RET_V_W
OFF_FQ = OFF_RG + RET_V_W
OFF_FK = OFF_FQ + FOX_W
OFF_FV = OFF_FK + FOX_W
OFF_MKV = OFF_FV + FOX_W
OFF_MQ = OFF_MKV + 256
OFF_MKR = OFF_MQ + 512
FF_LANE = MLA_ROPE
OFF_MKRR = OFF_MKR + LANES
P_WIDTH = OFF_MKRR + LANES

VMEM_LIMIT_BYTES = 56 * 1024 * 1024


def _tile(n, pref, align):
    t = (min(pref, n) // align) * align
    while t > align and n % t:
        t -= align
    assert t >= align and n % t == 0, (n, pref, align)
    return t


def _params(*sem):
    return pltpu.CompilerParams(dimension_semantics=sem, vmem_limit_bytes=VMEM_LIMIT_BYTES)


def _layer_spec(arr, layer, block, index):
    return pl.BlockSpec((None,) + tuple(block), lambda *ids: (layer,) + tuple(index(*ids)))


def _layer_full(arr, layer):
    zeros = (0,) * (arr.ndim - 1)
    return pl.BlockSpec((None,) + arr.shape[1:], lambda *ids: (layer,) + zeros)


def _rms(x, g):
    return x * lax.rsqrt(jnp.mean(x * x, axis=-1, keepdims=True) + NORM_EPS) * g


def _norm_rows(x_ref, g_ref, h_ref, rows, h_off=0):
    rc = _tile(rows, 256, 8)

    def body(c, carry):
        r0 = pl.multiple_of(c * rc, rc)
        h_ref[pl.ds(h_off + r0, rc), :] = _rms(x_ref[pl.ds(r0, rc), :], g_ref[...]).astype(BF16)
        return carry

    lax.fori_loop(0, rows // rc, body, 0)


def _dot_nt(a, b):
    return lax.dot_general(a, b, (((1,), (1,)), ((), ())), preferred_element_type=F32)


def _in_proj_kernel(x_ref, g_ref, w_ref, o_ref, f_ref, h_ref, *, f_tile, f_col):
    @pl.when(pl.program_id(1) == 0)
    def _():
        _norm_rows(x_ref, g_ref, h_ref, x_ref.shape[0])

    y = _dot_nt(h_ref[...], w_ref[...])
    o_ref[...] = y.astype(o_ref.dtype)

    @pl.when(pl.program_id(1) == f_tile)
    def _():
        f_ref[...] = y[:, f_col:f_col + LANES]


def _in_proj(x, g, w, layer):
    t, d = x.shape
    n = w.shape[1]
    tm = _tile(t, 1024, 8)
    tn = _tile(n, 1280, LANES)
    return pl.pallas_call(
        functools.partial(_in_proj_kernel, f_tile=OFF_MKR // tn, f_col=OFF_MKR % tn),
        grid=(t // tm, n // tn),
        in_specs=[pl.BlockSpec((tm, d), lambda i, j: (i, 0)),
                  _layer_full(g, layer),
                  _layer_spec(w, layer, (tn, d), lambda i, j: (j, 0))],
        out_specs=[pl.BlockSpec((tm, tn), lambda i, j: (i, j)),
                   pl.BlockSpec((tm, LANES), lambda i, j: (i, 0)),
                   pl.BlockSpec((tm, d), lambda i, j: (i, 0))],
        out_shape=[jax.ShapeDtypeStruct((t, n), BF16), jax.ShapeDtypeStruct((t, LANES), F32),
                   jax.ShapeDtypeStruct((t, d), BF16)],
        compiler_params=_params("parallel", "arbitrary"),
        name="in_proj",
    )(x, g, w)


def _fox_decay_kernel(f_ref, b_ref, tri_ref, o_ref, carry_ref):
    @pl.when(pl.program_id(1) == 0)
    def _():
        carry_ref[...] = jnp.zeros_like(carry_ref)

    f = f_ref[...] + b_ref[...]
    lf = jnp.minimum(f, 0.0) - jnp.log1p(jnp.exp(-jnp.abs(f)))
    lane = lax.broadcasted_iota(jnp.int32, lf.shape, 1)
    hi = lf.astype(BF16).astype(F32)
    r1 = lf - hi
    mid = r1.astype(BF16).astype(F32)
    lo = r1 - mid
    packed = jnp.where(lane < FF_LANE + FOX_HEADS, hi,
                       jnp.where(lane < FF_LANE + 2 * FOX_HEADS, pltpu.roll(mid, FOX_HEADS, axis=1),
                                 pltpu.roll(lo, 2 * FOX_HEADS, axis=1)))
    cp = jnp.dot(tri_ref[...], packed.astype(BF16), preferred_element_type=F32)
    c = (cp + pltpu.roll(cp, LANES - FOX_HEADS, axis=1)
         + pltpu.roll(cp, LANES - 2 * FOX_HEADS, axis=1)) + carry_ref[0:1, :]
    carry_ref[...] = jnp.broadcast_to(c[-1:, :], carry_ref.shape)
    for h in range(FOX_HEADS):
        src = FF_LANE + h
        rest = jnp.broadcast_to(c[:, src:src + 1], lane.shape) * (-(FOX_DH ** 0.5))
        slab = jnp.zeros(lane.shape, F32)
        for term in range(BIAS_TERMS):
            part = rest.astype(BF16).astype(F32)
            slab = jnp.where(lane == term, part, slab)
            rest = rest - part
        o_ref[:, h * LANES:(h + 1) * LANES] = slab.astype(BF16)


def _fox_decay(f, b_ff, layer, batch, seq):
    t = f.shape[0]
    ts = _tile(seq, 1024, 8)
    nb = seq // ts
    tri = (jnp.arange(ts)[:, None] >= jnp.arange(ts)[None, :]).astype(BF16)
    return pl.pallas_call(
        _fox_decay_kernel,
        grid=(batch, nb),
        in_specs=[pl.BlockSpec((ts, LANES), lambda b, n: (b * nb + n, 0)),
                  _layer_full(b_ff, layer),
                  pl.BlockSpec((ts, ts), lambda b, n: (0, 0))],
        out_specs=pl.BlockSpec((ts, FOX_HEADS * LANES), lambda b, n: (b * nb + n, 0)),
        out_shape=jax.ShapeDtypeStruct((t, FOX_HEADS * LANES), BF16),
        scratch_shapes=[pltpu.VMEM((8, LANES), F32)],
        compiler_params=_params("parallel", "arbitrary"),
        name="fox_decay_cumsum",
    )(f, b_ff, tri)


def _mla_prep_kernel(mq_ref, mkv_ref, kr_ref, krr_ref, gq_ref, gkv_ref, wq_ref, wqr_ref,
                     wk_ref, wv_ref, cos_ref, sin_ref, q_ref, k_ref, v_ref):
    cos = cos_ref[...]
    sin = sin_ref[...]
    cqn = _rms(mq_ref[...].astype(F32), gq_ref[...]).astype(BF16)
    q = jnp.dot(cqn, wq_ref[...], preferred_element_type=F32)
    qrot = jnp.dot(cqn, wqr_ref[...], preferred_element_type=F32)
    ckvn = _rms(mkv_ref[...].astype(F32), gkv_ref[...]).astype(BF16)
    kn = jnp.dot(ckvn, wk_ref[...], preferred_element_type=F32)
    v_ref[...] = jnp.dot(ckvn, wv_ref[...], preferred_element_type=F32).astype(BF16)
    kr = (kr_ref[...].astype(F32) * cos + krr_ref[...].astype(F32) * sin).astype(BF16)
    for h in range(MLA_HEADS):
        lo = h * MLA_QK_PAD
        q_ref[:, lo:lo + LANES] = q[:, lo:lo + LANES].astype(BF16)
        q_ref[:, lo + LANES:lo + 2 * LANES] = (
            q[:, lo + LANES:lo + 2 * LANES] * cos + qrot[:, h * LANES:(h + 1) * LANES] * sin).astype(BF16)
        k_ref[:, lo:lo + LANES] = kn[:, h * LANES:(h + 1) * LANES].astype(BF16)
        k_ref[:, lo + LANES:lo + 2 * LANES] = kr


def _mla_prep(p, gq, gkv, wq, wqr, wk, wv, cos, sin, layer, seq):
    t = p.shape[0]
    tm = _tile(seq, 512, 8)
    ns = seq // tm
    ql, kvl = wq.shape[1], wk.shape[1]
    row = lambda i: (i, 0)
    return pl.pallas_call(
        _mla_prep_kernel,
        grid=(t // tm,),
        in_specs=[pl.BlockSpec((tm, ql), lambda i: (i, OFF_MQ // ql)),
                  pl.BlockSpec((tm, kvl), lambda i: (i, OFF_MKV // kvl)),
                  pl.BlockSpec((tm, LANES), lambda i: (i, OFF_MKR // LANES)),
                  pl.BlockSpec((tm, LANES), lambda i: (i, OFF_MKRR // LANES)),
                  _layer_full(gq, layer), _layer_full(gkv, layer),
                  _layer_full(wq, layer), _layer_full(wqr, layer),
                  _layer_full(wk, layer), _layer_full(wv, layer),
                  pl.BlockSpec((tm, LANES), lambda i: (i % ns, 0)),
                  pl.BlockSpec((tm, LANES), lambda i: (i % ns, 0))],
        out_specs=[pl.BlockSpec((tm, MLA_HEADS * MLA_QK_PAD), row),
                   pl.BlockSpec((tm, MLA_HEADS * MLA_QK_PAD), row),
                   pl.BlockSpec((tm, MLA_W), row)],
        out_shape=[jax.ShapeDtypeStruct((t, MLA_HEADS * MLA_QK_PAD), BF16),
                   jax.ShapeDtypeStruct((t, MLA_HEADS * MLA_QK_PAD), BF16),
                   jax.ShapeDtypeStruct((t, MLA_W), BF16)],
        compiler_params=_params("parallel"),
        name="mla_prep",
    )(p, p, p, p, gq, gkv, wq, wqr, wk, wv, cos, sin)


def _attn_kernel(*refs, heads, dq, dv, coef, has_bias, tq):
    if has_bias:
        q_ref, k_ref, v_ref, rel_ref, kb_ref, o_ref, m_ref, l_ref, acc_ref = refs
    else:
        q_ref, k_ref, v_ref, rel_ref, o_ref, m_ref, l_ref, acc_ref = refs
    qi = pl.program_id(2)

    def accumulate(k0, visible, first):
        rows = pl.ds(k0, tq)
        ones_v = jnp.ones((tq, LANES), BF16)
        if has_bias:
            lane = lax.broadcasted_iota(jnp.int32, (tq, LANES), 1)
            ones_q = jnp.where(lane < BIAS_TERMS, 1.0, 0.0).astype(BF16)
        for h in range(heads):
            q = q_ref[:, h * dq:(h + 1) * dq]
            k = k_ref[rows, h * dq:(h + 1) * dq]
            if has_bias:
                q = jnp.concatenate([q, ones_q], axis=1)
                k = jnp.concatenate([k, kb_ref[rows, h * LANES:(h + 1) * LANES]], axis=1)
            s = lax.dot_general(q, k, (((1,), (1,)), ((), ())), preferred_element_type=F32)
            if visible is not None:
                s = jnp.where(visible, s, -jnp.inf)
            m_new = jnp.broadcast_to(jnp.max(s, axis=1, keepdims=True), (tq, LANES))
            if not first:
                m_prev = m_ref[h]
                m_new = jnp.maximum(m_prev, m_new)
                alpha = jnp.exp2((m_prev - m_new) * coef)
            p = jnp.concatenate(
                [jnp.exp2((s[:, j * LANES:(j + 1) * LANES] - m_new) * coef) for j in range(tq // LANES)],
                axis=1).astype(BF16)
            v1 = jnp.concatenate([v_ref[rows, h * dv:(h + 1) * dv], ones_v], axis=1)
            pv = jnp.dot(p, v1, preferred_element_type=F32)
            if first:
                acc_ref[:, h * dv:(h + 1) * dv] = pv[:, :dv]
                l_ref[h] = pv[:, dv:]
            else:
                acc_ref[:, h * dv:(h + 1) * dv] = alpha * acc_ref[:, h * dv:(h + 1) * dv] + pv[:, :dv]
                l_ref[h] = alpha * l_ref[h] + pv[:, dv:]
            m_ref[h] = m_new

    accumulate(pl.multiple_of(qi * tq, tq), rel_ref[...] >= 0, True)

    def below_diagonal_pair(pi, carry):
        k0 = pl.multiple_of(pi * (2 * tq), 2 * tq)
        accumulate(k0, None, False)
        accumulate(pl.multiple_of(k0 + tq, tq), None, False)
        return carry

    lax.fori_loop(0, qi // 2, below_diagonal_pair, 0)

    @pl.when(qi % 2 == 1)
    def _():
        accumulate(pl.multiple_of((qi - 1) * tq, tq), None, False)

    for h in range(heads):
        o_ref[:, h * dv:(h + 1) * dv] = (acc_ref[:, h * dv:(h + 1) * dv] / l_ref[h]).astype(o_ref.dtype)


def _attention(q_arr, q_cb, k_arr, k_cb, v_arr, v_cb, rel, kbias, *, batch, seq, heads, dq, dv, scale, name):
    assert dv == LANES and heads % ATTN_HEAD_GROUPS == 0
    tq = rel.shape[0]
    nq = seq // tq
    ng = ATTN_HEAD_GROUPS
    hg = heads // ng
    has_bias = kbias is not None
    in_specs = [pl.BlockSpec((tq, hg * dq), lambda b, g, qi: (b * nq + qi, q_cb * ng + g)),
                pl.BlockSpec((seq, hg * dq), lambda b, g, qi: (b, k_cb * ng + g)),
                pl.BlockSpec((seq, hg * dv), lambda b, g, qi: (b, v_cb * ng + g)),
                pl.BlockSpec((tq, tq), lambda b, g, qi: (0, 0))]
    args = [q_arr, k_arr, v_arr, rel]
    if has_bias:
        in_specs.append(pl.BlockSpec((seq, hg * LANES), lambda b, g, qi: (b, g)))
        args.append(kbias)
    return pl.pallas_call(
        functools.partial(_attn_kernel, heads=hg, dq=dq, dv=dv, coef=scale * math.log2(math.e),
                          has_bias=has_bias, tq=tq),
        grid=(batch, ng, nq),
        in_specs=in_specs,
        out_specs=pl.BlockSpec((tq, hg * dv), lambda b, g, qi: (b * nq + qi, g)),
        out_shape=jax.ShapeDtypeStruct((batch * seq, heads * dv), BF16),
        scratch_shapes=[pltpu.VMEM((hg, tq, LANES), F32),
                        pltpu.VMEM((hg, tq, LANES), F32),
                        pltpu.VMEM((tq, hg * dv), F32)],
        compiler_params=_params("parallel", "parallel", "arbitrary"),
        name=name,
    )(*args)


def _retention_kernel(q_ref, k_ref, v_ref, g_ref, cos_ref, sin_ref, dec_ref, sin_state_ref,
                      cross_ref, o_ref, state_ref, *, block_decay):
    @pl.when(pl.program_id(1) == 0)
    def _():
        state_ref[...] = jnp.zeros_like(state_ref)

    cos = cos_ref[...]
    sin = sin_ref[...]

    def rope(x):
        return x * cos + pltpu.roll(x, RET_DK // 2, axis=1) * sin

    for h in range(RET_HEADS):
        qs = slice(h * RET_DK, (h + 1) * RET_DK)
        vs = slice(h * RET_DV, (h + 1) * RET_DV)
        q = rope(q_ref[:, qs].astype(F32)).astype(BF16)
        kf = rope(k_ref[:, qs].astype(F32)) * (RET_DK ** -0.5)
        v = v_ref[:, vs]
        scores = lax.dot_general(q, kf.astype(BF16), (((1,), (1,)), ((), ())),
                                 preferred_element_type=F32) * dec_ref[h]
        o = jnp.dot(scores.astype(BF16), v, preferred_element_type=F32)
        state = state_ref[h]
        cross = jnp.dot(q, state.astype(BF16), preferred_element_type=F32)
        cd = cross_ref[h]
        o = o + cross * jnp.concatenate([cd, cd], axis=1)
        o = o * lax.rsqrt(jnp.mean(o * o, axis=-1, keepdims=True) + NORM_EPS)
        g = g_ref[:, vs].astype(F32)
        o_ref[:, vs] = (o * (g * jax.nn.sigmoid(g))).astype(o_ref.dtype)
        ks = (kf * sin_state_ref[h]).astype(BF16)
        kv = lax.dot_general(ks, v, (((0,), (0,)), ((), ())), preferred_element_type=F32)
        state_ref[h] = block_decay[h] * state + kv


def _retention(p, cos, sin, dec, sin_state, cross, block_decay, batch, seq):
    blk = dec.shape[1]
    nb = seq // blk
    row = lambda b, n: (b * nb + n, 0)
    const3 = lambda b, n: (0, 0, 0)
    return pl.pallas_call(
        functools.partial(_retention_kernel, block_decay=block_decay),
        grid=(batch, nb),
        in_specs=[pl.BlockSpec((blk, RET_QK_W), lambda b, n: (b * nb + n, OFF_RQ // RET_QK_W)),
                  pl.BlockSpec((blk, RET_QK_W), lambda b, n: (b * nb + n, OFF_RK // RET_QK_W)),
                  pl.BlockSpec((blk, RET_V_W), lambda b, n: (b * nb + n, OFF_RV // RET_V_W)),
                  pl.BlockSpec((blk, RET_V_W), lambda b, n: (b * nb + n, OFF_RG // RET_V_W)),
                  pl.BlockSpec((blk, RET_DK), lambda b, n: (n, 0)),
                  pl.BlockSpec((blk, RET_DK), lambda b, n: (n, 0)),
                  pl.BlockSpec(dec.shape, const3),
                  pl.BlockSpec(sin_state.shape, const3),
                  pl.BlockSpec(cross.shape, const3)],
        out_specs=pl.BlockSpec((blk, RET_V_W), row),
        out_shape=jax.ShapeDtypeStruct((batch * seq, RET_V_W), BF16),
        scratch_shapes=[pltpu.VMEM((RET_HEADS, RET_DK, RET_DV), F32)],
        compiler_params=_params("parallel", "arbitrary"),
        name="retention",
    )(p, p, p, p, cos, sin, dec, sin_state, cross)


def _gated_merge_kernel(h_ref, a_ref, b_ref, c_ref, wg0_ref, wg1_ref, wg2_ref,
                        wa_ref, wb_ref, wc_ref, o_ref):
    h = h_ref[...]

    def branch(wg_ref, br_ref, w_ref):
        gate = jax.nn.sigmoid(_dot_nt(h, wg_ref[...]))
        return gate * jnp.dot(br_ref[...], w_ref[...], preferred_element_type=F32)

    m = branch(wg0_ref, a_ref, wa_ref)
    m = m + branch(wg1_ref, b_ref, wb_ref)
    m = m + branch(wg2_ref, c_ref, wc_ref)
    o_ref[...] = m.astype(o_ref.dtype)


def _gated_merge(h, a, bm, c, w_gates, wa, wb, wc, layer):
    t, d = h.shape
    tm = _tile(t, 1024, 8)
    tn = _tile(d, 512, LANES)
    nj = d // tn
    act = lambda arr: pl.BlockSpec((tm, arr.shape[1]), lambda i, j: (i, 0))
    wgt = lambda w: _layer_spec(w, layer, (w.shape[1], tn), lambda i, j: (0, j))
    gate = lambda n: _layer_spec(w_gates, layer, (tn, d), lambda i, j: (n * nj + j, 0))
    return pl.pallas_call(
        _gated_merge_kernel,
        grid=(t // tm, nj),
        in_specs=[act(h), act(a), act(bm), act(c), gate(0), gate(1), gate(2), wgt(wa), wgt(wb), wgt(wc)],
        out_specs=pl.BlockSpec((tm, tn), lambda i, j: (i, j)),
        out_shape=jax.ShapeDtypeStruct((t, d), BF16),
        compiler_params=_params("parallel", "arbitrary"),
        name="gated_merge",
    )(h, a, bm, c, w_gates, w_gates, w_gates, wa, wb, wc)


def _mm_res_kernel(a_ref, w_ref, r_ref, o_ref):
    o_ref[...] = r_ref[...] + jnp.dot(a_ref[...], w_ref[...], preferred_element_type=F32)


def _matmul_residual(a, w, r, layer, tm_pref, tn_pref, name):
    t, k = a.shape
    n = w.shape[2]
    tm = _tile(t, tm_pref, 8)
    tn = _tile(n, tn_pref, LANES)
    return pl.pallas_call(
        _mm_res_kernel,
        grid=(n // tn, t // tm),
        in_specs=[pl.BlockSpec((tm, k), lambda j, i: (i, 0)),
                  _layer_spec(w, layer, (k, tn), lambda j, i: (0, j)),
                  pl.BlockSpec((tm, tn), lambda j, i: (i, j))],
        out_specs=pl.BlockSpec((tm, tn), lambda j, i: (i, j)),
        out_shape=jax.ShapeDtypeStruct((t, n), F32),
        compiler_params=_params("parallel", "arbitrary"),
        name=name,
    )(a, w, r)


def _ffn_up_kernel(x_ref, xh_ref, g_ref, wu_ref, wg_ref, cw_ref, cb_ref, o_ref, h_ref, *, seq):
    tm = x_ref.shape[0]

    @pl.when(pl.program_id(1) == 0)
    def _():
        keep = ((pl.program_id(0) * tm) % seq != 0).astype(F32)
        h_ref[0:HALO, :] = (_rms(xh_ref[...], g_ref[...]) * keep).astype(BF16)
        _norm_rows(x_ref, g_ref, h_ref, tm, h_off=HALO)

    u = jnp.dot(h_ref[...], wu_ref[...], preferred_element_type=F32)
    gate = jnp.dot(h_ref[HALO:, :], wg_ref[...], preferred_element_type=F32)
    conv = (cb_ref[...] + cw_ref[2:3, :] * u
            + cw_ref[1:2, :] * pltpu.roll(u, 1, axis=0)
            + cw_ref[0:1, :] * pltpu.roll(u, 2, axis=0))
    o_ref[...] = (jax.nn.gelu(conv[HALO:, :]) * gate).astype(o_ref.dtype)


def _ffn_up(x, g, wu, wg, cw, cb, layer, seq):
    t, d = x.shape
    f = wu.shape[2]
    tm = _tile(seq, 1024, HALO)
    tf = _tile(f, 512, LANES)
    hb = tm // HALO
    col = lambda i, j: (0, j)
    return pl.pallas_call(
        functools.partial(_ffn_up_kernel, seq=seq),
        grid=(t // tm, f // tf),
        in_specs=[pl.BlockSpec((tm, d), lambda i, j: (i, 0)),
                  pl.BlockSpec((HALO, d), lambda i, j: (jnp.maximum(i * hb - 1, 0), 0)),
                  _layer_full(g, layer),
                  _layer_spec(wu, layer, (d, tf), col),
                  _layer_spec(wg, layer, (d, tf), col),
                  _layer_spec(cw, layer, (8, tf), col),
                  _layer_spec(cb, layer, (1, tf), col)],
        out_specs=pl.BlockSpec((tm, tf), lambda i, j: (i, j)),
        out_shape=jax.ShapeDtypeStruct((t, f), BF16),
        scratch_shapes=[pltpu.VMEM((HALO + tm, d), BF16)],
        compiler_params=_params("parallel", "arbitrary"),
        name="ffn_up_conv_gate",
    )(x, x, g, wu, wg, cw, cb)


def _final_norm_kernel(x_ref, g_ref, o_ref):
    o_ref[...] = _rms(x_ref[...], g_ref[...])


def _final_norm(x, g):
    t, d = x.shape
    tm = _tile(t, 512, 8)
    return pl.pallas_call(
        _final_norm_kernel,
        grid=(t // tm,),
        in_specs=[pl.BlockSpec((tm, d), lambda i: (i, 0)), pl.BlockSpec((1, d), lambda i: (0, 0))],
        out_specs=pl.BlockSpec((tm, d), lambda i: (i, 0)),
        out_shape=jax.ShapeDtypeStruct((t, d), F32),
        compiler_params=_params("parallel"),
        name="final_norm",
    )(x, g)


def _rope_tables(seq, d):
    pos = jnp.arange(seq, dtype=F32)
    inv_freq = ROPE_THETA ** (-jnp.arange(0, d, 2, dtype=F32) / d)
    ang = pos[:, None] * inv_freq[None, :]
    return jnp.cos(ang), jnp.sin(ang)


def _retention_tables(blk):
    log_gamma = jnp.log(1.0 - 2.0 ** (-5.0 - jnp.arange(RET_HEADS, dtype=F32)))
    idx = jnp.arange(blk, dtype=F32)
    chunk = jnp.arange(blk) // CHUNK
    dist = jnp.abs(idx[:, None] - idx[None, :])
    dec = jnp.where(chunk[None, :] <= chunk[:, None],
                    jnp.exp(log_gamma[:, None, None] * dist[None]), 0.0)
    sin_state = jnp.exp(log_gamma[:, None] * (blk - 1 - idx)[None, :])
    cross = jnp.exp(log_gamma[:, None] * (idx + 1.0)[None, :])
    bcast = lambda a: jnp.broadcast_to(a[:, :, None], (RET_HEADS, blk, RET_DK))
    block_decay = tuple(math.exp(math.log(1.0 - 2.0 ** (-5.0 - h)) * blk) for h in range(RET_HEADS))
    return dec, bcast(sin_state), bcast(cross), block_decay


def _rotate_half_cols(w):
    half = w.shape[-1] // 2
    return jnp.concatenate([-w[..., half:], w[..., :half]], axis=-1)


def kernel(x, norm1_g, w_in, mla_q_norm_g, mla_kv_norm_g, mla_w_uq, mla_w_ukv, fox_b_f,
           w_br_fox, w_br_mla, w_br_ret, w_out, norm2_g, ffn_w_up, ffn_w_gate,
           ffn_conv_w, ffn_conv_b, ffn_w_down, final_norm_g):
    batch, seq, d = x.shape
    depth = w_in.shape[0]
    t = batch * seq
    q_lora = mla_w_uq.shape[1]
    kv_lora = mla_w_ukv.shape[1]

    widths = (FOX_W, FOX_W, FOX_W, FOX_HEADS, q_lora, kv_lora, MLA_ROPE,
              RET_QK_W, RET_QK_W, RET_V_W, RET_V_W, N_BRANCH * d)
    offs = [0]
    for wd in widths:
        offs.append(offs[-1] + wd)
    run_t = lambda k0, k1: jnp.swapaxes(w_in[:, :, offs[k0]:offs[k1]], 1, 2).astype(BF16)
    zeros = lambda n: jnp.zeros((depth, n, d), BF16)
    small = run_t(3, 7)
    cut = [0, FOX_HEADS, FOX_HEADS + q_lora, FOX_HEADS + q_lora + kv_lora]
    ff_t, mq_t, mkv_t, rope_t = (small[:, a:b] for a, b in zip(cut, cut[1:] + [small.shape[1]]))
    w_p = jnp.concatenate(
        [run_t(7, 11), run_t(0, 3), mkv_t, mq_t,
         rope_t, ff_t, zeros(LANES - MLA_ROPE - FOX_HEADS),
         -rope_t[:, MLA_ROPE // 2:], rope_t[:, :MLA_ROPE // 2], zeros(LANES - MLA_ROPE)], axis=1)
    assert w_p.shape[1] == P_WIDTH and q_lora == 512 and kv_lora == 256
    w_gates = run_t(11, 12)
    b_ff = jnp.pad(fox_b_f, ((0, 0), (FF_LANE, LANES - FF_LANE - FOX_HEADS)))[:, None, :]

    uq = mla_w_uq.reshape(depth, q_lora, MLA_HEADS, MLA_NOPE + MLA_ROPE)
    uq_rope = uq[..., MLA_NOPE:]
    pad_q = jnp.zeros((depth, q_lora, MLA_HEADS, MLA_QK_PAD - MLA_NOPE - MLA_ROPE), uq.dtype)
    w_q = jnp.concatenate([uq, pad_q], axis=-1).reshape(depth, q_lora, -1).astype(BF16)
    pad_r = jnp.zeros((depth, q_lora, MLA_HEADS, LANES - MLA_ROPE), uq.dtype)
    w_qr = jnp.concatenate([_rotate_half_cols(uq_rope), pad_r], axis=-1).reshape(depth, q_lora, -1).astype(BF16)
    ukv = mla_w_ukv.reshape(depth, kv_lora, MLA_HEADS, MLA_NOPE + MLA_V)
    w_k = ukv[..., :MLA_NOPE].reshape(depth, kv_lora, -1).astype(BF16)
    w_v = ukv[..., MLA_NOPE:].reshape(depth, kv_lora, -1).astype(BF16)

    w_bf, w_bm, w_br = w_br_fox.astype(BF16), w_br_mla.astype(BF16), w_br_ret.astype(BF16)
    w_o = w_out.astype(BF16)
    w_up, w_gate, w_down = ffn_w_up.astype(BF16), ffn_w_gate.astype(BF16), ffn_w_down.astype(BF16)
    conv_w = jnp.pad(ffn_conv_w, ((0, 0), (0, 8 - CONV_W), (0, 0)))
    conv_b = ffn_conv_b[:, None, :]
    g1, g2 = norm1_g[:, None, :], norm2_g[:, None, :]
    gq, gkv = mla_q_norm_g[:, None, :], mla_kv_norm_g[:, None, :]

    cos64, sin64 = _rope_tables(seq, MLA_ROPE)
    zpad = jnp.zeros((seq, LANES - MLA_ROPE), F32)
    mla_cos = jnp.concatenate([cos64, cos64, zpad], axis=1)
    mla_sin = jnp.concatenate([sin64, sin64, zpad], axis=1)
    cos128, sin128 = _rope_tables(seq, RET_DK)
    ret_cos = jnp.concatenate([cos128, cos128], axis=1)
    ret_sin = jnp.concatenate([-sin128, sin128], axis=1)
    ret_blk = _tile(seq, 512, CHUNK)
    dec, sin_state, cross, block_decay = _retention_tables(ret_blk)

    att_blk = _tile(seq, 512, CHUNK)
    row = jnp.arange(att_blk, dtype=jnp.int32)[:, None]
    col = jnp.arange(att_blk, dtype=jnp.int32)[None, :]
    rel_frame = row - col
    rel_chunk = (row // CHUNK) * CHUNK + (CHUNK - 1) - col

    xf = x.reshape(t, d)
    for i in range(depth):
        p, f_logit, h1 = _in_proj(xf, g1, w_p, i)
        kbias = _fox_decay(f_logit, b_ff, i, batch, seq)
        a = _attention(p, OFF_FQ // FOX_W, p, OFF_FK // FOX_W, p, OFF_FV // FOX_W, rel_frame, kbias,
                       batch=batch, seq=seq, heads=FOX_HEADS, dq=FOX_DH, dv=FOX_DH,
                       scale=FOX_DH ** -0.5, name="fox_attention")
        qm, km, vm = _mla_prep(p, gq, gkv, w_q, w_qr, w_k, w_v, mla_cos, mla_sin, i, seq)
        bm = _attention(qm, 0, km, 0, vm, 0, rel_chunk, None,
                        batch=batch, seq=seq, heads=MLA_HEADS, dq=MLA_QK_PAD, dv=MLA_V,
                        scale=(MLA_NOPE + MLA_ROPE) ** -0.5, name="mla_attention")
        cr = _retention(p, ret_cos, ret_sin, dec, sin_state, cross, block_decay, batch, seq)
        merged = _gated_merge(h1, a, bm, cr, w_gates, w_bf, w_bm, w_br, i)
        xf = _matmul_residual(merged, w_o, xf, i, 1024, 1024, "out_proj")
        act = _ffn_up(xf, g2, w_up, w_gate, conv_w, conv_b, i, seq)
        xf = _matmul_residual(act, w_down, xf, i, 512, 1024, "ffn_down")
    return _final_norm(xf, final_norm_g[None, :]).reshape(batch, seq, d)
```

```python
import functools
import math

import jax
import jax.numpy as jnp
from jax import lax
from jax.experimental import pallas as pl
from jax.experimental.pallas import tpu as pltpu

F32 = jnp.float32
BF16 = jnp.bfloat16

NORM_EPS = 1e-6
ROPE_THETA = 10000.0
CHUNK = 64

FOX_HEADS, FOX_DH = 6, 128
MLA_HEADS, MLA_NOPE, MLA_ROPE, MLA_V = 6, 128, 64, 128
MLA_QK_PAD = 256
RET_HEADS, RET_DK, RET_DV = 4, 128, 256
N_BRANCH = 3
CONV_W = 3
LANES = 128
HALO = 16
BIAS_TERMS = 3
ATTN_HEAD_GROUPS = 1

FOX_W = FOX_HEADS * FOX_DH
MLA_W = MLA_HEADS * MLA_V
RET_QK_W = RET_HEADS * RET_DK
RET_V_W = RET_HEADS * RET_DV

OFF_RQ = 0
OFF_RK = OFF_RQ + RET_QK_W
OFF_RV = OFF_RK + RET_QK_W
OFF_RG = OFF_RV + RET_V_W
OFF_FQ = OFF_RG + RET_V_W
OFF_FK = OFF_FQ + FOX_W
OFF_FV = OFF_FK + FOX_W
OFF_MKV = OFF_FV + FOX_W
OFF_MQ = OFF_MKV + 256
OFF_MKR = OFF_MQ + 512
FF_LANE = MLA_ROPE
OFF_MKRR = OFF_MKR + LANES
P_WIDTH = OFF_MKRR + LANES

VMEM_LIMIT_BYTES = 56 * 1024 * 1024


def _tile(n, pref, align):
    t = (min(pref, n) // align) * align
    while t > align and n % t:
        t -= align
    assert t >= align and n % t == 0, (n, pref, align)
    return t


def _params(*sem):
    return pltpu.CompilerParams(dimension_semantics=sem, vmem_limit_bytes=VMEM_LIMIT_BYTES)


def _layer_spec(arr, layer, block, index):
    return pl.BlockSpec((None,) + tuple(block), lambda *ids: (layer,) + tuple(index(*ids)))


def _layer_full(arr, layer):
    zeros = (0,) * (arr.ndim - 1)
    return pl.BlockSpec((None,) + arr.shape[1:], lambda *ids: (layer,) + zeros)


def _rms(x, g):
    return x * lax.rsqrt(jnp.mean(x * x, axis=-1, keepdims=True) + NORM_EPS) * g


def _norm_rows(x_ref, g_ref, h_ref, rows, h_off=0):
    rc = _tile(rows, 256, 8)

    def body(c, carry):
        r0 = pl.multiple_of(c * rc, rc)
        h_ref[pl.ds(h_off + r0, rc), :] = _rms(x_ref[pl.ds(r0, rc), :], g_ref[...]).astype(BF16)
        return carry

    lax.fori_loop(0, rows // rc, body, 0)


def _dot_nt(a, b):
    return lax.dot_general(a, b, (((1,), (1,)), ((), ())), preferred_element_type=F32)


def _in_proj_kernel(x_ref, g_ref, w_ref, o_ref, f_ref, h_ref, *, f_tile, f_col):
    @pl.when(pl.program_id(1) == 0)
    def _():
        _norm_rows(x_ref, g_ref, h_ref, x_ref.shape[0])

    y = _dot_nt(h_ref[...], w_ref[...])
    o_ref[...] = y.astype(o_ref.dtype)

    @pl.when(pl.program_id(1) == f_tile)
    def _():
        f_ref[...] = y[:, f_col:f_col + LANES]


def _in_proj(x, g, w, layer):
    t, d = x.shape
    n = w.shape[1]
    tm = _tile(t, 1024, 8)
    tn = _tile(n, 1280, LANES)
    return pl.pallas_call(
        functools.partial(_in_proj_kernel, f_tile=OFF_MKR // tn, f_col=OFF_MKR % tn),
        grid=(t // tm, n // tn),
        in_specs=[pl.BlockSpec((tm, d), lambda i, j: (i, 0)),
                  _layer_full(g, layer),
                  _layer_spec(w, layer, (tn, d), lambda i, j: (j, 0))],
        out_specs=[pl.BlockSpec((tm, tn), lambda i, j: (i, j)),
                   pl.BlockSpec((tm, LANES), lambda i, j: (i, 0)),
                   pl.BlockSpec((tm, d), lambda i, j: (i, 0))],
        out_shape=[jax.ShapeDtypeStruct((t, n), BF16), jax.ShapeDtypeStruct((t, LANES), F32),
                   jax.ShapeDtypeStruct((t, d), BF16)],
        compiler_params=_params("parallel", "arbitrary"),
        name="in_proj",
    )(x, g, w)


def _fox_decay_kernel(f_ref, b_ref, tri_ref, o_ref, carry_ref):
    @pl.when(pl.program_id(1) == 0)
    def _():
        carry_ref[...] = jnp.zeros_like(carry_ref)

    f = f_ref[...] + b_ref[...]
    lf = jnp.minimum(f, 0.0) - jnp.log1p(jnp.exp(-jnp.abs(f)))
    lane = lax.broadcasted_iota(jnp.int32, lf.shape, 1)
    hi = lf.astype(BF16).astype(F32)
    r1 = lf - hi
    mid = r1.astype(BF16).astype(F32)
    lo = r1 - mid
    packed = jnp.where(lane < FF_LANE + FOX_HEADS, hi,
                       jnp.where(lane < FF_LANE + 2 * FOX_HEADS, pltpu.roll(mid, FOX_HEADS, axis=1),
                                 pltpu.roll(lo, 2 * FOX_HEADS, axis=1)))
    cp = jnp.dot(tri_ref[...], packed.astype(BF16), preferred_element_type=F32)
    c = (cp + pltpu.roll(cp, LANES - FOX_HEADS, axis=1)
         + pltpu.roll(cp, LANES - 2 * FOX_HEADS, axis=1)) + carry_ref[0:1, :]
    carry_ref[...] = jnp.broadcast_to(c[-1:, :], carry_ref.shape)
    for h in range(FOX_HEADS):
        src = FF_LANE + h
        rest = jnp.broadcast_to(c[:, src:src + 1], lane.shape) * (-(FOX_DH ** 0.5))
        slab = jnp.zeros(lane.shape, F32)
        for term in range(BIAS_TERMS):
            part = rest.astype(BF16).astype(F32)
            slab = jnp.where(lane == term, part, slab)
            rest = rest - part
        o_ref[:, h * LANES:(h + 1) * LANES] = slab.astype(BF16)


def _fox_decay(f, b_ff, layer, batch, seq):
    t = f.shape[0]
    ts = _tile(seq, 1024, 8)
    nb = seq // ts
    tri = (jnp.arange(ts)[:, None] >= jnp.arange(ts)[None, :]).astype(BF16)
    return pl.pallas_call(
        _fox_decay_kernel,
        grid=(batch, nb),
        in_specs=[pl.BlockSpec((ts, LANES), lambda b, n: (b * nb + n, 0)),
                  _layer_full(b_ff, layer),
                  pl.BlockSpec((ts, ts), lambda b, n: (0, 0))],
        out_specs=pl.BlockSpec((ts, FOX_HEADS * LANES), lambda b, n: (b * nb + n, 0)),
        out_shape=jax.ShapeDtypeStruct((t, FOX_HEADS * LANES), BF16),
        scratch_shapes=[pltpu.VMEM((8, LANES), F32)],
        compiler_params=_params("parallel", "arbitrary"),
        name="fox_decay_cumsum",
    )(f, b_ff, tri)


def _mla_prep_kernel(mq_ref, mkv_ref, kr_ref, krr_ref, gq_ref, gkv_ref, wq_ref, wqr_ref,
                     wk_ref, wv_ref, cos_ref, sin_ref, q_ref, k_ref, v_ref):
    cos = cos_ref[...]
    sin = sin_ref[...]
    cqn = _rms(mq_ref[...].astype(F32), gq_ref[...]).astype(BF16)
    q = jnp.dot(cqn, wq_ref[...], preferred_element_type=F32)
    qrot = jnp.dot(cqn, wqr_ref[...], preferred_element_type=F32)
    ckvn = _rms(mkv_ref[...].astype(F32), gkv_ref[...]).astype(BF16)
    kn = jnp.dot(ckvn, wk_ref[...], preferred_element_type=F32)
    v_ref[...] = jnp.dot(ckvn, wv_ref[...], preferred_element_type=F32).astype(BF16)
    kr = (kr_ref[...].astype(F32) * cos + krr_ref[...].astype(F32) * sin).astype(BF16)
    for h in range(MLA_HEADS):
        lo = h * MLA_QK_PAD
        q_ref[:, lo:lo + LANES] = q[:, lo:lo + LANES].astype(BF16)
        q_ref[:, lo + LANES:lo + 2 * LANES] = (
            q[:, lo + LANES:lo + 2 * LANES] * cos + qrot[:, h * LANES:(h + 1) * LANES] * sin).astype(BF16)
        k_ref[:, lo:lo + LANES] = kn[:, h * LANES:(h + 1) * LANES].astype(BF16)
        k_ref[:, lo + LANES:lo + 2 * LANES] = kr


def _mla_prep(p, gq, gkv, wq, wqr, wk, wv, cos, sin, layer, seq):
    t = p.shape[0]
    tm = _tile(seq, 512, 8)
    ns = seq // tm
    ql, kvl = wq.shape[1], wk.shape[1]
    row = lambda i: (i, 0)
    return pl.pallas_call(
        _mla_prep_kernel,
        grid=(t // tm,),
        in_specs=[pl.BlockSpec((tm, ql), lambda i: (i, OFF_MQ // ql)),
                  pl.BlockSpec((tm, kvl), lambda i: (i, OFF_MKV // kvl)),
                  pl.BlockSpec((tm, LANES), lambda i: (i, OFF_MKR // LANES)),
                  pl.BlockSpec((tm, LANES), lambda i: (i, OFF_MKRR // LANES)),
                  _layer_full(gq, layer), _layer_full(gkv, layer),
                  _layer_full(wq, layer), _layer_full(wqr, layer),
                  _layer_full(wk, layer), _layer_full(wv, layer),
                  pl.BlockSpec((tm, LANES), lambda i: (i % ns, 0)),
                  pl.BlockSpec((tm, LANES), lambda i: (i % ns, 0))],
        out_specs=[pl.BlockSpec((tm, MLA_HEADS * MLA_QK_PAD), row),
                   pl.BlockSpec((tm, MLA_HEADS * MLA_QK_PAD), row),
                   pl.BlockSpec((tm, MLA_W), row)],
        out_shape=[jax.ShapeDtypeStruct((t, MLA_HEADS * MLA_QK_PAD), BF16),
                   jax.ShapeDtypeStruct((t, MLA_HEADS * MLA_QK_PAD), BF16),
                   jax.ShapeDtypeStruct((t, MLA_W), BF16)],
        compiler_params=_params("parallel"),
        name="mla_prep",
    )(p, p, p, p, gq, gkv, wq, wqr, wk, wv, cos, sin)


def _attn_kernel(*refs, heads, dq, dv, coef, has_bias, tq):
    if has_bias:
        q_ref, k_ref, v_ref, rel_ref, kb_ref, o_ref, m_ref, l_ref, acc_ref = refs
    else:
        q_ref, k_ref, v_ref, rel_ref, o_ref, m_ref, l_ref, acc_ref = refs
    qi = pl.program_id(2)

    def accumulate(k0, visible, first):
        rows = pl.ds(k0, tq)
        ones_v = jnp.ones((tq, LANES), BF16)
        if has_bias:
            lane = lax.broadcasted_iota(jnp.int32, (tq, LANES), 1)
            ones_q = jnp.where(lane < BIAS_TERMS, 1.0, 0.0).astype(BF16)
        for h in range(heads):
            q = q_ref[:, h * dq:(h + 1) * dq]
            k = k_ref[rows, h * dq:(h + 1) * dq]
            if has_bias:
                q = jnp.concatenate([q, ones_q], axis=1)
                k = jnp.concatenate([k, kb_ref[rows, h * LANES:(h + 1) * LANES]], axis=1)
            s = lax.dot_general(q, k, (((1,), (1,)), ((), ())), preferred_element_type=F32)
            if visible is not None:
                s = jnp.where(visible, s, -jnp.inf)
            m_new = jnp.broadcast_to(jnp.max(s, axis=1, keepdims=True), (tq, LANES))
            if not first:
                m_prev = m_ref[h]
                m_new = jnp.maximum(m_prev, m_new)
                alpha = jnp.exp2((m_prev - m_new) * coef)
            p = jnp.concatenate(
                [jnp.exp2((s[:, j * LANES:(j + 1) * LANES] - m_new) * coef) for j in range(tq // LANES)],
                axis=1).astype(BF16)
            v1 = jnp.concatenate([v_ref[rows, h * dv:(h + 1) * dv], ones_v], axis=1)
            pv = jnp.dot(p, v1, preferred_element_type=F32)
            if first:
                acc_ref[:, h * dv:(h + 1) * dv] = pv[:, :dv]
                l_ref[h] = pv[:, dv:]
            else:
                acc_ref[:, h * dv:(h + 1) * dv] = alpha * acc_ref[:, h * dv:(h + 1) * dv] + pv[:, :dv]
                l_ref[h] = alpha * l_ref[h] + pv[:, dv:]
            m_ref[h] = m_new

    accumulate(pl.multiple_of(qi * tq, tq), rel_ref[...] >= 0, True)

    def below_diagonal_pair(pi, carry):
        k0 = pl.multiple_of(pi * (2 * tq), 2 * tq)
        accumulate(k0, None, False)
        accumulate(pl.multiple_of(k0 + tq, tq), None, False)
        return carry

    lax.fori_loop(0, qi // 2, below_diagonal_pair, 0)

    @pl.when(qi % 2 == 1)
    def _():
        accumulate(pl.multiple_of((qi - 1) * tq, tq), None, False)

    for h in range(heads):
        o_ref[:, h * dv:(h + 1) * dv] = (acc_ref[:, h * dv:(h + 1) * dv] / l_ref[h]).astype(o_ref.dtype)


def _attention(q_arr, q_cb, k_arr, k_cb, v_arr, v_cb, rel, kbias, *, batch, seq, heads, dq, dv, scale, name):
    assert dv == LANES and heads % ATTN_HEAD_GROUPS == 0
    tq = rel.shape[0]
    nq = seq // tq
    ng = ATTN_HEAD_GROUPS
    hg = heads // ng
    has_bias = kbias is not None
    in_specs = [pl.BlockSpec((tq, hg * dq), lambda b, g, qi: (b * nq + qi, q_cb * ng + g)),
                pl.BlockSpec((seq, hg * dq), lambda b, g, qi: (b, k_cb * ng + g)),
                pl.BlockSpec((seq, hg * dv), lambda b, g, qi: (b, v_cb * ng + g)),
                pl.BlockSpec((tq, tq), lambda b, g, qi: (0, 0))]
    args = [q_arr, k_arr, v_arr, rel]
    if has_bias:
        in_specs.append(pl.BlockSpec((seq, hg * LANES), lambda b, g, qi: (b, g)))
        args.append(kbias)
    return pl.pallas_call(
        functools.partial(_attn_kernel, heads=hg, dq=dq, dv=dv, coef=scale * math.log2(math.e),
                          has_bias=has_bias, tq=tq),
        grid=(batch, ng, nq),
        in_specs=in_specs,
        out_specs=pl.BlockSpec((tq, hg * dv), lambda b, g, qi: (b * nq + qi, g)),
        out_shape=jax.ShapeDtypeStruct((batch * seq, heads * dv), BF16),
        scratch_shapes=[pltpu.VMEM((hg, tq, LANES), F32),
                        pltpu.VMEM((hg, tq, LANES), F32),
                        pltpu.VMEM((tq, hg * dv), F32)],
        compiler_params=_params("parallel", "parallel", "arbitrary"),
        name=name,
    )(*args)


def _retention_kernel(q_ref, k_ref, v_ref, g_ref, cos_ref, sin_ref, dec_ref, sin_state_ref,
                      cross_ref, o_ref, state_ref, *, block_decay):
    @pl.when(pl.program_id(1) == 0)
    def _():
        state_ref[...] = jnp.zeros_like(state_ref)

    cos = cos_ref[...]
    sin = sin_ref[...]

    def rope(x):
        return x * cos + pltpu.roll(x, RET_DK // 2, axis=1) * sin

    for h in range(RET_HEADS):
        qs = slice(h * RET_DK, (h + 1) * RET_DK)
        vs = slice(h * RET_DV, (h + 1) * RET_DV)
        q = rope(q_ref[:, qs].astype(F32)).astype(BF16)
        kf = rope(k_ref[:, qs].astype(F32)) * (RET_DK ** -0.5)
        v = v_ref[:, vs]
        scores = lax.dot_general(q, kf.astype(BF16), (((1,), (1,)), ((), ())),
                                 preferred_element_type=F32) * dec_ref[h]
        o = jnp.dot(scores.astype(BF16), v, preferred_element_type=F32)
        state = state_ref[h]
        cross = jnp.dot(q, state.astype(BF16), preferred_element_type=F32)
        cd = cross_ref[h]
        o = o + cross * jnp.concatenate([cd, cd], axis=1)
        o = o * lax.rsqrt(jnp.mean(o * o, axis=-1, keepdims=True) + NORM_EPS)
        g = g_ref[:, vs].astype(F32)
        o_ref[:, vs] = (o * (g * jax.nn.sigmoid(g))).astype(o_ref.dtype)
        ks = (kf * sin_state_ref[h]).astype(BF16)
        kv = lax.dot_general(ks, v, (((0,), (0,)), ((), ())), preferred_element_type=F32)
        state_ref[h] = block_decay[h] * state + kv


def _retention(p, cos, sin, dec, sin_state, cross, block_decay, batch, seq):
    blk = dec.shape[1]
    nb = seq // blk
    row = lambda b, n: (b * nb + n, 0)
    const3 = lambda b, n: (0, 0, 0)
    return pl.pallas_call(
        functools.partial(_retention_kernel, block_decay=block_decay),
        grid=(batch, nb),
        in_specs=[pl.BlockSpec((blk, RET_QK_W), lambda b, n: (b * nb + n, OFF_RQ // RET_QK_W)),
                  pl.BlockSpec((blk, RET_QK_W), lambda b, n: (b * nb + n, OFF_RK // RET_QK_W)),
                  pl.BlockSpec((blk, RET_V_W), lambda b, n: (b * nb + n, OFF_RV // RET_V_W)),
                  pl.BlockSpec((blk, RET_V_W), lambda b, n: (b * nb + n, OFF_RG // RET_V_W)),
                  pl.BlockSpec((blk, RET_DK), lambda b, n: (n, 0)),
                  pl.BlockSpec((blk, RET_DK), lambda b, n: (n, 0)),
                  pl.BlockSpec(dec.shape, const3),
                  pl.BlockSpec(sin_state.shape, const3),
                  pl.BlockSpec(cross.shape, const3)],
        out_specs=pl.BlockSpec((blk, RET_V_W), row),
        out_shape=jax.ShapeDtypeStruct((batch * seq, RET_V_W), BF16),
        scratch_shapes=[pltpu.VMEM((RET_HEADS, RET_DK, RET_DV), F32)],
        compiler_params=_params("parallel", "arbitrary"),
        name="retention",
    )(p, p, p, p, cos, sin, dec, sin_state, cross)


def _gated_merge_kernel(h_ref, a_ref, b_ref, c_ref, wg0_ref, wg1_ref, wg2_ref,
                        wa_ref, wb_ref, wc_ref, o_ref):
    h = h_ref[...]

    def branch(wg_ref, br_ref, w_ref):
        gate = jax.nn.sigmoid(_dot_nt(h, wg_ref[...]))
        return gate * jnp.dot(br_ref[...], w_ref[...], preferred_element_type=F32)

    m = branch(wg0_ref, a_ref, wa_ref)
    m = m + branch(wg1_ref, b_ref, wb_ref)
    m = m + branch(wg2_ref, c_ref, wc_ref)
    o_ref[...] = m.astype(o_ref.dtype)


def _gated_merge(h, a, bm, c, w_gates, wa, wb, wc, layer):
    t, d = h.shape
    tm = _tile(t, 1024, 8)
    tn = _tile(d, 512, LANES)
    nj = d // tn
    act = lambda arr: pl.BlockSpec((tm, arr.shape[1]), lambda i, j: (i, 0))
    wgt = lambda w: _layer_spec(w, layer, (w.shape[1], tn), lambda i, j: (0, j))
    gate = lambda n: _layer_spec(w_gates, layer, (tn, d), lambda i, j: (n * nj + j, 0))
    return pl.pallas_call(
        _gated_merge_kernel,
        grid=(t // tm, nj),
        in_specs=[act(h), act(a), act(bm), act(c), gate(0), gate(1), gate(2), wgt(wa), wgt(wb), wgt(wc)],
        out_specs=pl.BlockSpec((tm, tn), lambda i, j: (i, j)),
        out_shape=jax.ShapeDtypeStruct((t, d), BF16),
        compiler_params=_params("parallel", "arbitrary"),
        name="gated_merge",
    )(h, a, bm, c, w_gates, w_gates, w_gates, wa, wb, wc)


def _mm_res_kernel(a_ref, w_ref, r_ref, o_ref):
    o_ref[...] = r_ref[...] + jnp.dot(a_ref[...], w_ref[...], preferred_element_type=F32)


def _matmul_residual(a, w, r, layer, tm_pref, tn_pref, name):
    t, k = a.shape
    n = w.shape[2]
    tm = _tile(t, tm_pref, 8)
    tn = _tile(n, tn_pref, LANES)
    return pl.pallas_call(
        _mm_res_kernel,
        grid=(n // tn, t // tm),
        in_specs=[pl.BlockSpec((tm, k), lambda j, i: (i, 0)),
                  _layer_spec(w, layer, (k, tn), lambda j, i: (0, j)),
                  pl.BlockSpec((tm, tn), lambda j, i: (i, j))],
        out_specs=pl.BlockSpec((tm, tn), lambda j, i: (i, j)),
        out_shape=jax.ShapeDtypeStruct((t, n), F32),
        compiler_params=_params("parallel", "arbitrary"),
        name=name,
    )(a, w, r)


def _ffn_up_kernel(x_ref, xh_ref, g_ref, wu_ref, wg_ref, cw_ref, cb_ref, o_ref, h_ref, *, seq):
    tm = x_ref.shape[0]

    @pl.when(pl.program_id(1) == 0)
    def _():
        keep = ((pl.program_id(0) * tm) % seq != 0).astype(F32)
        h_ref[0:HALO, :] = (_rms(xh_ref[...], g_ref[...]) * keep).astype(BF16)
        _norm_rows(x_ref, g_ref, h_ref, tm, h_off=HALO)

    u = jnp.dot(h_ref[...], wu_ref[...], preferred_element_type=F32)
    gate = jnp.dot(h_ref[HALO:, :], wg_ref[...], preferred_element_type=F32)
    conv = (cb_ref[...] + cw_ref[2:3, :] * u
            + cw_ref[1:2, :] * pltpu.roll(u, 1, axis=0)
            + cw_ref[0:1, :] * pltpu.roll(u, 2, axis=0))
    o_ref[...] = (jax.nn.gelu(conv[HALO:, :]) * gate).astype(o_ref.dtype)


def _ffn_up(x, g, wu, wg, cw, cb, layer, seq):
    t, d = x.shape
    f = wu.shape[2]
    tm = _tile(seq, 1024, HALO)
    tf = _tile(f, 512, LANES)
    hb = tm // HALO
    col = lambda i, j: (0, j)
    return pl.pallas_call(
        functools.partial(_ffn_up_kernel, seq=seq),
        grid=(t // tm, f // tf),
        in_specs=[pl.BlockSpec((tm, d), lambda i, j: (i, 0)),
                  pl.BlockSpec((HALO, d), lambda i, j: (jnp.maximum(i * hb - 1, 0), 0)),
                  _layer_full(g, layer),
                  _layer_spec(wu, layer, (d, tf), col),
                  _layer_spec(wg, layer, (d, tf), col),
                  _layer_spec(cw, layer, (8, tf), col),
                  _layer_spec(cb, layer, (1, tf), col)],
        out_specs=pl.BlockSpec((tm, tf), lambda i, j: (i, j)),
        out_shape=jax.ShapeDtypeStruct((t, f), BF16),
        scratch_shapes=[pltpu.VMEM((HALO + tm, d), BF16)],
        compiler_params=_params("parallel", "arbitrary"),
        name="ffn_up_conv_gate",
    )(x, x, g, wu, wg, cw, cb)


def _final_norm_kernel(x_ref, g_ref, o_ref):
    o_ref[...] = _rms(x_ref[...], g_ref[...])


def _final_norm(x, g):
    t, d = x.shape
    tm = _tile(t, 512, 8)
    return pl.pallas_call(
        _final_norm_kernel,
        grid=(t // tm,),
        in_specs=[pl.BlockSpec((tm, d), lambda i: (i, 0)), pl.BlockSpec((1, d), lambda i: (0, 0))],
        out_specs=pl.BlockSpec((tm, d), lambda i: (i, 0)),
        out_shape=jax.ShapeDtypeStruct((t, d), F32),
        compiler_params=_params("parallel"),
        name="final_norm",
    )(x, g)


def _rope_tables(seq, d):
    pos = jnp.arange(seq, dtype=F32)
    inv_freq = ROPE_THETA ** (-jnp.arange(0, d, 2, dtype=F32) / d)
    ang = pos[:, None] * inv_freq[None, :]
    return jnp.cos(ang), jnp.sin(ang)


def _retention_tables(blk):
    log_gamma = jnp.log(1.0 - 2.0 ** (-5.0 - jnp.arange(RET_HEADS, dtype=F32)))
    idx = jnp.arange(blk, dtype=F32)
    chunk = jnp.arange(blk) // CHUNK
    dist = jnp.abs(idx[:, None] - idx[None, :])
    dec = jnp.where(chunk[None, :] <= chunk[:, None],
                    jnp.exp(log_gamma[:, None, None] * dist[None]), 0.0)
    sin_state = jnp.exp(log_gamma[:, None] * (blk - 1 - idx)[None, :])
    cross = jnp.exp(log_gamma[:, None] * (idx + 1.0)[None, :])
    bcast = lambda a: jnp.broadcast_to(a[:, :, None], (RET_HEADS, blk, RET_DK))
    block_decay = tuple(math.exp(math.log(1.0 - 2.0 ** (-5.0 - h)) * blk) for h in range(RET_HEADS))
    return dec, bcast(sin_state), bcast(cross), block_decay


def _rotate_half_cols(w):
    half = w.shape[-1] // 2
    return jnp.concatenate([-w[..., half:], w[..., :half]], axis=-1)


def kernel(x, norm1_g, w_in, mla_q_norm_g, mla_kv_norm_g, mla_w_uq, mla_w_ukv, fox_b_f,
           w_br_fox, w_br_mla, w_br_ret, w_out, norm2_g, ffn_w_up, ffn_w_gate,
           ffn_conv_w, ffn_conv_b, ffn_w_down, final_norm_g):
    batch, seq, d = x.shape
    depth = w_in.shape[0]
    t = batch * seq
    q_lora = mla_w_uq.shape[1]
    kv_lora = mla_w_ukv.shape[1]

    widths = (FOX_W, FOX_W, FOX_W, FOX_HEADS, q_lora, kv_lora, MLA_ROPE,
              RET_QK_W, RET_QK_W, RET_V_W, RET_V_W, N_BRANCH * d)
    offs = [0]
    for wd in widths:
        offs.append(offs[-1] + wd)
    run_t = lambda k0, k1: jnp.swapaxes(w_in[:, :, offs[k0]:offs[k1]], 1, 2).astype(BF16)
    zeros = lambda n: jnp.zeros((depth, n, d), BF16)
    small = run_t(3, 7)
    cut = [0, FOX_HEADS, FOX_HEADS + q_lora, FOX_HEADS + q_lora + kv_lora]
    ff_t, mq_t, mkv_t, rope_t = (small[:, a:b] for a, b in zip(cut, cut[1:] + [small.shape[1]]))
    mla_rows = jnp.concatenate(
        [mkv_t, mq_t, rope_t, ff_t, zeros(LANES - MLA_ROPE - FOX_HEADS),
         -rope_t[:, MLA_ROPE // 2:], rope_t[:, :MLA_ROPE // 2], zeros(LANES - MLA_ROPE)], axis=1)
    assert OFF_MKV + mla_rows.shape[1] == P_WIDTH and q_lora == 512 and kv_lora == 256
    w_p = jnp.zeros((depth, P_WIDTH, d), BF16)
    w_p = lax.dynamic_update_slice(w_p, run_t(7, 11), (0, OFF_RQ, 0))
    w_p = lax.dynamic_update_slice(w_p, run_t(0, 3), (0, OFF_FQ, 0))
    w_p = lax.dynamic_update_slice(w_p, mla_rows, (0, OFF_MKV, 0))
    w_gates = run_t(11, 12)
    b_ff = jnp.pad(fox_b_f, ((0, 0), (FF_LANE, LANES - FF_LANE - FOX_HEADS)))[:, None, :]

    uq = mla_w_uq.reshape(depth, q_lora, MLA_HEADS, MLA_NOPE + MLA_ROPE)
    uq_rope = uq[..., MLA_NOPE:]
    pad_q = jnp.zeros((depth, q_lora, MLA_HEADS, MLA_QK_PAD - MLA_NOPE - MLA_ROPE), uq.dtype)
    w_q = jnp.concatenate([uq, pad_q], axis=-1).reshape(depth, q_lora, -1).astype(BF16)
    pad_r = jnp.zeros((depth, q_lora, MLA_HEADS, LANES - MLA_ROPE), uq.dtype)
    w_qr = jnp.concatenate([_rotate_half_cols(uq_rope), pad_r], axis=-1).reshape(depth, q_lora, -1).astype(BF16)
    ukv = mla_w_ukv.reshape(depth, kv_lora, MLA_HEADS, MLA_NOPE + MLA_V)
    w_k = ukv[..., :MLA_NOPE].reshape(depth, kv_lora, -1).astype(BF16)
    w_v = ukv[..., MLA_NOPE:].reshape(depth, kv_lora, -1).astype(BF16)

    w_bf, w_bm, w_br = w_br_fox.astype(BF16), w_br_mla.astype(BF16), w_br_ret.astype(BF16)
    w_o = w_out.astype(BF16)
    w_up, w_gate, w_down = ffn_w_up.astype(BF16), ffn_w_gate.astype(BF16), ffn_w_down.astype(BF16)
    conv_w = jnp.pad(ffn_conv_w, ((0, 0), (0, 8 - CONV_W), (0, 0)))
    conv_b = ffn_conv_b[:, None, :]
    g1, g2 = norm1_g[:, None, :], norm2_g[:, None, :]
    gq, gkv = mla_q_norm_g[:, None, :], mla_kv_norm_g[:, None, :]

    cos64, sin64 = _rope_tables(seq, MLA_ROPE)
    zpad = jnp.zeros((seq, LANES - MLA_ROPE), F32)
    mla_cos = jnp.concatenate([cos64, cos64, zpad], axis=1)
    mla_sin = jnp.concatenate([sin64, sin64, zpad], axis=1)
    cos128, sin128 = _rope_tables(seq, RET_DK)
    ret_cos = jnp.concatenate([cos128, cos128], axis=1)
    ret_sin = jnp.concatenate([-sin128, sin128], axis=1)
    ret_blk = _tile(seq, 512, CHUNK)
    dec, sin_state, cross, block_decay = _retention_tables(ret_blk)

    att_blk = _tile(seq, 512, CHUNK)
    row = jnp.arange(att_blk, dtype=jnp.int32)[:, None]
    col = jnp.arange(att_blk, dtype=jnp.int32)[None, :]
    rel_frame = row - col
    rel_chunk = (row // CHUNK) * CHUNK + (CHUNK - 1) - col

    xf = x.reshape(t, d)
    for i in range(depth):
        p, f_logit, h1 = _in_proj(xf, g1, w_p, i)
        kbias = _fox_decay(f_logit, b_ff, i, batch, seq)
        a = _attention(p, OFF_FQ // FOX_W, p, OFF_FK // FOX_W, p, OFF_FV // FOX_W, rel_frame, kbias,
                       batch=batch, seq=seq, heads=FOX_HEADS, dq=FOX_DH, dv=FOX_DH,
                       scale=FOX_DH ** -0.5, name="fox_attention")
        qm, km, vm = _mla_prep(p, gq, gkv, w_q, w_qr, w_k, w_v, mla_cos, mla_sin, i, seq)
        bm = _attention(qm, 0, km, 0, vm, 0, rel_chunk, None,
                        batch=batch, seq=seq, heads=MLA_HEADS, dq=MLA_QK_PAD, dv=MLA_V,
                        scale=(MLA_NOPE + MLA_ROPE) ** -0.5, name="mla_attention")
        cr = _retention(p, ret_cos, ret_sin, dec, sin_state, cross, block_decay, batch, seq)
        merged = _gated_merge(h1, a, bm, cr, w_gates, w_bf, w_bm, w_br, i)
        xf = _matmul_residual(merged, w_o, xf, i, 1024, 1024, "out_proj")
        act = _ffn_up(xf, g2, w_up, w_gate, conv_w, conv_b, i, seq)
        xf = _matmul_residual(act, w_down, xf, i, 512, 1024, "ffn_down")
    return _final_norm(xf, final_norm_g[None, :]).reshape(batch, seq, d)
```

```python
import functools
import math

import jax
import jax.numpy as jnp
from jax import lax
from jax.experimental import pallas as pl
from jax.experimental.pallas import tpu as pltpu

F32 = jnp.float32
BF16 = jnp.bfloat16

NORM_EPS = 1e-6
ROPE_THETA = 10000.0
CHUNK = 64

FOX_HEADS, FOX_DH = 6, 128
MLA_HEADS, MLA_NOPE, MLA_ROPE, MLA_V = 6, 128, 64, 128
MLA_QK_PAD = 256
RET_HEADS, RET_DK, RET_DV = 4, 128, 256
N_BRANCH = 3
CONV_W = 3
LANES = 128
FFN_TOKEN_TILE = 1024
BIAS_TERMS = 3
ATTN_HEAD_GROUPS = 1

FOX_W = FOX_HEADS * FOX_DH
MLA_W = MLA_HEADS * MLA_V
RET_QK_W = RET_HEADS * RET_DK
RET_V_W = RET_HEADS * RET_DV

OFF_RQ = 0
OFF_RK = OFF_RQ + RET_QK_W
OFF_RV = OFF_RK + RET_QK_W
OFF_RG = OFF_RV + RET_V_W
OFF_FQ = OFF_RG + RET_V_W
OFF_FK = OFF_FQ + FOX_W
OFF_FV = OFF_FK + FOX_W
OFF_MKV = OFF_FV + FOX_W
OFF_MQ = OFF_MKV + 256
OFF_MKR = OFF_MQ + 512
FF_LANE = MLA_ROPE
OFF_MKRR = OFF_MKR + LANES
P_WIDTH = OFF_MKRR + LANES

VMEM_LIMIT_BYTES = 56 * 1024 * 1024


def _tile(n, pref, align):
    t = (min(pref, n) // align) * align
    while t > align and n % t:
        t -= align
    assert t >= align and n % t == 0, (n, pref, align)
    return t


def _params(*sem):
    return pltpu.CompilerParams(dimension_semantics=sem, vmem_limit_bytes=VMEM_LIMIT_BYTES)


def _layer_spec(arr, layer, block, index):
    return pl.BlockSpec((None,) + tuple(block), lambda *ids: (layer,) + tuple(index(*ids)))


def _layer_full(arr, layer):
    zeros = (0,) * (arr.ndim - 1)
    return pl.BlockSpec((None,) + arr.shape[1:], lambda *ids: (layer,) + zeros)


def _rms(x, g):
    return x * lax.rsqrt(jnp.mean(x * x, axis=-1, keepdims=True) + NORM_EPS) * g


def _norm_rows(x_ref, g_ref, h_ref, rows, h_off=0):
    rc = _tile(rows, 256, 8)

    def body(c, carry):
        r0 = pl.multiple_of(c * rc, rc)
        h_ref[pl.ds(h_off + r0, rc), :] = _rms(x_ref[pl.ds(r0, rc), :], g_ref[...]).astype(BF16)
        return carry

    lax.fori_loop(0, rows // rc, body, 0)


def _dot_nt(a, b):
    return lax.dot_general(a, b, (((1,), (1,)), ((), ())), preferred_element_type=F32)


def _in_proj_kernel(x_ref, g_ref, w_ref, o_ref, f_ref, h_ref, *, f_tile, f_col):
    @pl.when(pl.program_id(1) == 0)
    def _():
        _norm_rows(x_ref, g_ref, h_ref, x_ref.shape[0])

    y = _dot_nt(h_ref[...], w_ref[...])
    o_ref[...] = y.astype(o_ref.dtype)

    @pl.when(pl.program_id(1) == f_tile)
    def _():
        f_ref[...] = y[:, f_col:f_col + LANES]


def _in_proj(x, g, w, layer):
    t, d = x.shape
    n = w.shape[1]
    tm = _tile(t, 1024, 8)
    tn = _tile(n, 1280, LANES)
    return pl.pallas_call(
        functools.partial(_in_proj_kernel, f_tile=OFF_MKR // tn, f_col=OFF_MKR % tn),
        grid=(t // tm, n // tn),
        in_specs=[pl.BlockSpec((tm, d), lambda i, j: (i, 0)),
                  _layer_full(g, layer),
                  _layer_spec(w, layer, (tn, d), lambda i, j: (j, 0))],
        out_specs=[pl.BlockSpec((tm, tn), lambda i, j: (i, j)),
                   pl.BlockSpec((tm, LANES), lambda i, j: (i, 0)),
                   pl.BlockSpec((tm, d), lambda i, j: (i, 0))],
        out_shape=[jax.ShapeDtypeStruct((t, n), BF16), jax.ShapeDtypeStruct((t, LANES), F32),
                   jax.ShapeDtypeStruct((t, d), BF16)],
        compiler_params=_params("parallel", "arbitrary"),
        name="in_proj",
    )(x, g, w)


def _fox_decay_kernel(f_ref, b_ref, tri_ref, o_ref, carry_ref):
    @pl.when(pl.program_id(1) == 0)
    def _():
        carry_ref[...] = jnp.zeros_like(carry_ref)

    f = f_ref[...] + b_ref[...]
    lf = jnp.minimum(f, 0.0) - jnp.log1p(jnp.exp(-jnp.abs(f)))
    lane = lax.broadcasted_iota(jnp.int32, lf.shape, 1)
    hi = lf.astype(BF16).astype(F32)
    r1 = lf - hi
    mid = r1.astype(BF16).astype(F32)
    lo = r1 - mid
    packed = jnp.where(lane < FF_LANE + FOX_HEADS, hi,
                       jnp.where(lane < FF_LANE + 2 * FOX_HEADS, pltpu.roll(mid, FOX_HEADS, axis=1),
                                 pltpu.roll(lo, 2 * FOX_HEADS, axis=1)))
    cp = jnp.dot(tri_ref[...], packed.astype(BF16), preferred_element_type=F32)
    c = (cp + pltpu.roll(cp, LANES - FOX_HEADS, axis=1)
         + pltpu.roll(cp, LANES - 2 * FOX_HEADS, axis=1)) + carry_ref[0:1, :]
    carry_ref[...] = jnp.broadcast_to(c[-1:, :], carry_ref.shape)
    for h in range(FOX_HEADS):
        src = FF_LANE + h
        rest = jnp.broadcast_to(c[:, src:src + 1], lane.shape) * (-(FOX_DH ** 0.5))
        slab = jnp.zeros(lane.shape, F32)
        for term in range(BIAS_TERMS):
            part = rest.astype(BF16).astype(F32)
            slab = jnp.where(lane == term, part, slab)
            rest = rest - part
        o_ref[:, h * LANES:(h + 1) * LANES] = slab.astype(BF16)


def _fox_decay(f, b_ff, layer, batch, seq):
    t = f.shape[0]
    ts = _tile(seq, 1024, 8)
    nb = seq // ts
    tri = (jnp.arange(ts)[:, None] >= jnp.arange(ts)[None, :]).astype(BF16)
    return pl.pallas_call(
        _fox_decay_kernel,
        grid=(batch, nb),
        in_specs=[pl.BlockSpec((ts, LANES), lambda b, n: (b * nb + n, 0)),
                  _layer_full(b_ff, layer),
                  pl.BlockSpec((ts, ts), lambda b, n: (0, 0))],
        out_specs=pl.BlockSpec((ts, FOX_HEADS * LANES), lambda b, n: (b * nb + n, 0)),
        out_shape=jax.ShapeDtypeStruct((t, FOX_HEADS * LANES), BF16),
        scratch_shapes=[pltpu.VMEM((8, LANES), F32)],
        compiler_params=_params("parallel", "arbitrary"),
        name="fox_decay_cumsum",
    )(f, b_ff, tri)


def _mla_prep_kernel(mq_ref, mkv_ref, kr_ref, krr_ref, gq_ref, gkv_ref, wq_ref, wqr_ref,
                     wk_ref, wv_ref, cos_ref, sin_ref, q_ref, k_ref, v_ref):
    cos = cos_ref[...]
    sin = sin_ref[...]
    cqn = _rms(mq_ref[...].astype(F32), gq_ref[...]).astype(BF16)
    q = jnp.dot(cqn, wq_ref[...], preferred_element_type=F32)
    qrot = jnp.dot(cqn, wqr_ref[...], preferred_element_type=F32)
    ckvn = _rms(mkv_ref[...].astype(F32), gkv_ref[...]).astype(BF16)
    kn = jnp.dot(ckvn, wk_ref[...], preferred_element_type=F32)
    v_ref[...] = jnp.dot(ckvn, wv_ref[...], preferred_element_type=F32).astype(BF16)
    kr = (kr_ref[...].astype(F32) * cos + krr_ref[...].astype(F32) * sin).astype(BF16)
    for h in range(MLA_HEADS):
        lo = h * MLA_QK_PAD
        q_ref[:, lo:lo + LANES] = q[:, lo:lo + LANES].astype(BF16)
        q_ref[:, lo + LANES:lo + 2 * LANES] = (
            q[:, lo + LANES:lo + 2 * LANES] * cos + qrot[:, h * LANES:(h + 1) * LANES] * sin).astype(BF16)
        k_ref[:, lo:lo + LANES] = kn[:, h * LANES:(h + 1) * LANES].astype(BF16)
        k_ref[:, lo + LANES:lo + 2 * LANES] = kr


def _mla_prep(p, gq, gkv, wq, wqr, wk, wv, cos, sin, layer, seq):
    t = p.shape[0]
    tm = _tile(seq, 1024, 8)
    ns = seq // tm
    ql, kvl = wq.shape[1], wk.shape[1]
    row = lambda i: (i, 0)
    return pl.pallas_call(
        _mla_prep_kernel,
        grid=(t // tm,),
        in_specs=[pl.BlockSpec((tm, ql), lambda i: (i, OFF_MQ // ql)),
                  pl.BlockSpec((tm, kvl), lambda i: (i, OFF_MKV // kvl)),
                  pl.BlockSpec((tm, LANES), lambda i: (i, OFF_MKR // LANES)),
                  pl.BlockSpec((tm, LANES), lambda i: (i, OFF_MKRR // LANES)),
                  _layer_full(gq, layer), _layer_full(gkv, layer),
                  _layer_full(wq, layer), _layer_full(wqr, layer),
                  _layer_full(wk, layer), _layer_full(wv, layer),
                  pl.BlockSpec((tm, LANES), lambda i: (i % ns, 0)),
                  pl.BlockSpec((tm, LANES), lambda i: (i % ns, 0))],
        out_specs=[pl.BlockSpec((tm, MLA_HEADS * MLA_QK_PAD), row),
                   pl.BlockSpec((tm, MLA_HEADS * MLA_QK_PAD), row),
                   pl.BlockSpec((tm, MLA_W), row)],
        out_shape=[jax.ShapeDtypeStruct((t, MLA_HEADS * MLA_QK_PAD), BF16),
                   jax.ShapeDtypeStruct((t, MLA_HEADS * MLA_QK_PAD), BF16),
                   jax.ShapeDtypeStruct((t, MLA_W), BF16)],
        compiler_params=_params("parallel"),
        name="mla_prep",
    )(p, p, p, p, gq, gkv, wq, wqr, wk, wv, cos, sin)


def _attn_kernel(*refs, heads, dq, dv, coef, has_bias, tq):
    if has_bias:
        q_ref, k_ref, v_ref, rel_ref, kb_ref, o_ref, m_ref, l_ref, acc_ref = refs
    else:
        q_ref, k_ref, v_ref, rel_ref, o_ref, m_ref, l_ref, acc_ref = refs
    qi = pl.program_id(2)

    def accumulate(k0, visible, first):
        rows = pl.ds(k0, tq)
        ones_v = jnp.ones((tq, LANES), BF16)
        if has_bias:
            lane = lax.broadcasted_iota(jnp.int32, (tq, LANES), 1)
            ones_q = jnp.where(lane < BIAS_TERMS, 1.0, 0.0).astype(BF16)
        for h in range(heads):
            q = q_ref[:, h * dq:(h + 1) * dq]
            k = k_ref[rows, h * dq:(h + 1) * dq]
            if has_bias:
                q = jnp.concatenate([q, ones_q], axis=1)
                k = jnp.concatenate([k, kb_ref[rows, h * LANES:(h + 1) * LANES]], axis=1)
            s = lax.dot_general(q, k, (((1,), (1,)), ((), ())), preferred_element_type=F32)
            if visible is not None:
                s = jnp.where(visible, s, -jnp.inf)
            m_new = jnp.broadcast_to(jnp.max(s, axis=1, keepdims=True), (tq, LANES))
            if not first:
                m_prev = m_ref[h]
                m_new = jnp.maximum(m_prev, m_new)
                alpha = jnp.exp2((m_prev - m_new) * coef)
            p = jnp.concatenate(
                [jnp.exp2((s[:, j * LANES:(j + 1) * LANES] - m_new) * coef) for j in range(tq // LANES)],
                axis=1).astype(BF16)
            v1 = jnp.concatenate([v_ref[rows, h * dv:(h + 1) * dv], ones_v], axis=1)
            pv = jnp.dot(p, v1, preferred_element_type=F32)
            if first:
                acc_ref[:, h * dv:(h + 1) * dv] = pv[:, :dv]
                l_ref[h] = pv[:, dv:]
            else:
                acc_ref[:, h * dv:(h + 1) * dv] = alpha * acc_ref[:, h * dv:(h + 1) * dv] + pv[:, :dv]
                l_ref[h] = alpha * l_ref[h] + pv[:, dv:]
            m_ref[h] = m_new

    accumulate(pl.multiple_of(qi * tq, tq), rel_ref[...] >= 0, True)

    def below_diagonal_pair(pi, carry):
        k0 = pl.multiple_of(pi * (2 * tq), 2 * tq)
        accumulate(k0, None, False)
        accumulate(pl.multiple_of(k0 + tq, tq), None, False)
        return carry

    lax.fori_loop(0, qi // 2, below_diagonal_pair, 0)

    @pl.when(qi % 2 == 1)
    def _():
        accumulate(pl.multiple_of((qi - 1) * tq, tq), None, False)

    for h in range(heads):
        o_ref[:, h * dv:(h + 1) * dv] = (acc_ref[:, h * dv:(h + 1) * dv] / l_ref[h]).astype(o_ref.dtype)


def _attention(q_arr, q_cb, k_arr, k_cb, v_arr, v_cb, rel, kbias, *, batch, seq, heads, dq, dv, scale, name):
    assert dv == LANES and heads % ATTN_HEAD_GROUPS == 0
    tq = rel.shape[0]
    nq = seq // tq
    ng = ATTN_HEAD_GROUPS
    hg = heads // ng
    has_bias = kbias is not None
    in_specs = [pl.BlockSpec((tq, hg * dq), lambda b, g, qi: (b * nq + qi, q_cb * ng + g)),
                pl.BlockSpec((seq, hg * dq), lambda b, g, qi: (b, k_cb * ng + g)),
                pl.BlockSpec((seq, hg * dv), lambda b, g, qi: (b, v_cb * ng + g)),
                pl.BlockSpec((tq, tq), lambda b, g, qi: (0, 0))]
    args = [q_arr, k_arr, v_arr, rel]
    if has_bias:
        in_specs.append(pl.BlockSpec((seq, hg * LANES), lambda b, g, qi: (b, g)))
        args.append(kbias)
    return pl.pallas_call(
        functools.partial(_attn_kernel, heads=hg, dq=dq, dv=dv, coef=scale * math.log2(math.e),
                          has_bias=has_bias, tq=tq),
        grid=(batch, ng, nq),
        in_specs=in_specs,
        out_specs=pl.BlockSpec((tq, hg * dv), lambda b, g, qi: (b * nq + qi, g)),
        out_shape=jax.ShapeDtypeStruct((batch * seq, heads * dv), BF16),
        scratch_shapes=[pltpu.VMEM((hg, tq, LANES), F32),
                        pltpu.VMEM((hg, tq, LANES), F32),
                        pltpu.VMEM((tq, hg * dv), F32)],
        compiler_params=_params("parallel", "parallel", "arbitrary"),
        name=name,
    )(*args)


def _retention_kernel(q_ref, k_ref, v_ref, g_ref, cos_ref, sin_ref, dec_ref, sin_state_ref,
                      cross_ref, o_ref, state_ref, *, block_decay):
    @pl.when(pl.program_id(1) == 0)
    def _():
        state_ref[...] = jnp.zeros_like(state_ref)

    cos = cos_ref[...]
    sin = sin_ref[...]

    def rope(x):
        return x * cos + pltpu.roll(x, RET_DK // 2, axis=1) * sin

    for h in range(RET_HEADS):
        qs = slice(h * RET_DK, (h + 1) * RET_DK)
        vs = slice(h * RET_DV, (h + 1) * RET_DV)
        q = rope(q_ref[:, qs].astype(F32)).astype(BF16)
        kf = rope(k_ref[:, qs].astype(F32)) * (RET_DK ** -0.5)
        v = v_ref[:, vs]
        scores = lax.dot_general(q, kf.astype(BF16), (((1,), (1,)), ((), ())),
                                 preferred_element_type=F32) * dec_ref[h]
        o = jnp.dot(scores.astype(BF16), v, preferred_element_type=F32)
        state = state_ref[h]
        cross = jnp.dot(q, state.astype(BF16), preferred_element_type=F32)
        cd = cross_ref[h]
        o = o + cross * jnp.concatenate([cd, cd], axis=1)
        o = o * lax.rsqrt(jnp.mean(o * o, axis=-1, keepdims=True) + NORM_EPS)
        g = g_ref[:, vs].astype(F32)
        o_ref[:, vs] = (o * (g * jax.nn.sigmoid(g))).astype(o_ref.dtype)
        ks = (kf * sin_state_ref[h]).astype(BF16)
        kv = lax.dot_general(ks, v, (((0,), (0,)), ((), ())), preferred_element_type=F32)
        state_ref[h] = block_decay[h] * state + kv


def _retention(p, cos, sin, dec, sin_state, cross, block_decay, batch, seq):
    blk = dec.shape[1]
    nb = seq // blk
    row = lambda b, n: (b * nb + n, 0)
    const3 = lambda b, n: (0, 0, 0)
    return pl.pallas_call(
        functools.partial(_retention_kernel, block_decay=block_decay),
        grid=(batch, nb),
        in_specs=[pl.BlockSpec((blk, RET_QK_W), lambda b, n: (b * nb + n, OFF_RQ // RET_QK_W)),
                  pl.BlockSpec((blk, RET_QK_W), lambda b, n: (b * nb + n, OFF_RK // RET_QK_W)),
                  pl.BlockSpec((blk, RET_V_W), lambda b, n: (b * nb + n, OFF_RV // RET_V_W)),
                  pl.BlockSpec((blk, RET_V_W), lambda b, n: (b * nb + n, OFF_RG // RET_V_W)),
                  pl.BlockSpec((blk, RET_DK), lambda b, n: (n, 0)),
                  pl.BlockSpec((blk, RET_DK), lambda b, n: (n, 0)),
                  pl.BlockSpec(dec.shape, const3),
                  pl.BlockSpec(sin_state.shape, const3),
                  pl.BlockSpec(cross.shape, const3)],
        out_specs=pl.BlockSpec((blk, RET_V_W), row),
        out_shape=jax.ShapeDtypeStruct((batch * seq, RET_V_W), BF16),
        scratch_shapes=[pltpu.VMEM((RET_HEADS, RET_DK, RET_DV), F32)],
        compiler_params=_params("parallel", "arbitrary"),
        name="retention",
    )(p, p, p, p, cos, sin, dec, sin_state, cross)


def _gated_merge_kernel(h_ref, a_ref, b_ref, c_ref, wg0_ref, wg1_ref, wg2_ref,
                        wa_ref, wb_ref, wc_ref, o_ref):
    h = h_ref[...]

    def branch(wg_ref, br_ref, w_ref):
        gate = jax.nn.sigmoid(_dot_nt(h, wg_ref[...]))
        return gate * jnp.dot(br_ref[...], w_ref[...], preferred_element_type=F32)

    m = branch(wg0_ref, a_ref, wa_ref)
    m = m + branch(wg1_ref, b_ref, wb_ref)
    m = m + branch(wg2_ref, c_ref, wc_ref)
    o_ref[...] = m.astype(o_ref.dtype)


def _gated_merge(h, a, bm, c, w_gates, wa, wb, wc, layer):
    t, d = h.shape
    tm = _tile(t, 1024, 8)
    tn = _tile(d, 512, LANES)
    nj = d // tn
    act = lambda arr: pl.BlockSpec((tm, arr.shape[1]), lambda i, j: (i, 0))
    wgt = lambda w: _layer_spec(w, layer, (w.shape[1], tn), lambda i, j: (0, j))
    gate = lambda n: _layer_spec(w_gates, layer, (tn, d), lambda i, j: (n * nj + j, 0))
    return pl.pallas_call(
        _gated_merge_kernel,
        grid=(t // tm, nj),
        in_specs=[act(h), act(a), act(bm), act(c), gate(0), gate(1), gate(2), wgt(wa), wgt(wb), wgt(wc)],
        out_specs=pl.BlockSpec((tm, tn), lambda i, j: (i, j)),
        out_shape=jax.ShapeDtypeStruct((t, d), BF16),
        compiler_params=_params("parallel", "arbitrary"),
        name="gated_merge",
    )(h, a, bm, c, w_gates, w_gates, w_gates, wa, wb, wc)


def _mm_res_kernel(a_ref, w_ref, r_ref, o_ref):
    o_ref[...] = r_ref[...] + jnp.dot(a_ref[...], w_ref[...], preferred_element_type=F32)


def _matmul_residual(a, w, r, layer, tm_pref, tn_pref, name):
    t, k = a.shape
    n = w.shape[2]
    tm = _tile(t, tm_pref, 8)
    tn = _tile(n, tn_pref, LANES)
    return pl.pallas_call(
        _mm_res_kernel,
        grid=(n // tn, t // tm),
        in_specs=[pl.BlockSpec((tm, k), lambda j, i: (i, 0)),
                  _layer_spec(w, layer, (k, tn), lambda j, i: (0, j)),
                  pl.BlockSpec((tm, tn), lambda j, i: (i, j))],
        out_specs=pl.BlockSpec((tm, tn), lambda j, i: (i, j)),
        out_shape=jax.ShapeDtypeStruct((t, n), F32),
        compiler_params=_params("parallel", "arbitrary"),
        name=name,
    )(a, w, r)


def _out_proj_kernel(a_ref, w_ref, r_ref, g_ref, o_ref, h_ref):
    y = r_ref[...] + jnp.dot(a_ref[...], w_ref[...], preferred_element_type=F32)
    o_ref[...] = y
    h_ref[...] = _rms(y, g_ref[...]).astype(h_ref.dtype)


def _out_proj(a, w, r, g, layer):
    t, d = r.shape
    tm = _tile(t, 512, 8)
    row = pl.BlockSpec((tm, d), lambda i: (i, 0))
    return pl.pallas_call(
        _out_proj_kernel,
        grid=(t // tm,),
        in_specs=[pl.BlockSpec((tm, a.shape[1]), lambda i: (i, 0)), _layer_full(w, layer), row,
                  _layer_full(g, layer)],
        out_specs=[row, row],
        out_shape=[jax.ShapeDtypeStruct((t, d), F32), jax.ShapeDtypeStruct((t, d), BF16)],
        compiler_params=_params("parallel"),
        name="out_proj",
    )(a, w, r, g)


def _ffn_up_kernel(h_ref, wu_ref, wg_ref, cw_ref, cb_ref, o_ref, tail_ref, *, seq):
    tm = h_ref.shape[0]
    h = h_ref[...]

    @pl.when((pl.program_id(1) * tm) % seq == 0)
    def _():
        tail_ref[...] = jnp.zeros_like(tail_ref)

    u = jnp.dot(h, wu_ref[...], preferred_element_type=F32)
    gate = jnp.dot(h, wg_ref[...], preferred_element_type=F32)

    def conv_act(v, g):
        conv = (cb_ref[...] + cw_ref[2:3, :] * v
                + cw_ref[1:2, :] * pltpu.roll(v, 1, axis=0)
                + cw_ref[0:1, :] * pltpu.roll(v, 2, axis=0))
        return jax.nn.gelu(conv) * g

    nt = tail_ref.shape[0]
    o_ref[...] = conv_act(u, gate).astype(o_ref.dtype)
    head = jnp.concatenate([tail_ref[...], u[:nt, :]], axis=0)
    o_ref[0:nt, :] = conv_act(head, jnp.concatenate([gate[:nt, :], gate[:nt, :]], axis=0))[nt:, :].astype(
        o_ref.dtype)
    tail_ref[...] = u[tm - nt:, :]


def _ffn_up(h, wu, wg, cw, cb, layer, seq):
    t, d = h.shape
    f = wu.shape[2]
    tm = _tile(seq, FFN_TOKEN_TILE, 8)
    tf = _tile(f, 512, LANES)
    col = lambda j, i: (0, j)
    return pl.pallas_call(
        functools.partial(_ffn_up_kernel, seq=seq),
        grid=(f // tf, t // tm),
        in_specs=[pl.BlockSpec((tm, d), lambda j, i: (i, 0)),
                  _layer_spec(wu, layer, (d, tf), col),
                  _layer_spec(wg, layer, (d, tf), col),
                  _layer_spec(cw, layer, (8, tf), col),
                  _layer_spec(cb, layer, (1, tf), col)],
        out_specs=pl.BlockSpec((tm, tf), lambda j, i: (i, j)),
        out_shape=jax.ShapeDtypeStruct((t, f), BF16),
        scratch_shapes=[pltpu.VMEM((16, tf), F32)],
        compiler_params=_params("arbitrary", "arbitrary"),
        name="ffn_up_conv_gate",
    )(h, wu, wg, cw, cb)


def _final_norm_kernel(x_ref, g_ref, o_ref):
    o_ref[...] = _rms(x_ref[...], g_ref[...])


def _final_norm(x, g):
    t, d = x.shape
    tm = _tile(t, 512, 8)
    return pl.pallas_call(
        _final_norm_kernel,
        grid=(t // tm,),
        in_specs=[pl.BlockSpec((tm, d), lambda i: (i, 0)), pl.BlockSpec((1, d), lambda i: (0, 0))],
        out_specs=pl.BlockSpec((tm, d), lambda i: (i, 0)),
        out_shape=jax.ShapeDtypeStruct((t, d), F32),
        compiler_params=_params("parallel"),
        name="final_norm",
    )(x, g)


def _rope_tables(seq, d):
    pos = jnp.arange(seq, dtype=F32)
    inv_freq = ROPE_THETA ** (-jnp.arange(0, d, 2, dtype=F32) / d)
    ang = pos[:, None] * inv_freq[None, :]
    return jnp.cos(ang), jnp.sin(ang)


def _retention_tables(blk):
    log_gamma = jnp.log(1.0 - 2.0 ** (-5.0 - jnp.arange(RET_HEADS, dtype=F32)))
    idx = jnp.arange(blk, dtype=F32)
    chunk = jnp.arange(blk) // CHUNK
    dist = jnp.abs(idx[:, None] - idx[None, :])
    dec = jnp.where(chunk[None, :] <= chunk[:, None],
                    jnp.exp(log_gamma[:, None, None] * dist[None]), 0.0)
    sin_state = jnp.exp(log_gamma[:, None] * (blk - 1 - idx)[None, :])
    cross = jnp.exp(log_gamma[:, None] * (idx + 1.0)[None, :])
    bcast = lambda a: jnp.broadcast_to(a[:, :, None], (RET_HEADS, blk, RET_DK))
    block_decay = tuple(math.exp(math.log(1.0 - 2.0 ** (-5.0 - h)) * blk) for h in range(RET_HEADS))
    return dec, bcast(sin_state), bcast(cross), block_decay


def _rotate_half_cols(w):
    half = w.shape[-1] // 2
    return jnp.concatenate([-w[..., half:], w[..., :half]], axis=-1)


def kernel(x, norm1_g, w_in, mla_q_norm_g, mla_kv_norm_g, mla_w_uq, mla_w_ukv, fox_b_f,
           w_br_fox, w_br_mla, w_br_ret, w_out, norm2_g, ffn_w_up, ffn_w_gate,
           ffn_conv_w, ffn_conv_b, ffn_w_down, final_norm_g):
    batch, seq, d = x.shape
    depth = w_in.shape[0]
    t = batch * seq
    q_lora = mla_w_uq.shape[1]
    kv_lora = mla_w_ukv.shape[1]

    widths = (FOX_W, FOX_W, FOX_W, FOX_HEADS, q_lora, kv_lora, MLA_ROPE,
              RET_QK_W, RET_QK_W, RET_V_W, RET_V_W, N_BRANCH * d)
    offs = [0]
    for wd in widths:
        offs.append(offs[-1] + wd)
    run_t = lambda k0, k1: jnp.swapaxes(w_in[:, :, offs[k0]:offs[k1]], 1, 2).astype(BF16)
    zeros = lambda n: jnp.zeros((depth, n, d), BF16)
    small = run_t(3, 7)
    cut = [0, FOX_HEADS, FOX_HEADS + q_lora, FOX_HEADS + q_lora + kv_lora]
    ff_t, mq_t, mkv_t, rope_t = (small[:, a:b] for a, b in zip(cut, cut[1:] + [small.shape[1]]))
    mla_rows = jnp.concatenate(
        [mkv_t, mq_t, rope_t, ff_t, zeros(LANES - MLA_ROPE - FOX_HEADS),
         -rope_t[:, MLA_ROPE // 2:], rope_t[:, :MLA_ROPE // 2], zeros(LANES - MLA_ROPE)], axis=1)
    assert OFF_MKV + mla_rows.shape[1] == P_WIDTH and q_lora == 512 and kv_lora == 256
    w_p = jnp.zeros((depth, P_WIDTH, d), BF16)
    w_p = lax.dynamic_update_slice(w_p, run_t(7, 11), (0, OFF_RQ, 0))
    w_p = lax.dynamic_update_slice(w_p, run_t(0, 3), (0, OFF_FQ, 0))
    w_p = lax.dynamic_update_slice(w_p, mla_rows, (0, OFF_MKV, 0))
    w_gates = run_t(11, 12)
    b_ff = jnp.pad(fox_b_f, ((0, 0), (FF_LANE, LANES - FF_LANE - FOX_HEADS)))[:, None, :]

    uq = mla_w_uq.reshape(depth, q_lora, MLA_HEADS, MLA_NOPE + MLA_ROPE)
    uq_rope = uq[..., MLA_NOPE:]
    pad_q = jnp.zeros((depth, q_lora, MLA_HEADS, MLA_QK_PAD - MLA_NOPE - MLA_ROPE), uq.dtype)
    w_q = jnp.concatenate([uq, pad_q], axis=-1).reshape(depth, q_lora, -1).astype(BF16)
    pad_r = jnp.zeros((depth, q_lora, MLA_HEADS, LANES - MLA_ROPE), uq.dtype)
    w_qr = jnp.concatenate([_rotate_half_cols(uq_rope), pad_r], axis=-1).reshape(depth, q_lora, -1).astype(BF16)
    ukv = mla_w_ukv.reshape(depth, kv_lora, MLA_HEADS, MLA_NOPE + MLA_V)
    w_k = ukv[..., :MLA_NOPE].reshape(depth, kv_lora, -1).astype(BF16)
    w_v = ukv[..., MLA_NOPE:].reshape(depth, kv_lora, -1).astype(BF16)

    w_bf, w_bm, w_br = w_br_fox.astype(BF16), w_br_mla.astype(BF16), w_br_ret.astype(BF16)
    w_o = w_out.astype(BF16)
    w_up, w_gate, w_down = ffn_w_up.astype(BF16), ffn_w_gate.astype(BF16), ffn_w_down.astype(BF16)
    conv_w = jnp.pad(ffn_conv_w, ((0, 0), (0, 8 - CONV_W), (0, 0)))
    conv_b = ffn_conv_b[:, None, :]
    g1, g2 = norm1_g[:, None, :], norm2_g[:, None, :]
    gq, gkv = mla_q_norm_g[:, None, :], mla_kv_norm_g[:, None, :]

    cos64, sin64 = _rope_tables(seq, MLA_ROPE)
    zpad = jnp.zeros((seq, LANES - MLA_ROPE), F32)
    mla_cos = jnp.concatenate([cos64, cos64, zpad], axis=1)
    mla_sin = jnp.concatenate([sin64, sin64, zpad], axis=1)
    cos128, sin128 = _rope_tables(seq, RET_DK)
    ret_cos = jnp.concatenate([cos128, cos128], axis=1)
    ret_sin = jnp.concatenate([-sin128, sin128], axis=1)
    ret_blk = _tile(seq, 512, CHUNK)
    dec, sin_state, cross, block_decay = _retention_tables(ret_blk)

    att_blk = _tile(seq, 512, CHUNK)
    row = jnp.arange(att_blk, dtype=jnp.int32)[:, None]
    col = jnp.arange(att_blk, dtype=jnp.int32)[None, :]
    rel_frame = row - col
    rel_chunk = (row // CHUNK) * CHUNK + (CHUNK - 1) - col

    xf = x.reshape(t, d)
    for i in range(depth):
        p, f_logit, h1 = _in_proj(xf, g1, w_p, i)
        kbias = _fox_decay(f_logit, b_ff, i, batch, seq)
        a = _attention(p, OFF_FQ // FOX_W, p, OFF_FK // FOX_W, p, OFF_FV // FOX_W, rel_frame, kbias,
                       batch=batch, seq=seq, heads=FOX_HEADS, dq=FOX_DH, dv=FOX_DH,
                       scale=FOX_DH ** -0.5, name="fox_attention")
        qm, km, vm = _mla_prep(p, gq, gkv, w_q, w_qr, w_k, w_v, mla_cos, mla_sin, i, seq)
        bm = _attention(qm, 0, km, 0, vm, 0, rel_chunk, None,
                        batch=batch, seq=seq, heads=MLA_HEADS, dq=MLA_QK_PAD, dv=MLA_V,
                        scale=(MLA_NOPE + MLA_ROPE) ** -0.5, name="mla_attention")
        cr = _retention(p, ret_cos, ret_sin, dec, sin_state, cross, block_decay, batch, seq)
        merged = _gated_merge(h1, a, bm, cr, w_gates, w_bf, w_bm, w_br, i)
        xf, h2 = _out_proj(merged, w_o, xf, g2, i)
        act = _ffn_up(h2, w_up, w_gate, conv_w, conv_b, i, seq)
        xf = _matmul_residual(act, w_down, xf, i, 512, 1024, "ffn_down")
    return _final_norm(xf, final_norm_g[None, :]).reshape(batch, seq, d)
```

```python
import functools
import math

import jax
import jax.numpy as jnp
from jax import lax
from jax.experimental import pallas as pl
from jax.experimental.pallas import tpu as pltpu

F32 = jnp.float32
BF16 = jnp.bfloat16

NORM_EPS = 1e-6
ROPE_THETA = 10000.0
CHUNK = 64

FOX_HEADS, FOX_DH = 6, 128
MLA_HEADS, MLA_NOPE, MLA_ROPE, MLA_V = 6, 128, 64, 128
MLA_QK_PAD = 256
RET_HEADS, RET_DK, RET_DV = 4, 128, 256
N_BRANCH = 3
CONV_W = 3
LANES = 128
FFN_TOKEN_TILE = 1024
BIAS_TERMS = 3
ATTN_HEAD_GROUPS = 1

FOX_W = FOX_HEADS * FOX_DH
MLA_W = MLA_HEADS * MLA_V
RET_QK_W = RET_HEADS * RET_DK
RET_V_W = RET_HEADS * RET_DV

OFF_RQ = 0
OFF_RK = OFF_RQ + RET_QK_W
OFF_RV = OFF_RK + RET_QK_W
OFF_RG = OFF_RV + RET_V_W
OFF_FQ = OFF_RG + RET_V_W
OFF_FK = OFF_FQ + FOX_W
OFF_FV = OFF_FK + FOX_W
OFF_MKV = OFF_FV + FOX_W
OFF_MQ = OFF_MKV + 256
OFF_MKR = OFF_MQ + 512
FF_LANE = MLA_ROPE
OFF_MKRR = OFF_MKR + LANES
P_WIDTH = OFF_MKRR + LANES

VMEM_LIMIT_BYTES = 56 * 1024 * 1024


def _tile(n, pref, align):
    t = (min(pref, n) // align) * align
    while t > align and n % t:
        t -= align
    assert t >= align and n % t == 0, (n, pref, align)
    return t


def _params(*sem):
    return pltpu.CompilerParams(dimension_semantics=sem, vmem_limit_bytes=VMEM_LIMIT_BYTES)


def _layer_spec(arr, layer, block, index):
    return pl.BlockSpec((None,) + tuple(block), lambda *ids: (layer,) + tuple(index(*ids)))


def _layer_full(arr, layer):
    zeros = (0,) * (arr.ndim - 1)
    return pl.BlockSpec((None,) + arr.shape[1:], lambda *ids: (layer,) + zeros)


def _rms(x, g):
    return x * lax.rsqrt(jnp.mean(x * x, axis=-1, keepdims=True) + NORM_EPS) * g


def _norm_rows(x_ref, g_ref, h_ref, rows, h_off=0):
    rc = _tile(rows, 256, 8)

    def body(c, carry):
        r0 = pl.multiple_of(c * rc, rc)
        h_ref[pl.ds(h_off + r0, rc), :] = _rms(x_ref[pl.ds(r0, rc), :], g_ref[...]).astype(BF16)
        return carry

    lax.fori_loop(0, rows // rc, body, 0)


def _dot_nt(a, b):
    return lax.dot_general(a, b, (((1,), (1,)), ((), ())), preferred_element_type=F32)


def _in_proj_kernel(x_ref, g_ref, w_ref, o_ref, f_ref, h_ref, *, f_tile, f_col):
    @pl.when(pl.program_id(1) == 0)
    def _():
        _norm_rows(x_ref, g_ref, h_ref, x_ref.shape[0])

    y = _dot_nt(h_ref[...], w_ref[...])
    o_ref[...] = y.astype(o_ref.dtype)

    @pl.when(pl.program_id(1) == f_tile)
    def _():
        f_ref[...] = y[:, f_col:f_col + LANES]


def _in_proj(x, g, w, layer):
    t, d = x.shape
    n = w.shape[1]
    tm = _tile(t, 1024, 8)
    tn = _tile(n, 1280, LANES)
    return pl.pallas_call(
        functools.partial(_in_proj_kernel, f_tile=OFF_MKR // tn, f_col=OFF_MKR % tn),
        grid=(t // tm, n // tn),
        in_specs=[pl.BlockSpec((tm, d), lambda i, j: (i, 0)),
                  _layer_full(g, layer),
                  _layer_spec(w, layer, (tn, d), lambda i, j: (j, 0))],
        out_specs=[pl.BlockSpec((tm, tn), lambda i, j: (i, j)),
                   pl.BlockSpec((tm, LANES), lambda i, j: (i, 0)),
                   pl.BlockSpec((tm, d), lambda i, j: (i, 0))],
        out_shape=[jax.ShapeDtypeStruct((t, n), BF16), jax.ShapeDtypeStruct((t, LANES), F32),
                   jax.ShapeDtypeStruct((t, d), BF16)],
        compiler_params=_params("parallel", "arbitrary"),
        name="in_proj",
    )(x, g, w)


def _fox_decay_kernel(f_ref, b_ref, tri_ref, o_ref, carry_ref):
    @pl.when(pl.program_id(1) == 0)
    def _():
        carry_ref[...] = jnp.zeros_like(carry_ref)

    f = f_ref[...] + b_ref[...]
    lf = jnp.minimum(f, 0.0) - jnp.log1p(jnp.exp(-jnp.abs(f)))
    lane = lax.broadcasted_iota(jnp.int32, lf.shape, 1)
    hi = lf.astype(BF16).astype(F32)
    r1 = lf - hi
    mid = r1.astype(BF16).astype(F32)
    lo = r1 - mid
    packed = jnp.where(lane < FF_LANE + FOX_HEADS, hi,
                       jnp.where(lane < FF_LANE + 2 * FOX_HEADS, pltpu.roll(mid, FOX_HEADS, axis=1),
                                 pltpu.roll(lo, 2 * FOX_HEADS, axis=1)))
    cp = jnp.dot(tri_ref[...], packed.astype(BF16), preferred_element_type=F32)
    c = (cp + pltpu.roll(cp, LANES - FOX_HEADS, axis=1)
         + pltpu.roll(cp, LANES - 2 * FOX_HEADS, axis=1)) + carry_ref[0:1, :]
    carry_ref[...] = jnp.broadcast_to(c[-1:, :], carry_ref.shape)
    for h in range(FOX_HEADS):
        src = FF_LANE + h
        rest = jnp.broadcast_to(c[:, src:src + 1], lane.shape) * (-(FOX_DH ** 0.5))
        slab = jnp.zeros(lane.shape, F32)
        for term in range(BIAS_TERMS):
            part = rest.astype(BF16).astype(F32)
            slab = jnp.where(lane == term, part, slab)
            rest = rest - part
        o_ref[:, h * LANES:(h + 1) * LANES] = slab.astype(BF16)


def _fox_decay(f, b_ff, layer, batch, seq):
    t = f.shape[0]
    ts = _tile(seq, 1024, 8)
    nb = seq // ts
    tri = (jnp.arange(ts)[:, None] >= jnp.arange(ts)[None, :]).astype(BF16)
    return pl.pallas_call(
        _fox_decay_kernel,
        grid=(batch, nb),
        in_specs=[pl.BlockSpec((ts, LANES), lambda b, n: (b * nb + n, 0)),
                  _layer_full(b_ff, layer),
                  pl.BlockSpec((ts, ts), lambda b, n: (0, 0))],
        out_specs=pl.BlockSpec((ts, FOX_HEADS * LANES), lambda b, n: (b * nb + n, 0)),
        out_shape=jax.ShapeDtypeStruct((t, FOX_HEADS * LANES), BF16),
        scratch_shapes=[pltpu.VMEM((8, LANES), F32)],
        compiler_params=_params("parallel", "arbitrary"),
        name="fox_decay_cumsum",
    )(f, b_ff, tri)


def _mla_prep_kernel(mq_ref, mkv_ref, kr_ref, krr_ref, gq_ref, gkv_ref, wq_ref, wqr_ref,
                     wk_ref, wv_ref, cos_ref, sin_ref, q_ref, k_ref, v_ref):
    cos = cos_ref[...]
    sin = sin_ref[...]
    cqn = _rms(mq_ref[...].astype(F32), gq_ref[...]).astype(BF16)
    q = jnp.dot(cqn, wq_ref[...], preferred_element_type=F32)
    qrot = jnp.dot(cqn, wqr_ref[...], preferred_element_type=F32)
    ckvn = _rms(mkv_ref[...].astype(F32), gkv_ref[...]).astype(BF16)
    kn = jnp.dot(ckvn, wk_ref[...], preferred_element_type=F32)
    v_ref[...] = jnp.dot(ckvn, wv_ref[...], preferred_element_type=F32).astype(BF16)
    kr = (kr_ref[...].astype(F32) * cos + krr_ref[...].astype(F32) * sin).astype(BF16)
    for h in range(MLA_HEADS):
        lo = h * MLA_QK_PAD
        q_ref[:, lo:lo + LANES] = q[:, lo:lo + LANES].astype(BF16)
        q_ref[:, lo + LANES:lo + 2 * LANES] = (
            q[:, lo + LANES:lo + 2 * LANES] * cos + qrot[:, h * LANES:(h + 1) * LANES] * sin).astype(BF16)
        k_ref[:, lo:lo + LANES] = kn[:, h * LANES:(h + 1) * LANES].astype(BF16)
        k_ref[:, lo + LANES:lo + 2 * LANES] = kr


def _mla_prep(p, gq, gkv, wq, wqr, wk, wv, cos, sin, layer, seq):
    t = p.shape[0]
    tm = _tile(seq, 1024, 8)
    ns = seq // tm
    ql, kvl = wq.shape[1], wk.shape[1]
    row = lambda i: (i, 0)
    return pl.pallas_call(
        _mla_prep_kernel,
        grid=(t // tm,),
        in_specs=[pl.BlockSpec((tm, ql), lambda i: (i, OFF_MQ // ql)),
                  pl.BlockSpec((tm, kvl), lambda i: (i, OFF_MKV // kvl)),
                  pl.BlockSpec((tm, LANES), lambda i: (i, OFF_MKR // LANES)),
                  pl.BlockSpec((tm, LANES), lambda i: (i, OFF_MKRR // LANES)),
                  _layer_full(gq, layer), _layer_full(gkv, layer),
                  _layer_full(wq, layer), _layer_full(wqr, layer),
                  _layer_full(wk, layer), _layer_full(wv, layer),
                  pl.BlockSpec((tm, LANES), lambda i: (i % ns, 0)),
                  pl.BlockSpec((tm, LANES), lambda i: (i % ns, 0))],
        out_specs=[pl.BlockSpec((tm, MLA_HEADS * MLA_QK_PAD), row),
                   pl.BlockSpec((tm, MLA_HEADS * MLA_QK_PAD), row),
                   pl.BlockSpec((tm, MLA_W), row)],
        out_shape=[jax.ShapeDtypeStruct((t, MLA_HEADS * MLA_QK_PAD), BF16),
                   jax.ShapeDtypeStruct((t, MLA_HEADS * MLA_QK_PAD), BF16),
                   jax.ShapeDtypeStruct((t, MLA_W), BF16)],
        compiler_params=_params("parallel"),
        name="mla_prep",
    )(p, p, p, p, gq, gkv, wq, wqr, wk, wv, cos, sin)


def _attn_kernel(*refs, heads, dq, dv, coef, has_bias, tq):
    if has_bias:
        q_ref, k_ref, v_ref, rel_ref, kb_ref, o_ref, m_ref, l_ref, acc_ref = refs
    else:
        q_ref, k_ref, v_ref, rel_ref, o_ref, m_ref, l_ref, acc_ref = refs
    qi = pl.program_id(2)

    def accumulate(k0, visible, first):
        rows = pl.ds(k0, tq)
        ones_v = jnp.ones((tq, LANES), BF16)
        if has_bias:
            lane = lax.broadcasted_iota(jnp.int32, (tq, LANES), 1)
            ones_q = jnp.where(lane < BIAS_TERMS, 1.0, 0.0).astype(BF16)
        for h in range(heads):
            q = q_ref[:, h * dq:(h + 1) * dq]
            k = k_ref[rows, h * dq:(h + 1) * dq]
            if has_bias:
                q = jnp.concatenate([q, ones_q], axis=1)
                k = jnp.concatenate([k, kb_ref[rows, h * LANES:(h + 1) * LANES]], axis=1)
            s = lax.dot_general(q, k, (((1,), (1,)), ((), ())), preferred_element_type=F32)
            if visible is not None:
                s = jnp.where(visible, s, -jnp.inf)
            m_new = jnp.broadcast_to(jnp.max(s, axis=1, keepdims=True), (tq, LANES))
            if not first:
                m_prev = m_ref[h]
                m_new = jnp.maximum(m_prev, m_new)
                alpha = jnp.exp2((m_prev - m_new) * coef)
            p = jnp.concatenate(
                [jnp.exp2((s[:, j * LANES:(j + 1) * LANES] - m_new) * coef) for j in range(tq // LANES)],
                axis=1).astype(BF16)
            v1 = jnp.concatenate([v_ref[rows, h * dv:(h + 1) * dv], ones_v], axis=1)
            pv = jnp.dot(p, v1, preferred_element_type=F32)
            if first:
                acc_ref[:, h * dv:(h + 1) * dv] = pv[:, :dv]
                l_ref[h] = pv[:, dv:]
            else:
                acc_ref[:, h * dv:(h + 1) * dv] = alpha * acc_ref[:, h * dv:(h + 1) * dv] + pv[:, :dv]
                l_ref[h] = alpha * l_ref[h] + pv[:, dv:]
            m_ref[h] = m_new

    accumulate(pl.multiple_of(qi * tq, tq), rel_ref[...] >= 0, True)

    def below_diagonal_pair(pi, carry):
        k0 = pl.multiple_of(pi * (2 * tq), 2 * tq)
        accumulate(k0, None, False)
        accumulate(pl.multiple_of(k0 + tq, tq), None, False)
        return carry

    lax.fori_loop(0, qi // 2, below_diagonal_pair, 0)

    @pl.when(qi % 2 == 1)
    def _():
        accumulate(pl.multiple_of((qi - 1) * tq, tq), None, False)

    for h in range(heads):
        o_ref[:, h * dv:(h + 1) * dv] = (acc_ref[:, h * dv:(h + 1) * dv] / l_ref[h]).astype(o_ref.dtype)


def _attention(q_arr, q_cb, k_arr, k_cb, v_arr, v_cb, rel, kbias, *, batch, seq, heads, dq, dv, scale, name):
    assert dv == LANES and heads % ATTN_HEAD_GROUPS == 0
    tq = rel.shape[0]
    nq = seq // tq
    ng = ATTN_HEAD_GROUPS
    hg = heads // ng
    has_bias = kbias is not None
    in_specs = [pl.BlockSpec((tq, hg * dq), lambda b, g, qi: (b * nq + qi, q_cb * ng + g)),
                pl.BlockSpec((seq, hg * dq), lambda b, g, qi: (b, k_cb * ng + g)),
                pl.BlockSpec((seq, hg * dv), lambda b, g, qi: (b, v_cb * ng + g)),
                pl.BlockSpec((tq, tq), lambda b, g, qi: (0, 0))]
    args = [q_arr, k_arr, v_arr, rel]
    if has_bias:
        in_specs.append(pl.BlockSpec((seq, hg * LANES), lambda b, g, qi: (b, g)))
        args.append(kbias)
    return pl.pallas_call(
        functools.partial(_attn_kernel, heads=hg, dq=dq, dv=dv, coef=scale * math.log2(math.e),
                          has_bias=has_bias, tq=tq),
        grid=(batch, ng, nq),
        in_specs=in_specs,
        out_specs=pl.BlockSpec((tq, hg * dv), lambda b, g, qi: (b * nq + qi, g)),
        out_shape=jax.ShapeDtypeStruct((batch * seq, heads * dv), BF16),
        scratch_shapes=[pltpu.VMEM((hg, tq, LANES), F32),
                        pltpu.VMEM((hg, tq, LANES), F32),
                        pltpu.VMEM((tq, hg * dv), F32)],
        compiler_params=_params("parallel", "parallel", "arbitrary"),
        name=name,
    )(*args)


def _retention_kernel(q_ref, k_ref, v_ref, g_ref, cos_ref, sin_ref, dec_ref, sin_state_ref,
                      cross_ref, o_ref, state_ref, *, block_decay):
    @pl.when(pl.program_id(1) == 0)
    def _():
        state_ref[...] = jnp.zeros_like(state_ref)

    cos = cos_ref[...]
    sin = sin_ref[...]

    def rope(x):
        return x * cos + pltpu.roll(x, RET_DK // 2, axis=1) * sin

    for h in range(RET_HEADS):
        qs = slice(h * RET_DK, (h + 1) * RET_DK)
        vs = slice(h * RET_DV, (h + 1) * RET_DV)
        q = rope(q_ref[:, qs].astype(F32)).astype(BF16)
        kf = rope(k_ref[:, qs].astype(F32)) * (RET_DK ** -0.5)
        v = v_ref[:, vs]
        scores = lax.dot_general(q, kf.astype(BF16), (((1,), (1,)), ((), ())),
                                 preferred_element_type=F32) * dec_ref[h]
        o = jnp.dot(scores.astype(BF16), v, preferred_element_type=F32)
        state = state_ref[h]
        cross = jnp.dot(q, state.astype(BF16), preferred_element_type=F32)
        cd = cross_ref[h]
        o = o + cross * jnp.concatenate([cd, cd], axis=1)
        o = o * lax.rsqrt(jnp.mean(o * o, axis=-1, keepdims=True) + NORM_EPS)
        g = g_ref[:, vs].astype(F32)
        o_ref[:, vs] = (o * (g * jax.nn.sigmoid(g))).astype(o_ref.dtype)
        ks = (kf * sin_state_ref[h]).astype(BF16)
        kv = lax.dot_general(ks, v, (((0,), (0,)), ((), ())), preferred_element_type=F32)
        state_ref[h] = block_decay[h] * state + kv


def _retention(p, cos, sin, dec, sin_state, cross, block_decay, batch, seq):
    blk = dec.shape[1]
    nb = seq // blk
    row = lambda b, n: (b * nb + n, 0)
    const3 = lambda b, n: (0, 0, 0)
    return pl.pallas_call(
        functools.partial(_retention_kernel, block_decay=block_decay),
        grid=(batch, nb),
        in_specs=[pl.BlockSpec((blk, RET_QK_W), lambda b, n: (b * nb + n, OFF_RQ // RET_QK_W)),
                  pl.BlockSpec((blk, RET_QK_W), lambda b, n: (b * nb + n, OFF_RK // RET_QK_W)),
                  pl.BlockSpec((blk, RET_V_W), lambda b, n: (b * nb + n, OFF_RV // RET_V_W)),
                  pl.BlockSpec((blk, RET_V_W), lambda b, n: (b * nb + n, OFF_RG // RET_V_W)),
                  pl.BlockSpec((blk, RET_DK), lambda b, n: (n, 0)),
                  pl.BlockSpec((blk, RET_DK), lambda b, n: (n, 0)),
                  pl.BlockSpec(dec.shape, const3),
                  pl.BlockSpec(sin_state.shape, const3),
                  pl.BlockSpec(cross.shape, const3)],
        out_specs=pl.BlockSpec((blk, RET_V_W), row),
        out_shape=jax.ShapeDtypeStruct((batch * seq, RET_V_W), BF16),
        scratch_shapes=[pltpu.VMEM((RET_HEADS, RET_DK, RET_DV), F32)],
        compiler_params=_params("parallel", "arbitrary"),
        name="retention",
    )(p, p, p, p, cos, sin, dec, sin_state, cross)


def _gated_merge_kernel(h_ref, a_ref, b_ref, c_ref, wg0_ref, wg1_ref, wg2_ref,
                        wa_ref, wb_ref, wc_ref, o_ref):
    def branch(wg_ref, br_ref, w_ref):
        gate = jax.nn.sigmoid(_dot_nt(h_ref[...], wg_ref[...]))
        return gate * jnp.dot(br_ref[...], w_ref[...], preferred_element_type=F32)

    m = branch(wg0_ref, a_ref, wa_ref)
    m = m + branch(wg1_ref, b_ref, wb_ref)
    m = m + branch(wg2_ref, c_ref, wc_ref)
    o_ref[...] = m.astype(o_ref.dtype)


def _gated_merge(h, a, bm, c, w_gates, wa, wb, wc, layer):
    t, d = h.shape
    tm = _tile(t, 1024, 8)
    tn = _tile(d, 512, LANES)
    nj = d // tn
    act = lambda arr: pl.BlockSpec((tm, arr.shape[1]), lambda i, j: (i, 0))
    wgt = lambda w: _layer_spec(w, layer, (w.shape[1], tn), lambda i, j: (0, j))
    gate = lambda n: _layer_spec(w_gates, layer, (tn, d), lambda i, j: (n * nj + j, 0))
    return pl.pallas_call(
        _gated_merge_kernel,
        grid=(t // tm, nj),
        in_specs=[act(h), act(a), act(bm), act(c), gate(0), gate(1), gate(2), wgt(wa), wgt(wb), wgt(wc)],
        out_specs=pl.BlockSpec((tm, tn), lambda i, j: (i, j)),
        out_shape=jax.ShapeDtypeStruct((t, d), BF16),
        compiler_params=_params("parallel", "arbitrary"),
        name="gated_merge",
    )(h, a, bm, c, w_gates, w_gates, w_gates, wa, wb, wc)


def _mm_res_kernel(a_ref, w_ref, r_ref, o_ref):
    o_ref[...] = r_ref[...] + jnp.dot(a_ref[...], w_ref[...], preferred_element_type=F32)


def _matmul_residual(a, w, r, layer, tm_pref, tn_pref, name):
    t, k = a.shape
    n = w.shape[2]
    tm = _tile(t, tm_pref, 8)
    tn = _tile(n, tn_pref, LANES)
    return pl.pallas_call(
        _mm_res_kernel,
        grid=(n // tn, t // tm),
        in_specs=[pl.BlockSpec((tm, k), lambda j, i: (i, 0)),
                  _layer_spec(w, layer, (k, tn), lambda j, i: (0, j)),
                  pl.BlockSpec((tm, tn), lambda j, i: (i, j))],
        out_specs=pl.BlockSpec((tm, tn), lambda j, i: (i, j)),
        out_shape=jax.ShapeDtypeStruct((t, n), F32),
        compiler_params=_params("parallel", "arbitrary"),
        name=name,
    )(a, w, r)


def _out_proj_kernel(a_ref, w_ref, r_ref, g_ref, o_ref, h_ref):
    y = r_ref[...] + jnp.dot(a_ref[...], w_ref[...], preferred_element_type=F32)
    o_ref[...] = y
    h_ref[...] = _rms(y, g_ref[...]).astype(h_ref.dtype)


def _out_proj(a, w, r, g, layer):
    t, d = r.shape
    tm = _tile(t, 512, 8)
    row = pl.BlockSpec((tm, d), lambda i: (i, 0))
    return pl.pallas_call(
        _out_proj_kernel,
        grid=(t // tm,),
        in_specs=[pl.BlockSpec((tm, a.shape[1]), lambda i: (i, 0)), _layer_full(w, layer), row,
                  _layer_full(g, layer)],
        out_specs=[row, row],
        out_shape=[jax.ShapeDtypeStruct((t, d), F32), jax.ShapeDtypeStruct((t, d), BF16)],
        compiler_params=_params("parallel"),
        name="out_proj",
    )(a, w, r, g)


def _ffn_up_kernel(h_ref, wu_ref, wg_ref, cw_ref, cb_ref, o_ref, tail_ref, *, seq):
    tm = h_ref.shape[0]

    @pl.when((pl.program_id(1) * tm) % seq == 0)
    def _():
        tail_ref[...] = jnp.zeros_like(tail_ref)

    u = jnp.dot(h_ref[...], wu_ref[...], preferred_element_type=F32)
    gate = jnp.dot(h_ref[...], wg_ref[...], preferred_element_type=F32)

    def conv_act(v, g):
        conv = (cb_ref[...] + cw_ref[2:3, :] * v
                + cw_ref[1:2, :] * pltpu.roll(v, 1, axis=0)
                + cw_ref[0:1, :] * pltpu.roll(v, 2, axis=0))
        return jax.nn.gelu(conv) * g

    nt = tail_ref.shape[0]
    o_ref[...] = conv_act(u, gate).astype(o_ref.dtype)
    head = jnp.concatenate([tail_ref[...], u[:nt, :]], axis=0)
    o_ref[0:nt, :] = conv_act(head, jnp.concatenate([gate[:nt, :], gate[:nt, :]], axis=0))[nt:, :].astype(
        o_ref.dtype)
    tail_ref[...] = u[tm - nt:, :]


def _ffn_up(h, wu, wg, cw, cb, layer, seq):
    t, d = h.shape
    f = wu.shape[2]
    tm = _tile(seq, FFN_TOKEN_TILE, 8)
    tf = _tile(f, 512, LANES)
    col = lambda j, i: (0, j)
    return pl.pallas_call(
        functools.partial(_ffn_up_kernel, seq=seq),
        grid=(f // tf, t // tm),
        in_specs=[pl.BlockSpec((tm, d), lambda j, i: (i, 0)),
                  _layer_spec(wu, layer, (d, tf), col),
                  _layer_spec(wg, layer, (d, tf), col),
                  _layer_spec(cw, layer, (8, tf), col),
                  _layer_spec(cb, layer, (1, tf), col)],
        out_specs=pl.BlockSpec((tm, tf), lambda j, i: (i, j)),
        out_shape=jax.ShapeDtypeStruct((t, f), BF16),
        scratch_shapes=[pltpu.VMEM((16, tf), F32)],
        compiler_params=_params("arbitrary", "arbitrary"),
        name="ffn_up_conv_gate",
    )(h, wu, wg, cw, cb)


def _final_norm_kernel(x_ref, g_ref, o_ref):
    o_ref[...] = _rms(x_ref[...], g_ref[...])


def _final_norm(x, g):
    t, d = x.shape
    tm = _tile(t, 512, 8)
    return pl.pallas_call(
        _final_norm_kernel,
        grid=(t // tm,),
        in_specs=[pl.BlockSpec((tm, d), lambda i: (i, 0)), pl.BlockSpec((1, d), lambda i: (0, 0))],
        out_specs=pl.BlockSpec((tm, d), lambda i: (i, 0)),
        out_shape=jax.ShapeDtypeStruct((t, d), F32),
        compiler_params=_params("parallel"),
        name="final_norm",
    )(x, g)


def _rope_tables(seq, d):
    pos = jnp.arange(seq, dtype=F32)
    inv_freq = ROPE_THETA ** (-jnp.arange(0, d, 2, dtype=F32) / d)
    ang = pos[:, None] * inv_freq[None, :]
    return jnp.cos(ang), jnp.sin(ang)


def _retention_tables(blk):
    log_gamma = jnp.log(1.0 - 2.0 ** (-5.0 - jnp.arange(RET_HEADS, dtype=F32)))
    idx = jnp.arange(blk, dtype=F32)
    chunk = jnp.arange(blk) // CHUNK
    dist = jnp.abs(idx[:, None] - idx[None, :])
    dec = jnp.where(chunk[None, :] <= chunk[:, None],
                    jnp.exp(log_gamma[:, None, None] * dist[None]), 0.0)
    sin_state = jnp.exp(log_gamma[:, None] * (blk - 1 - idx)[None, :])
    cross = jnp.exp(log_gamma[:, None] * (idx + 1.0)[None, :])
    bcast = lambda a: jnp.broadcast_to(a[:, :, None], (RET_HEADS, blk, RET_DK))
    block_decay = tuple(math.exp(math.log(1.0 - 2.0 ** (-5.0 - h)) * blk) for h in range(RET_HEADS))
    return dec, bcast(sin_state), bcast(cross), block_decay


def _rotate_half_cols(w):
    half = w.shape[-1] // 2
    return jnp.concatenate([-w[..., half:], w[..., :half]], axis=-1)


def kernel(x, norm1_g, w_in, mla_q_norm_g, mla_kv_norm_g, mla_w_uq, mla_w_ukv, fox_b_f,
           w_br_fox, w_br_mla, w_br_ret, w_out, norm2_g, ffn_w_up, ffn_w_gate,
           ffn_conv_w, ffn_conv_b, ffn_w_down, final_norm_g):
    batch, seq, d = x.shape
    depth = w_in.shape[0]
    t = batch * seq
    q_lora = mla_w_uq.shape[1]
    kv_lora = mla_w_ukv.shape[1]

    widths = (FOX_W, FOX_W, FOX_W, FOX_HEADS, q_lora, kv_lora, MLA_ROPE,
              RET_QK_W, RET_QK_W, RET_V_W, RET_V_W, N_BRANCH * d)
    offs = [0]
    for wd in widths:
        offs.append(offs[-1] + wd)
    run_t = lambda k0, k1: jnp.swapaxes(w_in[:, :, offs[k0]:offs[k1]], 1, 2).astype(BF16)
    zeros = lambda n: jnp.zeros((depth, n, d), BF16)
    small = run_t(3, 7)
    cut = [0, FOX_HEADS, FOX_HEADS + q_lora, FOX_HEADS + q_lora + kv_lora]
    ff_t, mq_t, mkv_t, rope_t = (small[:, a:b] for a, b in zip(cut, cut[1:] + [small.shape[1]]))
    mla_rows = jnp.concatenate(
        [mkv_t, mq_t, rope_t, ff_t, zeros(LANES - MLA_ROPE - FOX_HEADS),
         -rope_t[:, MLA_ROPE // 2:], rope_t[:, :MLA_ROPE // 2], zeros(LANES - MLA_ROPE)], axis=1)
    assert OFF_MKV + mla_rows.shape[1] == P_WIDTH and q_lora == 512 and kv_lora == 256
    w_p = jnp.zeros((depth, P_WIDTH, d), BF16)
    w_p = lax.dynamic_update_slice(w_p, run_t(7, 11), (0, OFF_RQ, 0))
    w_p = lax.dynamic_update_slice(w_p, run_t(0, 3), (0, OFF_FQ, 0))
    w_p = lax.dynamic_update_slice(w_p, mla_rows, (0, OFF_MKV, 0))
    w_gates = run_t(11, 12)
    b_ff = jnp.pad(fox_b_f, ((0, 0), (FF_LANE, LANES - FF_LANE - FOX_HEADS)))[:, None, :]

    uq = mla_w_uq.reshape(depth, q_lora, MLA_HEADS, MLA_NOPE + MLA_ROPE)
    uq_rope = uq[..., MLA_NOPE:]
    pad_q = jnp.zeros((depth, q_lora, MLA_HEADS, MLA_QK_PAD - MLA_NOPE - MLA_ROPE), uq.dtype)
    w_q = jnp.concatenate([uq, pad_q], axis=-1).reshape(depth, q_lora, -1).astype(BF16)
    pad_r = jnp.zeros((depth, q_lora, MLA_HEADS, LANES - MLA_ROPE), uq.dtype)
    w_qr = jnp.concatenate([_rotate_half_cols(uq_rope), pad_r], axis=-1).reshape(depth, q_lora, -1).astype(BF16)
    ukv = mla_w_ukv.reshape(depth, kv_lora, MLA_HEADS, MLA_NOPE + MLA_V)
    w_k = ukv[..., :MLA_NOPE].reshape(depth, kv_lora, -1).astype(BF16)
    w_v = ukv[..., MLA_NOPE:].reshape(depth, kv_lora, -1).astype(BF16)

    w_bf, w_bm, w_br = w_br_fox.astype(BF16), w_br_mla.astype(BF16), w_br_ret.astype(BF16)
    w_o = w_out.astype(BF16)
    w_up, w_gate, w_down = ffn_w_up.astype(BF16), ffn_w_gate.astype(BF16), ffn_w_down.astype(BF16)
    conv_w = jnp.pad(ffn_conv_w, ((0, 0), (0, 8 - CONV_W), (0, 0)))
    conv_b = ffn_conv_b[:, None, :]
    g1, g2 = norm1_g[:, None, :], norm2_g[:, None, :]
    gq, gkv = mla_q_norm_g[:, None, :], mla_kv_norm_g[:, None, :]

    cos64, sin64 = _rope_tables(seq, MLA_ROPE)
    zpad = jnp.zeros((seq, LANES - MLA_ROPE), F32)
    mla_cos = jnp.concatenate([cos64, cos64, zpad], axis=1)
    mla_sin = jnp.concatenate([sin64, sin64, zpad], axis=1)
    cos128, sin128 = _rope_tables(seq, RET_DK)
    ret_cos = jnp.concatenate([cos128, cos128], axis=1)
    ret_sin = jnp.concatenate([-sin128, sin128], axis=1)
    ret_blk = _tile(seq, 512, CHUNK)
    dec, sin_state, cross, block_decay = _retention_tables(ret_blk)

    att_blk = _tile(seq, 512, CHUNK)
    row = jnp.arange(att_blk, dtype=jnp.int32)[:, None]
    col = jnp.arange(att_blk, dtype=jnp.int32)[None, :]
    rel_frame = row - col
    rel_chunk = (row // CHUNK) * CHUNK + (CHUNK - 1) - col

    xf = x.reshape(t, d)
    for i in range(depth):
        p, f_logit, h1 = _in_proj(xf, g1, w_p, i)
        kbias = _fox_decay(f_logit, b_ff, i, batch, seq)
        a = _attention(p, OFF_FQ // FOX_W, p, OFF_FK // FOX_W, p, OFF_FV // FOX_W, rel_frame, kbias,
                       batch=batch, seq=seq, heads=FOX_HEADS, dq=FOX_DH, dv=FOX_DH,
                       scale=FOX_DH ** -0.5, name="fox_attention")
        qm, km, vm = _mla_prep(p, gq, gkv, w_q, w_qr, w_k, w_v, mla_cos, mla_sin, i, seq)
        bm = _attention(qm, 0, km, 0, vm, 0, rel_chunk, None,
                        batch=batch, seq=seq, heads=MLA_HEADS, dq=MLA_QK_PAD, dv=MLA_V,
                        scale=(MLA_NOPE + MLA_ROPE) ** -0.5, name="mla_attention")
        cr = _retention(p, ret_cos, ret_sin, dec, sin_state, cross, block_decay, batch, seq)
        merged = _gated_merge(h1, a, bm, cr, w_gates, w_bf, w_bm, w_br, i)
        xf, h2 = _out_proj(merged, w_o, xf, g2, i)
        act = _ffn_up(h2, w_up, w_gate, conv_w, conv_b, i, seq)
        xf = _matmul_residual(act, w_down, xf, i, 512, 1024, "ffn_down")
    return _final_norm(xf, final_norm_g[None, :]).reshape(batch, seq, d)
```

```python
import functools
import math

import jax
import jax.numpy as jnp
from jax import lax
from jax.experimental import pallas as pl
from jax.experimental.pallas import tpu as pltpu

F32 = jnp.float32
BF16 = jnp.bfloat16

NORM_EPS = 1e-6
ROPE_THETA = 10000.0
CHUNK = 64

FOX_HEADS, FOX_DH = 6, 128
MLA_HEADS, MLA_NOPE, MLA_ROPE, MLA_V = 6, 128, 64, 128
MLA_QK_PAD = 256
RET_HEADS, RET_DK, RET_DV = 4, 128, 256
N_BRANCH = 3
CONV_W = 3
LANES = 128
FFN_TOKEN_TILE = 1024
BIAS_TERMS = 3
ATTN_HEAD_GROUPS = 1

FOX_W = FOX_HEADS * FOX_DH
MLA_W = MLA_HEADS * MLA_V
RET_QK_W = RET_HEADS * RET_DK
RET_V_W = RET_HEADS * RET_DV

OFF_RQ = 0
OFF_RK = OFF_RQ + RET_QK_W
OFF_RV = OFF_RK + RET_QK_W
OFF_RG = OFF_RV + RET_V_W
OFF_FQ = OFF_RG + RET_V_W
OFF_FK = OFF_FQ + FOX_W
OFF_FV = OFF_FK + FOX_W
OFF_MKV = OFF_FV + FOX_W
OFF_MQ = OFF_MKV + 256
OFF_MKR = OFF_MQ + 512
FF_LANE = MLA_ROPE
OFF_MKRR = OFF_MKR + LANES
P_WIDTH = OFF_MKRR + LANES

VMEM_LIMIT_BYTES = 56 * 1024 * 1024


def _tile(n, pref, align):
    t = (min(pref, n) // align) * align
    while t > align and n % t:
        t -= align
    assert t >= align and n % t == 0, (n, pref, align)
    return t


def _params(*sem):
    return pltpu.CompilerParams(dimension_semantics=sem, vmem_limit_bytes=VMEM_LIMIT_BYTES)


def _layer_spec(arr, layer, block, index):
    return pl.BlockSpec((None,) + tuple(block), lambda *ids: (layer,) + tuple(index(*ids)))


def _layer_full(arr, layer):
    zeros = (0,) * (arr.ndim - 1)
    return pl.BlockSpec((None,) + arr.shape[1:], lambda *ids: (layer,) + zeros)


def _rms(x, g):
    return x * lax.rsqrt(jnp.mean(x * x, axis=-1, keepdims=True) + NORM_EPS) * g


def _norm_rows(x_ref, g_ref, h_ref, rows):
    rc = _tile(rows, 256, 8)

    def body(c, carry):
        r0 = pl.multiple_of(c * rc, rc)
        h_ref[pl.ds(r0, rc), :] = _rms(x_ref[pl.ds(r0, rc), :], g_ref[...]).astype(BF16)
        return carry

    lax.fori_loop(0, rows // rc, body, 0)


def _dot_nt(a, b):
    return lax.dot_general(a, b, (((1,), (1,)), ((), ())), preferred_element_type=F32)


def _in_proj_kernel(x_ref, g_ref, w_ref, o_ref, f_ref, h_ref, *, f_tile, f_col):
    j = pl.program_id(1)

    @pl.when(j == 0)
    def _():
        rows = x_ref.shape[0]
        rc = _tile(rows, 256, 8)
        for c in range(rows // rc):
            sl = slice(c * rc, (c + 1) * rc)
            hc = _rms(x_ref[sl, :], g_ref[...]).astype(BF16)
            h_ref[sl, :] = hc
            o_ref[sl, :] = _dot_nt(hc, w_ref[...]).astype(o_ref.dtype)

    @pl.when(j > 0)
    def _():
        y = _dot_nt(h_ref[...], w_ref[...])
        o_ref[...] = y.astype(o_ref.dtype)

        @pl.when(j == f_tile)
        def _():
            f_ref[...] = y[:, f_col:f_col + LANES]


def _in_proj(x, g, w, layer):
    t, d = x.shape
    n = w.shape[1]
    tm = _tile(t, 1024, 8)
    tn = _tile(n, 1280, LANES)
    return pl.pallas_call(
        functools.partial(_in_proj_kernel, f_tile=OFF_MKR // tn, f_col=OFF_MKR % tn),
        grid=(t // tm, n // tn),
        in_specs=[pl.BlockSpec((tm, d), lambda i, j: (i, 0)),
                  _layer_full(g, layer),
                  _layer_spec(w, layer, (tn, d), lambda i, j: (j, 0))],
        out_specs=[pl.BlockSpec((tm, tn), lambda i, j: (i, j)),
                   pl.BlockSpec((tm, LANES), lambda i, j: (i, 0)),
                   pl.BlockSpec((tm, d), lambda i, j: (i, 0))],
        out_shape=[jax.ShapeDtypeStruct((t, n), BF16), jax.ShapeDtypeStruct((t, LANES), F32),
                   jax.ShapeDtypeStruct((t, d), BF16)],
        compiler_params=_params("parallel", "arbitrary"),
        name="in_proj",
    )(x, g, w)


def _fox_decay_kernel(f_ref, b_ref, tri_ref, sel_ref, o_ref, carry_ref):
    @pl.when(pl.program_id(1) == 0)
    def _():
        carry_ref[...] = jnp.zeros_like(carry_ref)

    f = f_ref[...] + b_ref[...]
    lf = jnp.minimum(f, 0.0) - jnp.log1p(jnp.exp(-jnp.abs(f)))
    lane = lax.broadcasted_iota(jnp.int32, lf.shape, 1)
    hi = lf.astype(BF16).astype(F32)
    r1 = lf - hi
    mid = r1.astype(BF16).astype(F32)
    lo = r1 - mid
    packed = jnp.where(lane < FF_LANE + FOX_HEADS, hi,
                       jnp.where(lane < FF_LANE + 2 * FOX_HEADS, pltpu.roll(mid, FOX_HEADS, axis=1),
                                 pltpu.roll(lo, 2 * FOX_HEADS, axis=1)))
    cp = jnp.dot(tri_ref[...], packed.astype(BF16), preferred_element_type=F32)
    c = (cp + pltpu.roll(cp, LANES - FOX_HEADS, axis=1)
         + pltpu.roll(cp, LANES - 2 * FOX_HEADS, axis=1)) + carry_ref[0:1, :]
    carry_ref[...] = jnp.broadcast_to(c[-1:, :], carry_ref.shape)
    live = (lane >= FF_LANE) & (lane < FF_LANE + FOX_HEADS)
    rest = jnp.where(live, c * (-(FOX_DH ** 0.5)), 0.0)
    terms = []
    for _ in range(BIAS_TERMS):
        part = rest.astype(BF16)
        terms.append(part)
        rest = rest - part.astype(F32)
    o_ref[...] = jnp.dot(jnp.concatenate(terms, axis=1), sel_ref[...],
                         preferred_element_type=F32).astype(o_ref.dtype)


def _fox_decay(f, b_ff, layer, batch, seq):
    t = f.shape[0]
    ts = _tile(seq, 1024, 8)
    nb = seq // ts
    tri = (jnp.arange(ts)[:, None] >= jnp.arange(ts)[None, :]).astype(BF16)
    src = jnp.arange(BIAS_TERMS * LANES)[:, None]
    dst = jnp.arange(FOX_HEADS * LANES)[None, :]
    sel = ((src // LANES == dst % LANES) & (src % LANES == FF_LANE + dst // LANES)
           & (dst % LANES < BIAS_TERMS)).astype(BF16)
    return pl.pallas_call(
        _fox_decay_kernel,
        grid=(batch, nb),
        in_specs=[pl.BlockSpec((ts, LANES), lambda b, n: (b * nb + n, 0)),
                  _layer_full(b_ff, layer),
                  pl.BlockSpec((ts, ts), lambda b, n: (0, 0)),
                  pl.BlockSpec(sel.shape, lambda b, n: (0, 0))],
        out_specs=pl.BlockSpec((ts, FOX_HEADS * LANES), lambda b, n: (b * nb + n, 0)),
        out_shape=jax.ShapeDtypeStruct((t, FOX_HEADS * LANES), BF16),
        scratch_shapes=[pltpu.VMEM((8, LANES), F32)],
        compiler_params=_params("parallel", "arbitrary"),
        name="fox_decay_cumsum",
    )(f, b_ff, tri, sel)


def _mla_prep_kernel(mq_ref, mkv_ref, kr_ref, krr_ref, gq_ref, gkv_ref, wq_ref, wqr_ref,
                     wk_ref, wv_ref, cos_ref, sin_ref, q_ref, k_ref, v_ref):
    cos = cos_ref[...]
    sin = sin_ref[...]
    cqn = _rms(mq_ref[...].astype(F32), gq_ref[...]).astype(BF16)
    q = jnp.dot(cqn, wq_ref[...], preferred_element_type=F32)
    qrot = jnp.dot(cqn, wqr_ref[...], preferred_element_type=F32)
    ckvn = _rms(mkv_ref[...].astype(F32), gkv_ref[...]).astype(BF16)
    kn = jnp.dot(ckvn, wk_ref[...], preferred_element_type=F32)
    v_ref[...] = jnp.dot(ckvn, wv_ref[...], preferred_element_type=F32).astype(BF16)
    kr = (kr_ref[...].astype(F32) * cos + krr_ref[...].astype(F32) * sin).astype(BF16)
    for h in range(MLA_HEADS):
        lo = h * MLA_QK_PAD
        q_ref[:, lo:lo + LANES] = q[:, lo:lo + LANES].astype(BF16)
        q_ref[:, lo + LANES:lo + 2 * LANES] = (
            q[:, lo + LANES:lo + 2 * LANES] * cos + qrot[:, h * LANES:(h + 1) * LANES] * sin).astype(BF16)
        k_ref[:, lo:lo + LANES] = kn[:, h * LANES:(h + 1) * LANES].astype(BF16)
        k_ref[:, lo + LANES:lo + 2 * LANES] = kr


def _mla_prep(p, gq, gkv, wq, wqr, wk, wv, cos, sin, layer, seq):
    t = p.shape[0]
    tm = _tile(seq, 1024, 8)
    ns = seq // tm
    ql, kvl = wq.shape[1], wk.shape[1]
    row = lambda i: (i, 0)
    return pl.pallas_call(
        _mla_prep_kernel,
        grid=(t // tm,),
        in_specs=[pl.BlockSpec((tm, ql), lambda i: (i, OFF_MQ // ql)),
                  pl.BlockSpec((tm, kvl), lambda i: (i, OFF_MKV // kvl)),
                  pl.BlockSpec((tm, LANES), lambda i: (i, OFF_MKR // LANES)),
                  pl.BlockSpec((tm, LANES), lambda i: (i, OFF_MKRR // LANES)),
                  _layer_full(gq, layer), _layer_full(gkv, layer),
                  _layer_full(wq, layer), _layer_full(wqr, layer),
                  _layer_full(wk, layer), _layer_full(wv, layer),
                  pl.BlockSpec((tm, LANES), lambda i: (i % ns, 0)),
                  pl.BlockSpec((tm, LANES), lambda i: (i % ns, 0))],
        out_specs=[pl.BlockSpec((tm, MLA_HEADS * MLA_QK_PAD), row),
                   pl.BlockSpec((tm, MLA_HEADS * MLA_QK_PAD), row),
                   pl.BlockSpec((tm, MLA_W), row)],
        out_shape=[jax.ShapeDtypeStruct((t, MLA_HEADS * MLA_QK_PAD), BF16),
                   jax.ShapeDtypeStruct((t, MLA_HEADS * MLA_QK_PAD), BF16),
                   jax.ShapeDtypeStruct((t, MLA_W), BF16)],
        compiler_params=_params("parallel"),
        name="mla_prep",
    )(p, p, p, p, gq, gkv, wq, wqr, wk, wv, cos, sin)


def _attn_kernel(*refs, heads, dq, dv, coef, has_bias, tq):
    if has_bias:
        q_ref, k_ref, v_ref, rel_ref, kb_ref, o_ref, m_ref, l_ref, acc_ref = refs
    else:
        q_ref, k_ref, v_ref, rel_ref, o_ref, m_ref, l_ref, acc_ref = refs
    qi = pl.program_id(2)

    def accumulate(k0, visible, first):
        rows = pl.ds(k0, tq)
        ones_v = jnp.ones((tq, LANES), BF16)
        if has_bias:
            lane = lax.broadcasted_iota(jnp.int32, (tq, LANES), 1)
            ones_q = jnp.where(lane < BIAS_TERMS, 1.0, 0.0).astype(BF16)
        for h in range(heads):
            q = q_ref[:, h * dq:(h + 1) * dq]
            k = k_ref[rows, h * dq:(h + 1) * dq]
            if has_bias:
                q = jnp.concatenate([q, ones_q], axis=1)
                k = jnp.concatenate([k, kb_ref[rows, h * LANES:(h + 1) * LANES]], axis=1)
            s = lax.dot_general(q, k, (((1,), (1,)), ((), ())), preferred_element_type=F32)
            if visible is not None:
                s = jnp.where(visible, s, -jnp.inf)
            m_new = jnp.broadcast_to(jnp.max(s, axis=1, keepdims=True), (tq, LANES))
            if not first:
                m_prev = m_ref[h]
                m_new = jnp.maximum(m_prev, m_new)
                alpha = jnp.exp2((m_prev - m_new) * coef)
            p = jnp.concatenate(
                [jnp.exp2((s[:, j * LANES:(j + 1) * LANES] - m_new) * coef) for j in range(tq // LANES)],
                axis=1).astype(BF16)
            v1 = jnp.concatenate([v_ref[rows, h * dv:(h + 1) * dv], ones_v], axis=1)
            pv = jnp.dot(p, v1, preferred_element_type=F32)
            if first:
                acc_ref[:, h * dv:(h + 1) * dv] = pv[:, :dv]
                l_ref[h] = pv[:, dv:]
            else:
                acc_ref[:, h * dv:(h + 1) * dv] = alpha * acc_ref[:, h * dv:(h + 1) * dv] + pv[:, :dv]
                l_ref[h] = alpha * l_ref[h] + pv[:, dv:]
            m_ref[h] = m_new

    accumulate(pl.multiple_of(qi * tq, tq), rel_ref[...] >= 0, True)

    def below_diagonal_pair(pi, carry):
        k0 = pl.multiple_of(pi * (2 * tq), 2 * tq)
        accumulate(k0, None, False)
        accumulate(pl.multiple_of(k0 + tq, tq), None, False)
        return carry

    lax.fori_loop(0, qi // 2, below_diagonal_pair, 0)

    @pl.when(qi % 2 == 1)
    def _():
        accumulate(pl.multiple_of((qi - 1) * tq, tq), None, False)

    for h in range(heads):
        o_ref[:, h * dv:(h + 1) * dv] = (acc_ref[:, h * dv:(h + 1) * dv] / l_ref[h]).astype(o_ref.dtype)


def _attention(q_arr, q_cb, k_arr, k_cb, v_arr, v_cb, rel, kbias, *, batch, seq, heads, dq, dv, scale, name):
    assert dv == LANES and heads % ATTN_HEAD_GROUPS == 0
    tq = rel.shape[0]
    nq = seq // tq
    ng = ATTN_HEAD_GROUPS
    hg = heads // ng
    has_bias = kbias is not None
    in_specs = [pl.BlockSpec((tq, hg * dq), lambda b, g, qi: (b * nq + qi, q_cb * ng + g)),
                pl.BlockSpec((seq, hg * dq), lambda b, g, qi: (b, k_cb * ng + g)),
                pl.BlockSpec((seq, hg * dv), lambda b, g, qi: (b, v_cb * ng + g)),
                pl.BlockSpec((tq, tq), lambda b, g, qi: (0, 0))]
    args = [q_arr, k_arr, v_arr, rel]
    if has_bias:
        in_specs.append(pl.BlockSpec((seq, hg * LANES), lambda b, g, qi: (b, g)))
        args.append(kbias)
    return pl.pallas_call(
        functools.partial(_attn_kernel, heads=hg, dq=dq, dv=dv, coef=scale * math.log2(math.e),
                          has_bias=has_bias, tq=tq),
        grid=(batch, ng, nq),
        in_specs=in_specs,
        out_specs=pl.BlockSpec((tq, hg * dv), lambda b, g, qi: (b * nq + qi, g)),
        out_shape=jax.ShapeDtypeStruct((batch * seq, heads * dv), BF16),
        scratch_shapes=[pltpu.VMEM((hg, tq, LANES), F32),
                        pltpu.VMEM((hg, tq, LANES), F32),
                        pltpu.VMEM((tq, hg * dv), F32)],
        compiler_params=_params("parallel", "parallel", "arbitrary"),
        name=name,
    )(*args)


def _retention_kernel(q_ref, k_ref, v_ref, g_ref, cos_ref, sin_ref, dec_ref, sin_state_ref,
                      cross_ref, o_ref, state_ref, *, block_decay):
    @pl.when(pl.program_id(1) == 0)
    def _():
        state_ref[...] = jnp.zeros_like(state_ref)

    cos = cos_ref[...]
    sin = sin_ref[...]

    def rope(x):
        return x * cos + pltpu.roll(x, RET_DK // 2, axis=1) * sin

    for h in range(RET_HEADS):
        qs = slice(h * RET_DK, (h + 1) * RET_DK)
        vs = slice(h * RET_DV, (h + 1) * RET_DV)
        q = rope(q_ref[:, qs].astype(F32)).astype(BF16)
        kf = rope(k_ref[:, qs].astype(F32)) * (RET_DK ** -0.5)
        v = v_ref[:, vs]
        scores = lax.dot_general(q, kf.astype(BF16), (((1,), (1,)), ((), ())),
                                 preferred_element_type=F32) * dec_ref[h]
        o = jnp.dot(scores.astype(BF16), v, preferred_element_type=F32)
        state = state_ref[h]
        cross = jnp.dot(q, state.astype(BF16), preferred_element_type=F32)
        cd = cross_ref[h]
        o = o + cross * jnp.concatenate([cd, cd], axis=1)
        o = o * lax.rsqrt(jnp.mean(o * o, axis=-1, keepdims=True) + NORM_EPS)
        g = g_ref[:, vs].astype(F32)
        o_ref[:, vs] = (o * (g * jax.nn.sigmoid(g))).astype(o_ref.dtype)
        ks = (kf * sin_state_ref[h]).astype(BF16)
        kv = lax.dot_general(ks, v, (((0,), (0,)), ((), ())), preferred_element_type=F32)
        state_ref[h] = block_decay[h] * state + kv


def _retention(p, cos, sin, dec, sin_state, cross, block_decay, batch, seq):
    blk = dec.shape[1]
    nb = seq // blk
    row = lambda b, n: (b * nb + n, 0)
    const3 = lambda b, n: (0, 0, 0)
    return pl.pallas_call(
        functools.partial(_retention_kernel, block_decay=block_decay),
        grid=(batch, nb),
        in_specs=[pl.BlockSpec((blk, RET_QK_W), lambda b, n: (b * nb + n, OFF_RQ // RET_QK_W)),
                  pl.BlockSpec((blk, RET_QK_W), lambda b, n: (b * nb + n, OFF_RK // RET_QK_W)),
                  pl.BlockSpec((blk, RET_V_W), lambda b, n: (b * nb + n, OFF_RV // RET_V_W)),
                  pl.BlockSpec((blk, RET_V_W), lambda b, n: (b * nb + n, OFF_RG // RET_V_W)),
                  pl.BlockSpec((blk, RET_DK), lambda b, n: (n, 0)),
                  pl.BlockSpec((blk, RET_DK), lambda b, n: (n, 0)),
                  pl.BlockSpec(dec.shape, const3),
                  pl.BlockSpec(sin_state.shape, const3),
                  pl.BlockSpec(cross.shape, const3)],
        out_specs=pl.BlockSpec((blk, RET_V_W), row),
        out_shape=jax.ShapeDtypeStruct((batch * seq, RET_V_W), BF16),
        scratch_shapes=[pltpu.VMEM((RET_HEADS, RET_DK, RET_DV), F32)],
        compiler_params=_params("parallel", "arbitrary"),
        name="retention",
    )(p, p, p, p, cos, sin, dec, sin_state, cross)


def _gated_merge_kernel(h_ref, a_ref, b_ref, c_ref, wg0_ref, wg1_ref, wg2_ref,
                        wa_ref, wb_ref, wc_ref, o_ref):
    def branch(wg_ref, br_ref, w_ref):
        gate = jax.nn.sigmoid(_dot_nt(h_ref[...], wg_ref[...]))
        return gate * jnp.dot(br_ref[...], w_ref[...], preferred_element_type=F32)

    m = branch(wg0_ref, a_ref, wa_ref)
    m = m + branch(wg1_ref, b_ref, wb_ref)
    m = m + branch(wg2_ref, c_ref, wc_ref)
    o_ref[...] = m.astype(o_ref.dtype)


def _gated_merge(h, a, bm, c, w_gates, wa, wb, wc, layer):
    t, d = h.shape
    tm = _tile(t, 1024, 8)
    tn = _tile(d, 512, LANES)
    nj = d // tn
    act = lambda arr: pl.BlockSpec((tm, arr.shape[1]), lambda i, j: (i, 0))
    wgt = lambda w: _layer_spec(w, layer, (w.shape[1], tn), lambda i, j: (0, j))
    gate = lambda n: _layer_spec(w_gates, layer, (tn, d), lambda i, j: (n * nj + j, 0))
    return pl.pallas_call(
        _gated_merge_kernel,
        grid=(t // tm, nj),
        in_specs=[act(h), act(a), act(bm), act(c), gate(0), gate(1), gate(2), wgt(wa), wgt(wb), wgt(wc)],
        out_specs=pl.BlockSpec((tm, tn), lambda i, j: (i, j)),
        out_shape=jax.ShapeDtypeStruct((t, d), BF16),
        compiler_params=_params("parallel", "arbitrary"),
        name="gated_merge",
    )(h, a, bm, c, w_gates, w_gates, w_gates, wa, wb, wc)


def _mm_res_kernel(a_ref, w_ref, r_ref, o_ref):
    o_ref[...] = r_ref[...] + jnp.dot(a_ref[...], w_ref[...], preferred_element_type=F32)


def _matmul_residual(a, w, r, layer, tm_pref, tn_pref, name):
    t, k = a.shape
    n = w.shape[2]
    tm = _tile(t, tm_pref, 8)
    tn = _tile(n, tn_pref, LANES)
    return pl.pallas_call(
        _mm_res_kernel,
        grid=(n // tn, t // tm),
        in_specs=[pl.BlockSpec((tm, k), lambda j, i: (i, 0)),
                  _layer_spec(w, layer, (k, tn), lambda j, i: (0, j)),
                  pl.BlockSpec((tm, tn), lambda j, i: (i, j))],
        out_specs=pl.BlockSpec((tm, tn), lambda j, i: (i, j)),
        out_shape=jax.ShapeDtypeStruct((t, n), F32),
        compiler_params=_params("parallel", "arbitrary"),
        name=name,
    )(a, w, r)


def _out_proj_kernel(a_ref, w_ref, r_ref, g_ref, o_ref, h_ref):
    y = r_ref[...] + jnp.dot(a_ref[...], w_ref[...], preferred_element_type=F32)
    o_ref[...] = y
    h_ref[...] = _rms(y, g_ref[...]).astype(h_ref.dtype)


def _out_proj(a, w, r, g, layer):
    t, d = r.shape
    tm = _tile(t, 512, 8)
    row = pl.BlockSpec((tm, d), lambda i: (i, 0))
    return pl.pallas_call(
        _out_proj_kernel,
        grid=(t // tm,),
        in_specs=[pl.BlockSpec((tm, a.shape[1]), lambda i: (i, 0)), _layer_full(w, layer), row,
                  _layer_full(g, layer)],
        out_specs=[row, row],
        out_shape=[jax.ShapeDtypeStruct((t, d), F32), jax.ShapeDtypeStruct((t, d), BF16)],
        compiler_params=_params("parallel"),
        name="out_proj",
    )(a, w, r, g)


def _ffn_up_kernel(h_ref, wu_ref, wg_ref, cw_ref, cb_ref, o_ref, tail_ref, *, seq):
    tm = h_ref.shape[0]

    @pl.when((pl.program_id(1) * tm) % seq == 0)
    def _():
        tail_ref[...] = jnp.zeros_like(tail_ref)

    u = jnp.dot(h_ref[...], wu_ref[...], preferred_element_type=F32)
    gate = jnp.dot(h_ref[...], wg_ref[...], preferred_element_type=F32)

    def conv_act(v, g):
        conv = (cb_ref[...] + cw_ref[2:3, :] * v
                + cw_ref[1:2, :] * pltpu.roll(v, 1, axis=0)
                + cw_ref[0:1, :] * pltpu.roll(v, 2, axis=0))
        return jax.nn.gelu(conv) * g

    nt = tail_ref.shape[0]
    o_ref[...] = conv_act(u, gate).astype(o_ref.dtype)
    head = jnp.concatenate([tail_ref[...], u[:nt, :]], axis=0)
    o_ref[0:nt, :] = conv_act(head, jnp.concatenate([gate[:nt, :], gate[:nt, :]], axis=0))[nt:, :].astype(
        o_ref.dtype)
    tail_ref[...] = u[tm - nt:, :]


def _ffn_up(h, wu, wg, cw, cb, layer, seq):
    t, d = h.shape
    f = wu.shape[2]
    tm = _tile(seq, FFN_TOKEN_TILE, 8)
    tf = _tile(f, 512, LANES)
    col = lambda j, i: (0, j)
    return pl.pallas_call(
        functools.partial(_ffn_up_kernel, seq=seq),
        grid=(f // tf, t // tm),
        in_specs=[pl.BlockSpec((tm, d), lambda j, i: (i, 0)),
                  _layer_spec(wu, layer, (d, tf), col),
                  _layer_spec(wg, layer, (d, tf), col),
                  _layer_spec(cw, layer, (8, tf), col),
                  _layer_spec(cb, layer, (1, tf), col)],
        out_specs=pl.BlockSpec((tm, tf), lambda j, i: (i, j)),
        out_shape=jax.ShapeDtypeStruct((t, f), BF16),
        scratch_shapes=[pltpu.VMEM((16, tf), F32)],
        compiler_params=_params("arbitrary", "arbitrary"),
        name="ffn_up_conv_gate",
    )(h, wu, wg, cw, cb)


def _final_norm_kernel(x_ref, g_ref, o_ref):
    o_ref[...] = _rms(x_ref[...], g_ref[...])


def _final_norm(x, g):
    t, d = x.shape
    tm = _tile(t, 512, 8)
    return pl.pallas_call(
        _final_norm_kernel,
        grid=(t // tm,),
        in_specs=[pl.BlockSpec((tm, d), lambda i: (i, 0)), pl.BlockSpec((1, d), lambda i: (0, 0))],
        out_specs=pl.BlockSpec((tm, d), lambda i: (i, 0)),
        out_shape=jax.ShapeDtypeStruct((t, d), F32),
        compiler_params=_params("parallel"),
        name="final_norm",
    )(x, g)


def _rope_tables(seq, d):
    pos = jnp.arange(seq, dtype=F32)
    inv_freq = ROPE_THETA ** (-jnp.arange(0, d, 2, dtype=F32) / d)
    ang = pos[:, None] * inv_freq[None, :]
    return jnp.cos(ang), jnp.sin(ang)


def _retention_tables(blk):
    log_gamma = jnp.log(1.0 - 2.0 ** (-5.0 - jnp.arange(RET_HEADS, dtype=F32)))
    idx = jnp.arange(blk, dtype=F32)
    chunk = jnp.arange(blk) // CHUNK
    dist = jnp.abs(idx[:, None] - idx[None, :])
    dec = jnp.where(chunk[None, :] <= chunk[:, None],
                    jnp.exp(log_gamma[:, None, None] * dist[None]), 0.0)
    sin_state = jnp.exp(log_gamma[:, None] * (blk - 1 - idx)[None, :])
    cross = jnp.exp(log_gamma[:, None] * (idx + 1.0)[None, :])
    bcast = lambda a: jnp.broadcast_to(a[:, :, None], (RET_HEADS, blk, RET_DK))
    block_decay = tuple(math.exp(math.log(1.0 - 2.0 ** (-5.0 - h)) * blk) for h in range(RET_HEADS))
    return dec, bcast(sin_state), bcast(cross), block_decay


def _rotate_half_cols(w):
    half = w.shape[-1] // 2
    return jnp.concatenate([-w[..., half:], w[..., :half]], axis=-1)


def kernel(x, norm1_g, w_in, mla_q_norm_g, mla_kv_norm_g, mla_w_uq, mla_w_ukv, fox_b_f,
           w_br_fox, w_br_mla, w_br_ret, w_out, norm2_g, ffn_w_up, ffn_w_gate,
           ffn_conv_w, ffn_conv_b, ffn_w_down, final_norm_g):
    batch, seq, d = x.shape
    depth = w_in.shape[0]
    t = batch * seq
    q_lora = mla_w_uq.shape[1]
    kv_lora = mla_w_ukv.shape[1]

    widths = (FOX_W, FOX_W, FOX_W, FOX_HEADS, q_lora, kv_lora, MLA_ROPE,
              RET_QK_W, RET_QK_W, RET_V_W, RET_V_W, N_BRANCH * d)
    offs = [0]
    for wd in widths:
        offs.append(offs[-1] + wd)
    run_t = lambda k0, k1: jnp.swapaxes(w_in[:, :, offs[k0]:offs[k1]], 1, 2).astype(BF16)
    zeros = lambda n: jnp.zeros((depth, n, d), BF16)
    small = run_t(3, 7)
    cut = [0, FOX_HEADS, FOX_HEADS + q_lora, FOX_HEADS + q_lora + kv_lora]
    ff_t, mq_t, mkv_t, rope_t = (small[:, a:b] for a, b in zip(cut, cut[1:] + [small.shape[1]]))
    mla_rows = jnp.concatenate(
        [mkv_t, mq_t, rope_t, ff_t, zeros(LANES - MLA_ROPE - FOX_HEADS),
         -rope_t[:, MLA_ROPE // 2:], rope_t[:, :MLA_ROPE // 2], zeros(LANES - MLA_ROPE)], axis=1)
    assert OFF_MKV + mla_rows.shape[1] == P_WIDTH and q_lora == 512 and kv_lora == 256
    w_p = jnp.zeros((depth, P_WIDTH, d), BF16)
    w_p = lax.dynamic_update_slice(w_p, run_t(7, 11), (0, OFF_RQ, 0))
    w_p = lax.dynamic_update_slice(w_p, run_t(0, 3), (0, OFF_FQ, 0))
    w_p = lax.dynamic_update_slice(w_p, mla_rows, (0, OFF_MKV, 0))
    w_gates = run_t(11, 12)
    b_ff = jnp.pad(fox_b_f, ((0, 0), (FF_LANE, LANES - FF_LANE - FOX_HEADS)))[:, None, :]

    uq = mla_w_uq.reshape(depth, q_lora, MLA_HEADS, MLA_NOPE + MLA_ROPE)
    uq_rope = uq[..., MLA_NOPE:]
    pad_q = jnp.zeros((depth, q_lora, MLA_HEADS, MLA_QK_PAD - MLA_NOPE - MLA_ROPE), uq.dtype)
    w_q = jnp.concatenate([uq, pad_q], axis=-1).reshape(depth, q_lora, -1).astype(BF16)
    pad_r = jnp.zeros((depth, q_lora, MLA_HEADS, LANES - MLA_ROPE), uq.dtype)
    w_qr = jnp.concatenate([_rotate_half_cols(uq_rope), pad_r], axis=-1).reshape(depth, q_lora, -1).astype(BF16)
    ukv = mla_w_ukv.reshape(depth, kv_lora, MLA_HEADS, MLA_NOPE + MLA_V)
    w_k = ukv[..., :MLA_NOPE].reshape(depth, kv_lora, -1).astype(BF16)
    w_v = ukv[..., MLA_NOPE:].reshape(depth, kv_lora, -1).astype(BF16)

    w_bf, w_bm, w_br = w_br_fox.astype(BF16), w_br_mla.astype(BF16), w_br_ret.astype(BF16)
    w_o = w_out.astype(BF16)
    w_up, w_gate, w_down = ffn_w_up.astype(BF16), ffn_w_gate.astype(BF16), ffn_w_down.astype(BF16)
    conv_w = jnp.pad(ffn_conv_w, ((0, 0), (0, 8 - CONV_W), (0, 0)))
    conv_b = ffn_conv_b[:, None, :]
    g1, g2 = norm1_g[:, None, :], norm2_g[:, None, :]
    gq, gkv = mla_q_norm_g[:, None, :], mla_kv_norm_g[:, None, :]

    cos64, sin64 = _rope_tables(seq, MLA_ROPE)
    zpad = jnp.zeros((seq, LANES - MLA_ROPE), F32)
    mla_cos = jnp.concatenate([cos64, cos64, zpad], axis=1)
    mla_sin = jnp.concatenate([sin64, sin64, zpad], axis=1)
    cos128, sin128 = _rope_tables(seq, RET_DK)
    ret_cos = jnp.concatenate([cos128, cos128], axis=1)
    ret_sin = jnp.concatenate([-sin128, sin128], axis=1)
    ret_blk = _tile(seq, 512, CHUNK)
    dec, sin_state, cross, block_decay = _retention_tables(ret_blk)

    att_blk = _tile(seq, 512, CHUNK)
    row = jnp.arange(att_blk, dtype=jnp.int32)[:, None]
    col = jnp.arange(att_blk, dtype=jnp.int32)[None, :]
    rel_frame = row - col
    rel_chunk = (row // CHUNK) * CHUNK + (CHUNK - 1) - col

    xf = x.reshape(t, d)
    for i in range(depth):
        p, f_logit, h1 = _in_proj(xf, g1, w_p, i)
        kbias = _fox_decay(f_logit, b_ff, i, batch, seq)
        a = _attention(p, OFF_FQ // FOX_W, p, OFF_FK // FOX_W, p, OFF_FV // FOX_W, rel_frame, kbias,
                       batch=batch, seq=seq, heads=FOX_HEADS, dq=FOX_DH, dv=FOX_DH,
                       scale=FOX_DH ** -0.5, name="fox_attention")
        qm, km, vm = _mla_prep(p, gq, gkv, w_q, w_qr, w_k, w_v, mla_cos, mla_sin, i, seq)
        bm = _attention(qm, 0, km, 0, vm, 0, rel_chunk, None,
                        batch=batch, seq=seq, heads=MLA_HEADS, dq=MLA_QK_PAD, dv=MLA_V,
                        scale=(MLA_NOPE + MLA_ROPE) ** -0.5, name="mla_attention")
        cr = _retention(p, ret_cos, ret_sin, dec, sin_state, cross, block_decay, batch, seq)
        merged = _gated_merge(h1, a, bm, cr, w_gates, w_bf, w_bm, w_br, i)
        xf, h2 = _out_proj(merged, w_o, xf, g2, i)
        act = _ffn_up(h2, w_up, w_gate, conv_w, conv_b, i, seq)
        xf = _matmul_residual(act, w_down, xf, i, 512, 1024, "ffn_down")
    return _final_norm(xf, final_norm_g[None, :]).reshape(batch, seq, d)
```

```python
import functools
import math

import jax
import jax.numpy as jnp
from jax import lax
from jax.experimental import pallas as pl
from jax.experimental.pallas import tpu as pltpu

F32 = jnp.float32
BF16 = jnp.bfloat16

NORM_EPS = 1e-6
ROPE_THETA = 10000.0
CHUNK = 64

FOX_HEADS, FOX_DH = 6, 128
MLA_HEADS, MLA_NOPE, MLA_ROPE, MLA_V = 6, 128, 64, 128
MLA_QK_PAD = 256
RET_HEADS, RET_DK, RET_DV = 4, 128, 256
N_BRANCH = 3
CONV_W = 3
LANES = 128
FFN_TOKEN_TILE = 1024
BIAS_TERMS = 3
ATTN_HEAD_GROUPS = 1

FOX_W = FOX_HEADS * FOX_DH
MLA_W = MLA_HEADS * MLA_V
RET_QK_W = RET_HEADS * RET_DK
RET_V_W = RET_HEADS * RET_DV

OFF_RQ = 0
OFF_RK = OFF_RQ + RET_QK_W
OFF_RV = OFF_RK + RET_QK_W
OFF_RG = OFF_RV + RET_V_W
OFF_FQ = OFF_RG + RET_V_W
OFF_FK = OFF_FQ + FOX_W
OFF_FV = OFF_FK + FOX_W
OFF_MKV = OFF_FV + FOX_W
OFF_MQ = OFF_MKV + 256
OFF_MKR = OFF_MQ + 512
FF_LANE = MLA_ROPE
OFF_MKRR = OFF_MKR + LANES
P_WIDTH = OFF_MKRR + LANES

VMEM_LIMIT_BYTES = 56 * 1024 * 1024


def _tile(n, pref, align):
    t = (min(pref, n) // align) * align
    while t > align and n % t:
        t -= align
    assert t >= align and n % t == 0, (n, pref, align)
    return t


def _params(*sem):
    return pltpu.CompilerParams(dimension_semantics=sem, vmem_limit_bytes=VMEM_LIMIT_BYTES)


def _layer_spec(arr, layer, block, index):
    return pl.BlockSpec((None,) + tuple(block), lambda *ids: (layer,) + tuple(index(*ids)))


def _layer_full(arr, layer):
    zeros = (0,) * (arr.ndim - 1)
    return pl.BlockSpec((None,) + arr.shape[1:], lambda *ids: (layer,) + zeros)


def _rms(x, g):
    return x * lax.rsqrt(jnp.mean(x * x, axis=-1, keepdims=True) + NORM_EPS) * g


def _norm_rows(x_ref, g_ref, h_ref, rows):
    rc = _tile(rows, 256, 8)

    def body(c, carry):
        r0 = pl.multiple_of(c * rc, rc)
        h_ref[pl.ds(r0, rc), :] = _rms(x_ref[pl.ds(r0, rc), :], g_ref[...]).astype(BF16)
        return carry

    lax.fori_loop(0, rows // rc, body, 0)


def _dot_nt(a, b):
    return lax.dot_general(a, b, (((1,), (1,)), ((), ())), preferred_element_type=F32)


def _in_proj_kernel(x_ref, g_ref, w_ref, o_ref, f_ref, h_ref, *, f_tile, f_col):
    j = pl.program_id(1)

    @pl.when(j == 0)
    def _():
        rows = x_ref.shape[0]
        rc = _tile(rows, 256, 8)
        for c in range(rows // rc):
            sl = slice(c * rc, (c + 1) * rc)
            hc = _rms(x_ref[sl, :], g_ref[...]).astype(BF16)
            h_ref[sl, :] = hc
            o_ref[sl, :] = _dot_nt(hc, w_ref[...]).astype(o_ref.dtype)

    @pl.when(j > 0)
    def _():
        y = _dot_nt(h_ref[...], w_ref[...])
        o_ref[...] = y.astype(o_ref.dtype)

        @pl.when(j == f_tile)
        def _():
            f_ref[...] = y[:, f_col:f_col + LANES]


def _in_proj(x, g, w, layer):
    t, d = x.shape
    n = w.shape[1]
    tm = _tile(t, 1024, 8)
    tn = _tile(n, 1280, LANES)
    return pl.pallas_call(
        functools.partial(_in_proj_kernel, f_tile=OFF_MKR // tn, f_col=OFF_MKR % tn),
        grid=(t // tm, n // tn),
        in_specs=[pl.BlockSpec((tm, d), lambda i, j: (i, 0)),
                  _layer_full(g, layer),
                  _layer_spec(w, layer, (tn, d), lambda i, j: (j, 0))],
        out_specs=[pl.BlockSpec((tm, tn), lambda i, j: (i, j)),
                   pl.BlockSpec((tm, LANES), lambda i, j: (i, 0)),
                   pl.BlockSpec((tm, d), lambda i, j: (i, 0))],
        out_shape=[jax.ShapeDtypeStruct((t, n), BF16), jax.ShapeDtypeStruct((t, LANES), F32),
                   jax.ShapeDtypeStruct((t, d), BF16)],
        compiler_params=_params("parallel", "arbitrary"),
        name="in_proj",
    )(x, g, w)


def _fox_decay_kernel(f_ref, b_ref, tri_ref, sel_ref, o_ref, carry_ref):
    @pl.when(pl.program_id(1) == 0)
    def _():
        carry_ref[...] = jnp.zeros_like(carry_ref)

    f = f_ref[...] + b_ref[...]
    lf = jnp.minimum(f, 0.0) - jnp.log1p(jnp.exp(-jnp.abs(f)))
    lane = lax.broadcasted_iota(jnp.int32, lf.shape, 1)
    hi = lf.astype(BF16).astype(F32)
    r1 = lf - hi
    mid = r1.astype(BF16).astype(F32)
    lo = r1 - mid
    packed = jnp.where(lane < FF_LANE + FOX_HEADS, hi,
                       jnp.where(lane < FF_LANE + 2 * FOX_HEADS, pltpu.roll(mid, FOX_HEADS, axis=1),
                                 pltpu.roll(lo, 2 * FOX_HEADS, axis=1)))
    cp = jnp.dot(tri_ref[...], packed.astype(BF16), preferred_element_type=F32)
    c = (cp + pltpu.roll(cp, LANES - FOX_HEADS, axis=1)
         + pltpu.roll(cp, LANES - 2 * FOX_HEADS, axis=1)) + carry_ref[0:1, :]
    carry_ref[...] = jnp.broadcast_to(c[-1:, :], carry_ref.shape)
    live = (lane >= FF_LANE) & (lane < FF_LANE + FOX_HEADS)
    rest = jnp.where(live, c * (-(FOX_DH ** 0.5)), 0.0)
    terms = []
    for _ in range(BIAS_TERMS):
        part = rest.astype(BF16)
        terms.append(part)
        rest = rest - part.astype(F32)
    o_ref[...] = jnp.dot(jnp.concatenate(terms, axis=1), sel_ref[...],
                         preferred_element_type=F32).astype(o_ref.dtype)


def _fox_decay(f, b_ff, layer, batch, seq):
    t = f.shape[0]
    ts = _tile(seq, 1024, 8)
    nb = seq // ts
    tri = (jnp.arange(ts)[:, None] >= jnp.arange(ts)[None, :]).astype(BF16)
    src = jnp.arange(BIAS_TERMS * LANES)[:, None]
    dst = jnp.arange(FOX_HEADS * LANES)[None, :]
    sel = ((src // LANES == dst % LANES) & (src % LANES == FF_LANE + dst // LANES)
           & (dst % LANES < BIAS_TERMS)).astype(BF16)
    return pl.pallas_call(
        _fox_decay_kernel,
        grid=(batch, nb),
        in_specs=[pl.BlockSpec((ts, LANES), lambda b, n: (b * nb + n, 0)),
                  _layer_full(b_ff, layer),
                  pl.BlockSpec((ts, ts), lambda b, n: (0, 0)),
                  pl.BlockSpec(sel.shape, lambda b, n: (0, 0))],
        out_specs=pl.BlockSpec((ts, FOX_HEADS * LANES), lambda b, n: (b * nb + n, 0)),
        out_shape=jax.ShapeDtypeStruct((t, FOX_HEADS * LANES), BF16),
        scratch_shapes=[pltpu.VMEM((8, LANES), F32)],
        compiler_params=_params("parallel", "arbitrary"),
        name="fox_decay_cumsum",
    )(f, b_ff, tri, sel)


def _mla_prep_kernel(mq_ref, mkv_ref, kr_ref, krr_ref, gq_ref, gkv_ref, wq_ref, wqr_ref,
                     wk_ref, wv_ref, cos_ref, sin_ref, q_ref, k_ref, v_ref):
    cos = cos_ref[...]
    sin = sin_ref[...]
    cqn = _rms(mq_ref[...].astype(F32), gq_ref[...]).astype(BF16)
    q = jnp.dot(cqn, wq_ref[...], preferred_element_type=F32)
    qrot = jnp.dot(cqn, wqr_ref[...], preferred_element_type=F32)
    ckvn = _rms(mkv_ref[...].astype(F32), gkv_ref[...]).astype(BF16)
    kn = jnp.dot(ckvn, wk_ref[...], preferred_element_type=F32)
    v_ref[...] = jnp.dot(ckvn, wv_ref[...], preferred_element_type=F32).astype(BF16)
    kr = (kr_ref[...].astype(F32) * cos + krr_ref[...].astype(F32) * sin).astype(BF16)
    for h in range(MLA_HEADS):
        lo = h * MLA_QK_PAD
        q_ref[:, lo:lo + LANES] = q[:, lo:lo + LANES].astype(BF16)
        q_ref[:, lo + LANES:lo + 2 * LANES] = (
            q[:, lo + LANES:lo + 2 * LANES] * cos + qrot[:, h * LANES:(h + 1) * LANES] * sin).astype(BF16)
        k_ref[:, lo:lo + LANES] = kn[:, h * LANES:(h + 1) * LANES].astype(BF16)
        k_ref[:, lo + LANES:lo + 2 * LANES] = kr


def _mla_prep(p, gq, gkv, wq, wqr, wk, wv, cos, sin, layer, seq):
    t = p.shape[0]
    tm = _tile(seq, 1024, 8)
    ns = seq // tm
    ql, kvl = wq.shape[1], wk.shape[1]
    row = lambda i: (i, 0)
    return pl.pallas_call(
        _mla_prep_kernel,
        grid=(t // tm,),
        in_specs=[pl.BlockSpec((tm, ql), lambda i: (i, OFF_MQ // ql)),
                  pl.BlockSpec((tm, kvl), lambda i: (i, OFF_MKV // kvl)),
                  pl.BlockSpec((tm, LANES), lambda i: (i, OFF_MKR // LANES)),
                  pl.BlockSpec((tm, LANES), lambda i: (i, OFF_MKRR // LANES)),
                  _layer_full(gq, layer), _layer_full(gkv, layer),
                  _layer_full(wq, layer), _layer_full(wqr, layer),
                  _layer_full(wk, layer), _layer_full(wv, layer),
                  pl.BlockSpec((tm, LANES), lambda i: (i % ns, 0)),
                  pl.BlockSpec((tm, LANES), lambda i: (i % ns, 0))],
        out_specs=[pl.BlockSpec((tm, MLA_HEADS * MLA_QK_PAD), row),
                   pl.BlockSpec((tm, MLA_HEADS * MLA_QK_PAD), row),
                   pl.BlockSpec((tm, MLA_W), row)],
        out_shape=[jax.ShapeDtypeStruct((t, MLA_HEADS * MLA_QK_PAD), BF16),
                   jax.ShapeDtypeStruct((t, MLA_HEADS * MLA_QK_PAD), BF16),
                   jax.ShapeDtypeStruct((t, MLA_W), BF16)],
        compiler_params=_params("parallel"),
        name="mla_prep",
    )(p, p, p, p, gq, gkv, wq, wqr, wk, wv, cos, sin)


def _attn_kernel(*refs, heads, dq, dv, coef, has_bias, tq):
    if has_bias:
        q_ref, k_ref, v_ref, rel_ref, kb_ref, o_ref, m_ref, l_ref, acc_ref = refs
    else:
        q_ref, k_ref, v_ref, rel_ref, o_ref, m_ref, l_ref, acc_ref = refs
    qi = pl.program_id(2)

    def accumulate(k0, visible, first):
        rows = pl.ds(k0, tq)
        ones_v = jnp.ones((tq, LANES), BF16)
        if has_bias:
            lane = lax.broadcasted_iota(jnp.int32, (tq, LANES), 1)
            ones_q = jnp.where(lane < BIAS_TERMS, 1.0, 0.0).astype(BF16)
        for h in range(heads):
            q = q_ref[:, h * dq:(h + 1) * dq]
            k = k_ref[rows, h * dq:(h + 1) * dq]
            if has_bias:
                q = jnp.concatenate([q, ones_q], axis=1)
                k = jnp.concatenate([k, kb_ref[rows, h * LANES:(h + 1) * LANES]], axis=1)
            s = lax.dot_general(q, k, (((1,), (1,)), ((), ())), preferred_element_type=F32)
            if visible is not None:
                s = jnp.where(visible, s, -jnp.inf)
            m_new = jnp.broadcast_to(jnp.max(s, axis=1, keepdims=True), (tq, LANES))
            if not first:
                m_prev = m_ref[h]
                m_new = jnp.maximum(m_prev, m_new)
                alpha = jnp.exp2((m_prev - m_new) * coef)
            p = jnp.concatenate(
                [jnp.exp2((s[:, j * LANES:(j + 1) * LANES] - m_new) * coef) for j in range(tq // LANES)],
                axis=1).astype(BF16)
            v1 = jnp.concatenate([v_ref[rows, h * dv:(h + 1) * dv], ones_v], axis=1)
            pv = jnp.dot(p, v1, preferred_element_type=F32)
            if first:
                acc_ref[:, h * dv:(h + 1) * dv] = pv[:, :dv]
                l_ref[h] = pv[:, dv:]
            else:
                acc_ref[:, h * dv:(h + 1) * dv] = alpha * acc_ref[:, h * dv:(h + 1) * dv] + pv[:, :dv]
                l_ref[h] = alpha * l_ref[h] + pv[:, dv:]
            m_ref[h] = m_new

    def diagonal():
        accumulate(pl.multiple_of(qi * tq, tq), rel_ref[...] >= 0, True)

    @pl.when(qi % 2 == 0)
    def _():
        diagonal()

    @pl.when(qi % 2 == 1)
    def _():
        diagonal()
        accumulate(pl.multiple_of((qi - 1) * tq, tq), None, False)

    def below_diagonal_pair(pi, carry):
        k0 = pl.multiple_of(pi * (2 * tq), 2 * tq)
        accumulate(k0, None, False)
        accumulate(pl.multiple_of(k0 + tq, tq), None, False)
        return carry

    lax.fori_loop(0, qi // 2, below_diagonal_pair, 0)

    for h in range(heads):
        o_ref[:, h * dv:(h + 1) * dv] = (acc_ref[:, h * dv:(h + 1) * dv] / l_ref[h]).astype(o_ref.dtype)


def _attention(q_arr, q_cb, k_arr, k_cb, v_arr, v_cb, rel, kbias, *, batch, seq, heads, dq, dv, scale, name):
    assert dv == LANES and heads % ATTN_HEAD_GROUPS == 0
    tq = rel.shape[0]
    nq = seq // tq
    ng = ATTN_HEAD_GROUPS
    hg = heads // ng
    has_bias = kbias is not None
    in_specs = [pl.BlockSpec((tq, hg * dq), lambda b, g, qi: (b * nq + qi, q_cb * ng + g)),
                pl.BlockSpec((seq, hg * dq), lambda b, g, qi: (b, k_cb * ng + g)),
                pl.BlockSpec((seq, hg * dv), lambda b, g, qi: (b, v_cb * ng + g)),
                pl.BlockSpec((tq, tq), lambda b, g, qi: (0, 0))]
    args = [q_arr, k_arr, v_arr, rel]
    if has_bias:
        in_specs.append(pl.BlockSpec((seq, hg * LANES), lambda b, g, qi: (b, g)))
        args.append(kbias)
    return pl.pallas_call(
        functools.partial(_attn_kernel, heads=hg, dq=dq, dv=dv, coef=scale * math.log2(math.e),
                          has_bias=has_bias, tq=tq),
        grid=(batch, ng, nq),
        in_specs=in_specs,
        out_specs=pl.BlockSpec((tq, hg * dv), lambda b, g, qi: (b * nq + qi, g)),
        out_shape=jax.ShapeDtypeStruct((batch * seq, heads * dv), BF16),
        scratch_shapes=[pltpu.VMEM((hg, tq, LANES), F32),
                        pltpu.VMEM((hg, tq, LANES), F32),
                        pltpu.VMEM((tq, hg * dv), F32)],
        compiler_params=_params("parallel", "parallel", "arbitrary"),
        name=name,
    )(*args)


def _retention_kernel(q_ref, k_ref, v_ref, g_ref, cos_ref, sin_ref, dec_ref, sin_state_ref,
                      cross_ref, o_ref, state_ref, *, block_decay):
    @pl.when(pl.program_id(1) == 0)
    def _():
        state_ref[...] = jnp.zeros_like(state_ref)

    cos = cos_ref[...]
    sin = sin_ref[...]

    def rope(x):
        return x * cos + pltpu.roll(x, RET_DK // 2, axis=1) * sin

    for h in range(RET_HEADS):
        qs = slice(h * RET_DK, (h + 1) * RET_DK)
        vs = slice(h * RET_DV, (h + 1) * RET_DV)
        q = rope(q_ref[:, qs].astype(F32)).astype(BF16)
        kf = rope(k_ref[:, qs].astype(F32)) * (RET_DK ** -0.5)
        v = v_ref[:, vs]
        scores = lax.dot_general(q, kf.astype(BF16), (((1,), (1,)), ((), ())),
                                 preferred_element_type=F32) * dec_ref[h]
        o = jnp.dot(scores.astype(BF16), v, preferred_element_type=F32)
        state = state_ref[h]
        cross = jnp.dot(q, state.astype(BF16), preferred_element_type=F32)
        cd = cross_ref[h]
        o = o + cross * jnp.concatenate([cd, cd], axis=1)
        o = o * lax.rsqrt(jnp.mean(o * o, axis=-1, keepdims=True) + NORM_EPS)
        g = g_ref[:, vs].astype(F32)
        o_ref[:, vs] = (o * (g * jax.nn.sigmoid(g))).astype(o_ref.dtype)
        ks = (kf * sin_state_ref[h]).astype(BF16)
        kv = lax.dot_general(ks, v, (((0,), (0,)), ((), ())), preferred_element_type=F32)
        state_ref[h] = block_decay[h] * state + kv


def _retention(p, cos, sin, dec, sin_state, cross, block_decay, batch, seq):
    blk = dec.shape[1]
    nb = seq // blk
    row = lambda b, n: (b * nb + n, 0)
    const3 = lambda b, n: (0, 0, 0)
    return pl.pallas_call(
        functools.partial(_retention_kernel, block_decay=block_decay),
        grid=(batch, nb),
        in_specs=[pl.BlockSpec((blk, RET_QK_W), lambda b, n: (b * nb + n, OFF_RQ // RET_QK_W)),
                  pl.BlockSpec((blk, RET_QK_W), lambda b, n: (b * nb + n, OFF_RK // RET_QK_W)),
                  pl.BlockSpec((blk, RET_V_W), lambda b, n: (b * nb + n, OFF_RV // RET_V_W)),
                  pl.BlockSpec((blk, RET_V_W), lambda b, n: (b * nb + n, OFF_RG // RET_V_W)),
                  pl.BlockSpec((blk, RET_DK), lambda b, n: (n, 0)),
                  pl.BlockSpec((blk, RET_DK), lambda b, n: (n, 0)),
                  pl.BlockSpec(dec.shape, const3),
                  pl.BlockSpec(sin_state.shape, const3),
                  pl.BlockSpec(cross.shape, const3)],
        out_specs=pl.BlockSpec((blk, RET_V_W), row),
        out_shape=jax.ShapeDtypeStruct((batch * seq, RET_V_W), BF16),
        scratch_shapes=[pltpu.VMEM((RET_HEADS, RET_DK, RET_DV), F32)],
        compiler_params=_params("parallel", "arbitrary"),
        name="retention",
    )(p, p, p, p, cos, sin, dec, sin_state, cross)


def _gated_merge_kernel(h_ref, a_ref, b_ref, c_ref, wg0_ref, wg1_ref, wg2_ref,
                        wa_ref, wb_ref, wc_ref, o_ref):
    def branch(wg_ref, br_ref, w_ref):
        gate = jax.nn.sigmoid(_dot_nt(h_ref[...], wg_ref[...]))
        return gate * jnp.dot(br_ref[...], w_ref[...], preferred_element_type=F32)

    m = branch(wg0_ref, a_ref, wa_ref)
    m = m + branch(wg1_ref, b_ref, wb_ref)
    m = m + branch(wg2_ref, c_ref, wc_ref)
    o_ref[...] = m.astype(o_ref.dtype)


def _gated_merge(h, a, bm, c, w_gates, wa, wb, wc, layer):
    t, d = h.shape
    tm = _tile(t, 1024, 8)
    tn = _tile(d, 512, LANES)
    nj = d // tn
    act = lambda arr: pl.BlockSpec((tm, arr.shape[1]), lambda i, j: (i, 0))
    wgt = lambda w: _layer_spec(w, layer, (w.shape[1], tn), lambda i, j: (0, j))
    gate = lambda n: _layer_spec(w_gates, layer, (tn, d), lambda i, j: (n * nj + j, 0))
    return pl.pallas_call(
        _gated_merge_kernel,
        grid=(t // tm, nj),
        in_specs=[act(h), act(a), act(bm), act(c), gate(0), gate(1), gate(2), wgt(wa), wgt(wb), wgt(wc)],
        out_specs=pl.BlockSpec((tm, tn), lambda i, j: (i, j)),
        out_shape=jax.ShapeDtypeStruct((t, d), BF16),
        compiler_params=_params("parallel", "arbitrary"),
        name="gated_merge",
    )(h, a, bm, c, w_gates, w_gates, w_gates, wa, wb, wc)


def _mm_res_kernel(a_ref, w_ref, r_ref, o_ref):
    o_ref[...] = r_ref[...] + jnp.dot(a_ref[...], w_ref[...], preferred_element_type=F32)


def _matmul_residual(a, w, r, layer, tm_pref, tn_pref, name):
    t, k = a.shape
    n = w.shape[2]
    tm = _tile(t, tm_pref, 8)
    tn = _tile(n, tn_pref, LANES)
    return pl.pallas_call(
        _mm_res_kernel,
        grid=(n // tn, t // tm),
        in_specs=[pl.BlockSpec((tm, k), lambda j, i: (i, 0)),
                  _layer_spec(w, layer, (k, tn), lambda j, i: (0, j)),
                  pl.BlockSpec((tm, tn), lambda j, i: (i, j))],
        out_specs=pl.BlockSpec((tm, tn), lambda j, i: (i, j)),
        out_shape=jax.ShapeDtypeStruct((t, n), F32),
        compiler_params=_params("parallel", "arbitrary"),
        name=name,
    )(a, w, r)


def _out_proj_kernel(a_ref, w_ref, r_ref, g_ref, o_ref, h_ref):
    y = r_ref[...] + jnp.dot(a_ref[...], w_ref[...], preferred_element_type=F32)
    o_ref[...] = y
    h_ref[...] = _rms(y, g_ref[...]).astype(h_ref.dtype)


def _out_proj(a, w, r, g, layer):
    t, d = r.shape
    tm = _tile(t, 512, 8)
    row = pl.BlockSpec((tm, d), lambda i: (i, 0))
    return pl.pallas_call(
        _out_proj_kernel,
        grid=(t // tm,),
        in_specs=[pl.BlockSpec((tm, a.shape[1]), lambda i: (i, 0)), _layer_full(w, layer), row,
                  _layer_full(g, layer)],
        out_specs=[row, row],
        out_shape=[jax.ShapeDtypeStruct((t, d), F32), jax.ShapeDtypeStruct((t, d), BF16)],
        compiler_params=_params("parallel"),
        name="out_proj",
    )(a, w, r, g)


def _ffn_up_kernel(h_ref, wu_ref, wg_ref, cw_ref, cb_ref, o_ref, tail_ref, *, seq):
    tm = h_ref.shape[0]

    @pl.when((pl.program_id(1) * tm) % seq == 0)
    def _():
        tail_ref[...] = jnp.zeros_like(tail_ref)

    u = jnp.dot(h_ref[...], wu_ref[...], preferred_element_type=F32)
    gate = jnp.dot(h_ref[...], wg_ref[...], preferred_element_type=F32)

    def conv_act(v, g):
        conv = (cb_ref[...] + cw_ref[2:3, :] * v
                + cw_ref[1:2, :] * pltpu.roll(v, 1, axis=0)
                + cw_ref[0:1, :] * pltpu.roll(v, 2, axis=0))
        return jax.nn.gelu(conv) * g

    nt = tail_ref.shape[0]
    o_ref[...] = conv_act(u, gate).astype(o_ref.dtype)
    head = jnp.concatenate([tail_ref[...], u[:nt, :]], axis=0)
    o_ref[0:nt, :] = conv_act(head, jnp.concatenate([gate[:nt, :], gate[:nt, :]], axis=0))[nt:, :].astype(
        o_ref.dtype)
    tail_ref[...] = u[tm - nt:, :]


def _ffn_up(h, wu, wg, cw, cb, layer, seq):
    t, d = h.shape
    f = wu.shape[2]
    tm = _tile(seq, FFN_TOKEN_TILE, 8)
    tf = _tile(f, 512, LANES)
    col = lambda j, i: (0, j)
    return pl.pallas_call(
        functools.partial(_ffn_up_kernel, seq=seq),
        grid=(f // tf, t // tm),
        in_specs=[pl.BlockSpec((tm, d), lambda j, i: (i, 0)),
                  _layer_spec(wu, layer, (d, tf), col),
                  _layer_spec(wg, layer, (d, tf), col),
                  _layer_spec(cw, layer, (8, tf), col),
                  _layer_spec(cb, layer, (1, tf), col)],
        out_specs=pl.BlockSpec((tm, tf), lambda j, i: (i, j)),
        out_shape=jax.ShapeDtypeStruct((t, f), BF16),
        scratch_shapes=[pltpu.VMEM((16, tf), F32)],
        compiler_params=_params("arbitrary", "arbitrary"),
        name="ffn_up_conv_gate",
    )(h, wu, wg, cw, cb)


def _final_norm_kernel(x_ref, g_ref, o_ref):
    o_ref[...] = _rms(x_ref[...], g_ref[...])


def _final_norm(x, g):
    t, d = x.shape
    tm = _tile(t, 512, 8)
    return pl.pallas_call(
        _final_norm_kernel,
        grid=(t // tm,),
        in_specs=[pl.BlockSpec((tm, d), lambda i: (i, 0)), pl.BlockSpec((1, d), lambda i: (0, 0))],
        out_specs=pl.BlockSpec((tm, d), lambda i: (i, 0)),
        out_shape=jax.ShapeDtypeStruct((t, d), F32),
        compiler_params=_params("parallel"),
        name="final_norm",
    )(x, g)


def _rope_tables(seq, d):
    pos = jnp.arange(seq, dtype=F32)
    inv_freq = ROPE_THETA ** (-jnp.arange(0, d, 2, dtype=F32) / d)
    ang = pos[:, None] * inv_freq[None, :]
    return jnp.cos(ang), jnp.sin(ang)


def _retention_tables(blk):
    log_gamma = jnp.log(1.0 - 2.0 ** (-5.0 - jnp.arange(RET_HEADS, dtype=F32)))
    idx = jnp.arange(blk, dtype=F32)
    chunk = jnp.arange(blk) // CHUNK
    dist = jnp.abs(idx[:, None] - idx[None, :])
    dec = jnp.where(chunk[None, :] <= chunk[:, None],
                    jnp.exp(log_gamma[:, None, None] * dist[None]), 0.0)
    sin_state = jnp.exp(log_gamma[:, None] * (blk - 1 - idx)[None, :])
    cross = jnp.exp(log_gamma[:, None] * (idx + 1.0)[None, :])
    bcast = lambda a: jnp.broadcast_to(a[:, :, None], (RET_HEADS, blk, RET_DK))
    block_decay = tuple(math.exp(math.log(1.0 - 2.0 ** (-5.0 - h)) * blk) for h in range(RET_HEADS))
    return dec, bcast(sin_state), bcast(cross), block_decay


def _rotate_half_cols(w):
    half = w.shape[-1] // 2
    return jnp.concatenate([-w[..., half:], w[..., :half]], axis=-1)


def kernel(x, norm1_g, w_in, mla_q_norm_g, mla_kv_norm_g, mla_w_uq, mla_w_ukv, fox_b_f,
           w_br_fox, w_br_mla, w_br_ret, w_out, norm2_g, ffn_w_up, ffn_w_gate,
           ffn_conv_w, ffn_conv_b, ffn_w_down, final_norm_g):
    batch, seq, d = x.shape
    depth = w_in.shape[0]
    t = batch * seq
    q_lora = mla_w_uq.shape[1]
    kv_lora = mla_w_ukv.shape[1]

    widths = (FOX_W, FOX_W, FOX_W, FOX_HEADS, q_lora, kv_lora, MLA_ROPE,
              RET_QK_W, RET_QK_W, RET_V_W, RET_V_W, N_BRANCH * d)
    offs = [0]
    for wd in widths:
        offs.append(offs[-1] + wd)
    run_t = lambda k0, k1: jnp.swapaxes(w_in[:, :, offs[k0]:offs[k1]], 1, 2).astype(BF16)
    zeros = lambda n: jnp.zeros((depth, n, d), BF16)
    small = run_t(3, 7)
    cut = [0, FOX_HEADS, FOX_HEADS + q_lora, FOX_HEADS + q_lora + kv_lora]
    ff_t, mq_t, mkv_t, rope_t = (small[:, a:b] for a, b in zip(cut, cut[1:] + [small.shape[1]]))
    mla_rows = jnp.concatenate(
        [mkv_t, mq_t, rope_t, ff_t, zeros(LANES - MLA_ROPE - FOX_HEADS),
         -rope_t[:, MLA_ROPE // 2:], rope_t[:, :MLA_ROPE // 2], zeros(LANES - MLA_ROPE)], axis=1)
    assert OFF_MKV + mla_rows.shape[1] == P_WIDTH and q_lora == 512 and kv_lora == 256
    w_p = jnp.zeros((depth, P_WIDTH, d), BF16)
    w_p = lax.dynamic_update_slice(w_p, run_t(7, 11), (0, OFF_RQ, 0))
    w_p = lax.dynamic_update_slice(w_p, run_t(0, 3), (0, OFF_FQ, 0))
    w_p = lax.dynamic_update_slice(w_p, mla_rows, (0, OFF_MKV, 0))
    w_gates = run_t(11, 12)
    b_ff = jnp.pad(fox_b_f, ((0, 0), (FF_LANE, LANES - FF_LANE - FOX_HEADS)))[:, None, :]

    uq = mla_w_uq.reshape(depth, q_lora, MLA_HEADS, MLA_NOPE + MLA_ROPE)
    uq_rope = uq[..., MLA_NOPE:]
    pad_q = jnp.zeros((depth, q_lora, MLA_HEADS, MLA_QK_PAD - MLA_NOPE - MLA_ROPE), uq.dtype)
    w_q = jnp.concatenate([uq, pad_q], axis=-1).reshape(depth, q_lora, -1).astype(BF16)
    pad_r = jnp.zeros((depth, q_lora, MLA_HEADS, LANES - MLA_ROPE), uq.dtype)
    w_qr = jnp.concatenate([_rotate_half_cols(uq_rope), pad_r], axis=-1).reshape(depth, q_lora, -1).astype(BF16)
    ukv = mla_w_ukv.reshape(depth, kv_lora, MLA_HEADS, MLA_NOPE + MLA_V)
    w_k = ukv[..., :MLA_NOPE].reshape(depth, kv_lora, -1).astype(BF16)
    w_v = ukv[..., MLA_NOPE:].reshape(depth, kv_lora, -1).astype(BF16)

    w_bf, w_bm, w_br = w_br_fox.astype(BF16), w_br_mla.astype(BF16), w_br_ret.astype(BF16)
    w_o = w_out.astype(BF16)
    w_up, w_gate, w_down = ffn_w_up.astype(BF16), ffn_w_gate.astype(BF16), ffn_w_down.astype(BF16)
    conv_w = jnp.pad(ffn_conv_w, ((0, 0), (0, 8 - CONV_W), (0, 0)))
    conv_b = ffn_conv_b[:, None, :]
    g1, g2 = norm1_g[:, None, :], norm2_g[:, None, :]
    gq, gkv = mla_q_norm_g[:, None, :], mla_kv_norm_g[:, None, :]

    cos64, sin64 = _rope_tables(seq, MLA_ROPE)
    zpad = jnp.zeros((seq, LANES - MLA_ROPE), F32)
    mla_cos = jnp.concatenate([cos64, cos64, zpad], axis=1)
    mla_sin = jnp.concatenate([sin64, sin64, zpad], axis=1)
    cos128, sin128 = _rope_tables(seq, RET_DK)
    ret_cos = jnp.concatenate([cos128, cos128], axis=1)
    ret_sin = jnp.concatenate([-sin128, sin128], axis=1)
    ret_blk = _tile(seq, 512, CHUNK)
    dec, sin_state, cross, block_decay = _retention_tables(ret_blk)

    att_blk = _tile(seq, 512, CHUNK)
    row = jnp.arange(att_blk, dtype=jnp.int32)[:, None]
    col = jnp.arange(att_blk, dtype=jnp.int32)[None, :]
    rel_frame = row - col
    rel_chunk = (row // CHUNK) * CHUNK + (CHUNK - 1) - col

    xf = x.reshape(t, d)
    for i in range(depth):
        p, f_logit, h1 = _in_proj(xf, g1, w_p, i)
        kbias = _fox_decay(f_logit, b_ff, i, batch, seq)
        a = _attention(p, OFF_FQ // FOX_W, p, OFF_FK // FOX_W, p, OFF_FV // FOX_W, rel_frame, kbias,
                       batch=batch, seq=seq, heads=FOX_HEADS, dq=FOX_DH, dv=FOX_DH,
                       scale=FOX_DH ** -0.5, name="fox_attention")
        qm, km, vm = _mla_prep(p, gq, gkv, w_q, w_qr, w_k, w_v, mla_cos, mla_sin, i, seq)
        bm = _attention(qm, 0, km, 0, vm, 0, rel_chunk, None,
                        batch=batch, seq=seq, heads=MLA_HEADS, dq=MLA_QK_PAD, dv=MLA_V,
                        scale=(MLA_NOPE + MLA_ROPE) ** -0.5, name="mla_attention")
        cr = _retention(p, ret_cos, ret_sin, dec, sin_state, cross, block_decay, batch, seq)
        merged = _gated_merge(h1, a, bm, cr, w_gates, w_bf, w_bm, w_br, i)
        xf, h2 = _out_proj(merged, w_o, xf, g2, i)
        act = _ffn_up(h2, w_up, w_gate, conv_w, conv_b, i, seq)
        xf = _matmul_residual(act, w_down, xf, i, 512, 1024, "ffn_down")
    return _final_norm(xf, final_norm_g[None, :]).reshape(batch, seq, d)
```

```python
import functools
import math

import jax
import jax.numpy as jnp
from jax import lax
from jax.experimental import pallas as pl
from jax.experimental.pallas import tpu as pltpu

F32 = jnp.float32
BF16 = jnp.bfloat16

NORM_EPS = 1e-6
ROPE_THETA = 10000.0
CHUNK = 64

FOX_HEADS, FOX_DH = 6, 128
MLA_HEADS, MLA_NOPE, MLA_ROPE, MLA_V = 6, 128, 64, 128
MLA_QK_PAD = 256
RET_HEADS, RET_DK, RET_DV = 4, 128, 256
N_BRANCH = 3
CONV_W = 3
LANES = 128
FFN_TOKEN_TILE = 1024
BIAS_TERMS = 3
ATTN_HEAD_GROUPS = 1

FOX_W = FOX_HEADS * FOX_DH
MLA_W = MLA_HEADS * MLA_V
RET_QK_W = RET_HEADS * RET_DK
RET_V_W = RET_HEADS * RET_DV

OFF_RQ = 0
OFF_RK = OFF_RQ + RET_QK_W
OFF_RV = OFF_RK + RET_QK_W
OFF_RG = OFF_RV + RET_V_W
OFF_FQ = OFF_RG + RET_V_W
OFF_FK = OFF_FQ + FOX_W
OFF_FV = OFF_FK + FOX_W
OFF_MKV = OFF_FV + FOX_W
OFF_MQ = OFF_MKV + 256
OFF_MKR = OFF_MQ + 512
FF_LANE = MLA_ROPE
OFF_MKRR = OFF_MKR + LANES
P_WIDTH = OFF_MKRR + LANES

VMEM_LIMIT_BYTES = 56 * 1024 * 1024


def _tile(n, pref, align):
    t = (min(pref, n) // align) * align
    while t > align and n % t:
        t -= align
    assert t >= align and n % t == 0, (n, pref, align)
    return t


def _params(*sem):
    return pltpu.CompilerParams(dimension_semantics=sem, vmem_limit_bytes=VMEM_LIMIT_BYTES)


def _layer_spec(arr, layer, block, index):
    return pl.BlockSpec((None,) + tuple(block), lambda *ids: (layer,) + tuple(index(*ids)))


def _layer_full(arr, layer):
    zeros = (0,) * (arr.ndim - 1)
    return pl.BlockSpec((None,) + arr.shape[1:], lambda *ids: (layer,) + zeros)


def _rms(x, g):
    return x * lax.rsqrt(jnp.mean(x * x, axis=-1, keepdims=True) + NORM_EPS) * g


def _norm_rows(x_ref, g_ref, h_ref, rows):
    rc = _tile(rows, 256, 8)

    def body(c, carry):
        r0 = pl.multiple_of(c * rc, rc)
        h_ref[pl.ds(r0, rc), :] = _rms(x_ref[pl.ds(r0, rc), :], g_ref[...]).astype(BF16)
        return carry

    lax.fori_loop(0, rows // rc, body, 0)


def _dot_nt(a, b):
    return lax.dot_general(a, b, (((1,), (1,)), ((), ())), preferred_element_type=F32)


def _in_proj_kernel(x_ref, g_ref, w_ref, o_ref, f_ref, h_ref, *, f_tile, f_col):
    j = pl.program_id(1)

    @pl.when(j == 0)
    def _():
        rows = x_ref.shape[0]
        rc = _tile(rows, 256, 8)
        for c in range(rows // rc):
            sl = slice(c * rc, (c + 1) * rc)
            hc = _rms(x_ref[sl, :], g_ref[...]).astype(BF16)
            h_ref[sl, :] = hc
            o_ref[sl, :] = _dot_nt(hc, w_ref[...]).astype(o_ref.dtype)

    @pl.when(j > 0)
    def _():
        y = _dot_nt(h_ref[...], w_ref[...])
        o_ref[...] = y.astype(o_ref.dtype)

        @pl.when(j == f_tile)
        def _():
            f_ref[...] = y[:, f_col:f_col + LANES]


def _in_proj(x, g, w, layer):
    t, d = x.shape
    n = w.shape[1]
    tm = _tile(t, 1024, 8)
    tn = _tile(n, 1280, LANES)
    return pl.pallas_call(
        functools.partial(_in_proj_kernel, f_tile=OFF_MKR // tn, f_col=OFF_MKR % tn),
        grid=(t // tm, n // tn),
        in_specs=[pl.BlockSpec((tm, d), lambda i, j: (i, 0)),
                  _layer_full(g, layer),
                  _layer_spec(w, layer, (tn, d), lambda i, j: (j, 0))],
        out_specs=[pl.BlockSpec((tm, tn), lambda i, j: (i, j)),
                   pl.BlockSpec((tm, LANES), lambda i, j: (i, 0)),
                   pl.BlockSpec((tm, d), lambda i, j: (i, 0))],
        out_shape=[jax.ShapeDtypeStruct((t, n), BF16), jax.ShapeDtypeStruct((t, LANES), F32),
                   jax.ShapeDtypeStruct((t, d), BF16)],
        compiler_params=_params("parallel", "arbitrary"),
        name="in_proj",
    )(x, g, w)


def _fox_decay_kernel(f_ref, b_ref, tri_ref, sel_ref, o_ref, carry_ref):
    @pl.when(pl.program_id(1) == 0)
    def _():
        carry_ref[...] = jnp.zeros_like(carry_ref)

    f = f_ref[...] + b_ref[...]
    lf = jnp.minimum(f, 0.0) - jnp.log1p(jnp.exp(-jnp.abs(f)))
    lane = lax.broadcasted_iota(jnp.int32, lf.shape, 1)
    hi = lf.astype(BF16).astype(F32)
    r1 = lf - hi
    mid = r1.astype(BF16).astype(F32)
    lo = r1 - mid
    packed = jnp.where(lane < FF_LANE + FOX_HEADS, hi,
                       jnp.where(lane < FF_LANE + 2 * FOX_HEADS, pltpu.roll(mid, FOX_HEADS, axis=1),
                                 pltpu.roll(lo, 2 * FOX_HEADS, axis=1)))
    cp = jnp.dot(tri_ref[...], packed.astype(BF16), preferred_element_type=F32)
    c = (cp + pltpu.roll(cp, LANES - FOX_HEADS, axis=1)
         + pltpu.roll(cp, LANES - 2 * FOX_HEADS, axis=1)) + carry_ref[0:1, :]
    carry_ref[...] = jnp.broadcast_to(c[-1:, :], carry_ref.shape)
    live = (lane >= FF_LANE) & (lane < FF_LANE + FOX_HEADS)
    rest = jnp.where(live, c * (-(FOX_DH ** 0.5)), 0.0)
    terms = []
    for _ in range(BIAS_TERMS):
        part = rest.astype(BF16)
        terms.append(part)
        rest = rest - part.astype(F32)
    o_ref[...] = jnp.dot(jnp.concatenate(terms, axis=1), sel_ref[...],
                         preferred_element_type=F32).astype(o_ref.dtype)


def _fox_decay(f, b_ff, layer, batch, seq):
    t = f.shape[0]
    ts = _tile(seq, 1024, 8)
    nb = seq // ts
    tri = (jnp.arange(ts)[:, None] >= jnp.arange(ts)[None, :]).astype(BF16)
    src = jnp.arange(BIAS_TERMS * LANES)[:, None]
    dst = jnp.arange(FOX_HEADS * LANES)[None, :]
    sel = ((src // LANES == dst % LANES) & (src % LANES == FF_LANE + dst // LANES)
           & (dst % LANES < BIAS_TERMS)).astype(BF16)
    return pl.pallas_call(
        _fox_decay_kernel,
        grid=(batch, nb),
        in_specs=[pl.BlockSpec((ts, LANES), lambda b, n: (b * nb + n, 0)),
                  _layer_full(b_ff, layer),
                  pl.BlockSpec((ts, ts), lambda b, n: (0, 0)),
                  pl.BlockSpec(sel.shape, lambda b, n: (0, 0))],
        out_specs=pl.BlockSpec((ts, FOX_HEADS * LANES), lambda b, n: (b * nb + n, 0)),
        out_shape=jax.ShapeDtypeStruct((t, FOX_HEADS * LANES), BF16),
        scratch_shapes=[pltpu.VMEM((8, LANES), F32)],
        compiler_params=_params("parallel", "arbitrary"),
        name="fox_decay_cumsum",
    )(f, b_ff, tri, sel)


def _mla_prep_kernel(mq_ref, mkv_ref, kr_ref, krr_ref, gq_ref, gkv_ref, wq_ref, wqr_ref,
                     wk_ref, wv_ref, cos_ref, sin_ref, q_ref, k_ref, v_ref):
    cos = cos_ref[...]
    sin = sin_ref[...]
    cqn = _rms(mq_ref[...].astype(F32), gq_ref[...]).astype(BF16)
    q = jnp.dot(cqn, wq_ref[...], preferred_element_type=F32)
    qrot = jnp.dot(cqn, wqr_ref[...], preferred_element_type=F32)
    ckvn = _rms(mkv_ref[...].astype(F32), gkv_ref[...]).astype(BF16)
    kn = jnp.dot(ckvn, wk_ref[...], preferred_element_type=F32)
    v_ref[...] = jnp.dot(ckvn, wv_ref[...], preferred_element_type=F32).astype(BF16)
    kr = (kr_ref[...].astype(F32) * cos + krr_ref[...].astype(F32) * sin).astype(BF16)
    for h in range(MLA_HEADS):
        lo = h * MLA_QK_PAD
        q_ref[:, lo:lo + LANES] = q[:, lo:lo + LANES].astype(BF16)
        q_ref[:, lo + LANES:lo + 2 * LANES] = (
            q[:, lo + LANES:lo + 2 * LANES] * cos + qrot[:, h * LANES:(h + 1) * LANES] * sin).astype(BF16)
        k_ref[:, lo:lo + LANES] = kn[:, h * LANES:(h + 1) * LANES].astype(BF16)
        k_ref[:, lo + LANES:lo + 2 * LANES] = kr


def _mla_prep(p, gq, gkv, wq, wqr, wk, wv, cos, sin, layer, seq):
    t = p.shape[0]
    tm = _tile(seq, 1024, 8)
    ns = seq // tm
    ql, kvl = wq.shape[1], wk.shape[1]
    row = lambda i: (i, 0)
    return pl.pallas_call(
        _mla_prep_kernel,
        grid=(t // tm,),
        in_specs=[pl.BlockSpec((tm, ql), lambda i: (i, OFF_MQ // ql)),
                  pl.BlockSpec((tm, kvl), lambda i: (i, OFF_MKV // kvl)),
                  pl.BlockSpec((tm, LANES), lambda i: (i, OFF_MKR // LANES)),
                  pl.BlockSpec((tm, LANES), lambda i: (i, OFF_MKRR // LANES)),
                  _layer_full(gq, layer), _layer_full(gkv, layer),
                  _layer_full(wq, layer), _layer_full(wqr, layer),
                  _layer_full(wk, layer), _layer_full(wv, layer),
                  pl.BlockSpec((tm, LANES), lambda i: (i % ns, 0)),
                  pl.BlockSpec((tm, LANES), lambda i: (i % ns, 0))],
        out_specs=[pl.BlockSpec((tm, MLA_HEADS * MLA_QK_PAD), row),
                   pl.BlockSpec((tm, MLA_HEADS * MLA_QK_PAD), row),
                   pl.BlockSpec((tm, MLA_W), row)],
        out_shape=[jax.ShapeDtypeStruct((t, MLA_HEADS * MLA_QK_PAD), BF16),
                   jax.ShapeDtypeStruct((t, MLA_HEADS * MLA_QK_PAD), BF16),
                   jax.ShapeDtypeStruct((t, MLA_W), BF16)],
        compiler_params=_params("parallel"),
        name="mla_prep",
    )(p, p, p, p, gq, gkv, wq, wqr, wk, wv, cos, sin)


def _attn_kernel(*refs, heads, dq, dv, coef, has_bias, tq):
    if has_bias:
        q_ref, k_ref, v_ref, rel_ref, kb_ref, o_ref, m_ref, l_ref, acc_ref = refs
    else:
        q_ref, k_ref, v_ref, rel_ref, o_ref, m_ref, l_ref, acc_ref = refs
    qi = pl.program_id(2)

    def accumulate(k0, visible, first):
        rows = pl.ds(k0, tq)
        ones_v = jnp.ones((tq, LANES), BF16)
        if has_bias:
            lane = lax.broadcasted_iota(jnp.int32, (tq, LANES), 1)
            ones_q = jnp.where(lane < BIAS_TERMS, 1.0, 0.0).astype(BF16)
        for h in range(heads):
            q = q_ref[:, h * dq:(h + 1) * dq]
            k = k_ref[rows, h * dq:(h + 1) * dq]
            if has_bias:
                q = jnp.concatenate([q, ones_q], axis=1)
                k = jnp.concatenate([k, kb_ref[rows, h * LANES:(h + 1) * LANES]], axis=1)
            s = lax.dot_general(q, k, (((1,), (1,)), ((), ())), preferred_element_type=F32)
            if visible is not None:
                s = jnp.where(visible, s, -jnp.inf)
            m_new = jnp.broadcast_to(jnp.max(s, axis=1, keepdims=True), (tq, LANES))
            if not first:
                m_prev = m_ref[h]
                m_new = jnp.maximum(m_prev, m_new)
                alpha = jnp.exp2((m_prev - m_new) * coef)
            p = jnp.concatenate(
                [jnp.exp2((s[:, j * LANES:(j + 1) * LANES] - m_new) * coef) for j in range(tq // LANES)],
                axis=1).astype(BF16)
            v1 = jnp.concatenate([v_ref[rows, h * dv:(h + 1) * dv], ones_v], axis=1)
            pv = jnp.dot(p, v1, preferred_element_type=F32)
            if first:
                acc_ref[:, h * dv:(h + 1) * dv] = pv[:, :dv]
                l_ref[h] = pv[:, dv:]
            else:
                acc_ref[:, h * dv:(h + 1) * dv] = alpha * acc_ref[:, h * dv:(h + 1) * dv] + pv[:, :dv]
                l_ref[h] = alpha * l_ref[h] + pv[:, dv:]
            m_ref[h] = m_new

    def diagonal(extra):
        accumulate(pl.multiple_of(qi * tq, tq), rel_ref[...] >= 0, True)
        for e in range(1, extra + 1):
            accumulate(pl.multiple_of((qi - e) * tq, tq), None, False)

    odd = qi % 2 == 1

    @pl.when(qi == 0)
    def _():
        diagonal(0)

    @pl.when(odd)
    def _():
        diagonal(1)

    @pl.when(jnp.logical_and(jnp.logical_not(odd), qi >= 2))
    def _():
        diagonal(2)

    def below_diagonal_pair(pi, carry):
        k0 = pl.multiple_of(pi * (2 * tq), 2 * tq)
        accumulate(k0, None, False)
        accumulate(pl.multiple_of(k0 + tq, tq), None, False)
        return carry

    lax.fori_loop(0, (qi - 1) // 2, below_diagonal_pair, 0)

    for h in range(heads):
        o_ref[:, h * dv:(h + 1) * dv] = (acc_ref[:, h * dv:(h + 1) * dv] / l_ref[h]).astype(o_ref.dtype)


def _attention(q_arr, q_cb, k_arr, k_cb, v_arr, v_cb, rel, kbias, *, batch, seq, heads, dq, dv, scale, name):
    assert dv == LANES and heads % ATTN_HEAD_GROUPS == 0
    tq = rel.shape[0]
    nq = seq // tq
    ng = ATTN_HEAD_GROUPS
    hg = heads // ng
    has_bias = kbias is not None
    in_specs = [pl.BlockSpec((tq, hg * dq), lambda b, g, qi: (b * nq + qi, q_cb * ng + g)),
                pl.BlockSpec((seq, hg * dq), lambda b, g, qi: (b, k_cb * ng + g)),
                pl.BlockSpec((seq, hg * dv), lambda b, g, qi: (b, v_cb * ng + g)),
                pl.BlockSpec((tq, tq), lambda b, g, qi: (0, 0))]
    args = [q_arr, k_arr, v_arr, rel]
    if has_bias:
        in_specs.append(pl.BlockSpec((seq, hg * LANES), lambda b, g, qi: (b, g)))
        args.append(kbias)
    return pl.pallas_call(
        functools.partial(_attn_kernel, heads=hg, dq=dq, dv=dv, coef=scale * math.log2(math.e),
                          has_bias=has_bias, tq=tq),
        grid=(batch, ng, nq),
        in_specs=in_specs,
        out_specs=pl.BlockSpec((tq, hg * dv), lambda b, g, qi: (b * nq + qi, g)),
        out_shape=jax.ShapeDtypeStruct((batch * seq, heads * dv), BF16),
        scratch_shapes=[pltpu.VMEM((hg, tq, LANES), F32),
                        pltpu.VMEM((hg, tq, LANES), F32),
                        pltpu.VMEM((tq, hg * dv), F32)],
        compiler_params=_params("parallel", "parallel", "arbitrary"),
        name=name,
    )(*args)


def _retention_kernel(q_ref, k_ref, v_ref, g_ref, cos_ref, sin_ref, dec_ref, sin_state_ref,
                      cross_ref, o_ref, state_ref, *, block_decay):
    @pl.when(pl.program_id(1) == 0)
    def _():
        state_ref[...] = jnp.zeros_like(state_ref)

    cos = cos_ref[...]
    sin = sin_ref[...]

    def rope(x):
        return x * cos + pltpu.roll(x, RET_DK // 2, axis=1) * sin

    for h in range(RET_HEADS):
        qs = slice(h * RET_DK, (h + 1) * RET_DK)
        vs = slice(h * RET_DV, (h + 1) * RET_DV)
        q = rope(q_ref[:, qs].astype(F32)).astype(BF16)
        kf = rope(k_ref[:, qs].astype(F32)) * (RET_DK ** -0.5)
        v = v_ref[:, vs]
        scores = lax.dot_general(q, kf.astype(BF16), (((1,), (1,)), ((), ())),
                                 preferred_element_type=F32) * dec_ref[h]
        o = jnp.dot(scores.astype(BF16), v, preferred_element_type=F32)
        state = state_ref[h]
        cross = jnp.dot(q, state.astype(BF16), preferred_element_type=F32)
        cd = cross_ref[h]
        o = o + cross * jnp.concatenate([cd, cd], axis=1)
        o = o * lax.rsqrt(jnp.mean(o * o, axis=-1, keepdims=True) + NORM_EPS)
        g = g_ref[:, vs].astype(F32)
        o_ref[:, vs] = (o * (g * jax.nn.sigmoid(g))).astype(o_ref.dtype)
        ks = (kf * sin_state_ref[h]).astype(BF16)
        kv = lax.dot_general(ks, v, (((0,), (0,)), ((), ())), preferred_element_type=F32)
        state_ref[h] = block_decay[h] * state + kv


def _retention(p, cos, sin, dec, sin_state, cross, block_decay, batch, seq):
    blk = dec.shape[1]
    nb = seq // blk
    row = lambda b, n: (b * nb + n, 0)
    const3 = lambda b, n: (0, 0, 0)
    return pl.pallas_call(
        functools.partial(_retention_kernel, block_decay=block_decay),
        grid=(batch, nb),
        in_specs=[pl.BlockSpec((blk, RET_QK_W), lambda b, n: (b * nb + n, OFF_RQ // RET_QK_W)),
                  pl.BlockSpec((blk, RET_QK_W), lambda b, n: (b * nb + n, OFF_RK // RET_QK_W)),
                  pl.BlockSpec((blk, RET_V_W), lambda b, n: (b * nb + n, OFF_RV // RET_V_W)),
                  pl.BlockSpec((blk, RET_V_W), lambda b, n: (b * nb + n, OFF_RG // RET_V_W)),
                  pl.BlockSpec((blk, RET_DK), lambda b, n: (n, 0)),
                  pl.BlockSpec((blk, RET_DK), lambda b, n: (n, 0)),
                  pl.BlockSpec(dec.shape, const3),
                  pl.BlockSpec(sin_state.shape, const3),
                  pl.BlockSpec(cross.shape, const3)],
        out_specs=pl.BlockSpec((blk, RET_V_W), row),
        out_shape=jax.ShapeDtypeStruct((batch * seq, RET_V_W), BF16),
        scratch_shapes=[pltpu.VMEM((RET_HEADS, RET_DK, RET_DV), F32)],
        compiler_params=_params("parallel", "arbitrary"),
        name="retention",
    )(p, p, p, p, cos, sin, dec, sin_state, cross)


def _gated_merge_kernel(h_ref, a_ref, b_ref, c_ref, wg0_ref, wg1_ref, wg2_ref,
                        wa_ref, wb_ref, wc_ref, o_ref):
    def branch(wg_ref, br_ref, w_ref):
        gate = jax.nn.sigmoid(_dot_nt(h_ref[...], wg_ref[...]))
        return gate * jnp.dot(br_ref[...], w_ref[...], preferred_element_type=F32)

    m = branch(wg0_ref, a_ref, wa_ref)
    m = m + branch(wg1_ref, b_ref, wb_ref)
    m = m + branch(wg2_ref, c_ref, wc_ref)
    o_ref[...] = m.astype(o_ref.dtype)


def _gated_merge(h, a, bm, c, w_gates, wa, wb, wc, layer):
    t, d = h.shape
    tm = _tile(t, 1024, 8)
    tn = _tile(d, 512, LANES)
    nj = d // tn
    act = lambda arr: pl.BlockSpec((tm, arr.shape[1]), lambda i, j: (i, 0))
    wgt = lambda w: _layer_spec(w, layer, (w.shape[1], tn), lambda i, j: (0, j))
    gate = lambda n: _layer_spec(w_gates, layer, (tn, d), lambda i, j: (n * nj + j, 0))
    return pl.pallas_call(
        _gated_merge_kernel,
        grid=(t // tm, nj),
        in_specs=[act(h), act(a), act(bm), act(c), gate(0), gate(1), gate(2), wgt(wa), wgt(wb), wgt(wc)],
        out_specs=pl.BlockSpec((tm, tn), lambda i, j: (i, j)),
        out_shape=jax.ShapeDtypeStruct((t, d), BF16),
        compiler_params=_params("parallel", "arbitrary"),
        name="gated_merge",
    )(h, a, bm, c, w_gates, w_gates, w_gates, wa, wb, wc)


def _mm_res_kernel(a_ref, w_ref, r_ref, o_ref):
    o_ref[...] = r_ref[...] + jnp.dot(a_ref[...], w_ref[...], preferred_element_type=F32)


def _matmul_residual(a, w, r, layer, tm_pref, tn_pref, name):
    t, k = a.shape
    n = w.shape[2]
    tm = _tile(t, tm_pref, 8)
    tn = _tile(n, tn_pref, LANES)
    return pl.pallas_call(
        _mm_res_kernel,
        grid=(n // tn, t // tm),
        in_specs=[pl.BlockSpec((tm, k), lambda j, i: (i, 0)),
                  _layer_spec(w, layer, (k, tn), lambda j, i: (0, j)),
                  pl.BlockSpec((tm, tn), lambda j, i: (i, j))],
        out_specs=pl.BlockSpec((tm, tn), lambda j, i: (i, j)),
        out_shape=jax.ShapeDtypeStruct((t, n), F32),
        compiler_params=_params("parallel", "arbitrary"),
        name=name,
    )(a, w, r)


def _out_proj_kernel(a_ref, w_ref, r_ref, g_ref, o_ref, h_ref):
    y = r_ref[...] + jnp.dot(a_ref[...], w_ref[...], preferred_element_type=F32)
    o_ref[...] = y
    h_ref[...] = _rms(y, g_ref[...]).astype(h_ref.dtype)


def _out_proj(a, w, r, g, layer):
    t, d = r.shape
    tm = _tile(t, 512, 8)
    row = pl.BlockSpec((tm, d), lambda i: (i, 0))
    return pl.pallas_call(
        _out_proj_kernel,
        grid=(t // tm,),
        in_specs=[pl.BlockSpec((tm, a.shape[1]), lambda i: (i, 0)), _layer_full(w, layer), row,
                  _layer_full(g, layer)],
        out_specs=[row, row],
        out_shape=[jax.ShapeDtypeStruct((t, d), F32), jax.ShapeDtypeStruct((t, d), BF16)],
        compiler_params=_params("parallel"),
        name="out_proj",
    )(a, w, r, g)


def _ffn_up_kernel(h_ref, wu_ref, wg_ref, cw_ref, cb_ref, o_ref, tail_ref, *, seq):
    tm = h_ref.shape[0]

    @pl.when((pl.program_id(1) * tm) % seq == 0)
    def _():
        tail_ref[...] = jnp.zeros_like(tail_ref)

    u = jnp.dot(h_ref[...], wu_ref[...], preferred_element_type=F32)
    gate = jnp.dot(h_ref[...], wg_ref[...], preferred_element_type=F32)

    def conv_act(v, g):
        conv = (cb_ref[...] + cw_ref[2:3, :] * v
                + cw_ref[1:2, :] * pltpu.roll(v, 1, axis=0)
                + cw_ref[0:1, :] * pltpu.roll(v, 2, axis=0))
        return jax.nn.gelu(conv) * g

    nt = tail_ref.shape[0]
    o_ref[...] = conv_act(u, gate).astype(o_ref.dtype)
    head = jnp.concatenate([tail_ref[...], u[:nt, :]], axis=0)
    o_ref[0:nt, :] = conv_act(head, jnp.concatenate([gate[:nt, :], gate[:nt, :]], axis=0))[nt:, :].astype(
        o_ref.dtype)
    tail_ref[...] = u[tm - nt:, :]


def _ffn_up(h, wu, wg, cw, cb, layer, seq):
    t, d = h.shape
    f = wu.shape[2]
    tm = _tile(seq, FFN_TOKEN_TILE, 8)
    tf = _tile(f, 512, LANES)
    col = lambda j, i: (0, j)
    return pl.pallas_call(
        functools.partial(_ffn_up_kernel, seq=seq),
        grid=(f // tf, t // tm),
        in_specs=[pl.BlockSpec((tm, d), lambda j, i: (i, 0)),
                  _layer_spec(wu, layer, (d, tf), col),
                  _layer_spec(wg, layer, (d, tf), col),
                  _layer_spec(cw, layer, (8, tf), col),
                  _layer_spec(cb, layer, (1, tf), col)],
        out_specs=pl.BlockSpec((tm, tf), lambda j, i: (i, j)),
        out_shape=jax.ShapeDtypeStruct((t, f), BF16),
        scratch_shapes=[pltpu.VMEM((16, tf), F32)],
        compiler_params=_params("arbitrary", "arbitrary"),
        name="ffn_up_conv_gate",
    )(h, wu, wg, cw, cb)


def _final_norm_kernel(x_ref, g_ref, o_ref):
    o_ref[...] = _rms(x_ref[...], g_ref[...])


def _final_norm(x, g):
    t, d = x.shape
    tm = _tile(t, 512, 8)
    return pl.pallas_call(
        _final_norm_kernel,
        grid=(t // tm,),
        in_specs=[pl.BlockSpec((tm, d), lambda i: (i, 0)), pl.BlockSpec((1, d), lambda i: (0, 0))],
        out_specs=pl.BlockSpec((tm, d), lambda i: (i, 0)),
        out_shape=jax.ShapeDtypeStruct((t, d), F32),
        compiler_params=_params("parallel"),
        name="final_norm",
    )(x, g)


def _rope_tables(seq, d):
    pos = jnp.arange(seq, dtype=F32)
    inv_freq = ROPE_THETA ** (-jnp.arange(0, d, 2, dtype=F32) / d)
    ang = pos[:, None] * inv_freq[None, :]
    return jnp.cos(ang), jnp.sin(ang)


def _retention_tables(blk):
    log_gamma = jnp.log(1.0 - 2.0 ** (-5.0 - jnp.arange(RET_HEADS, dtype=F32)))
    idx = jnp.arange(blk, dtype=F32)
    chunk = jnp.arange(blk) // CHUNK
    dist = jnp.abs(idx[:, None] - idx[None, :])
    dec = jnp.where(chunk[None, :] <= chunk[:, None],
                    jnp.exp(log_gamma[:, None, None] * dist[None]), 0.0)
    sin_state = jnp.exp(log_gamma[:, None] * (blk - 1 - idx)[None, :])
    cross = jnp.exp(log_gamma[:, None] * (idx + 1.0)[None, :])
    bcast = lambda a: jnp.broadcast_to(a[:, :, None], (RET_HEADS, blk, RET_DK))
    block_decay = tuple(math.exp(math.log(1.0 - 2.0 ** (-5.0 - h)) * blk) for h in range(RET_HEADS))
    return dec, bcast(sin_state), bcast(cross), block_decay


def _rotate_half_cols(w):
    half = w.shape[-1] // 2
    return jnp.concatenate([-w[..., half:], w[..., :half]], axis=-1)


def kernel(x, norm1_g, w_in, mla_q_norm_g, mla_kv_norm_g, mla_w_uq, mla_w_ukv, fox_b_f,
           w_br_fox, w_br_mla, w_br_ret, w_out, norm2_g, ffn_w_up, ffn_w_gate,
           ffn_conv_w, ffn_conv_b, ffn_w_down, final_norm_g):
    batch, seq, d = x.shape
    depth = w_in.shape[0]
    t = batch * seq
    q_lora = mla_w_uq.shape[1]
    kv_lora = mla_w_ukv.shape[1]

    widths = (FOX_W, FOX_W, FOX_W, FOX_HEADS, q_lora, kv_lora, MLA_ROPE,
              RET_QK_W, RET_QK_W, RET_V_W, RET_V_W, N_BRANCH * d)
    offs = [0]
    for wd in widths:
        offs.append(offs[-1] + wd)
    run_t = lambda k0, k1: jnp.swapaxes(w_in[:, :, offs[k0]:offs[k1]], 1, 2).astype(BF16)
    zeros = lambda n: jnp.zeros((depth, n, d), BF16)
    small = run_t(3, 7)
    cut = [0, FOX_HEADS, FOX_HEADS + q_lora, FOX_HEADS + q_lora + kv_lora]
    ff_t, mq_t, mkv_t, rope_t = (small[:, a:b] for a, b in zip(cut, cut[1:] + [small.shape[1]]))
    mla_rows = jnp.concatenate(
        [mkv_t, mq_t, rope_t, ff_t, zeros(LANES - MLA_ROPE - FOX_HEADS),
         -rope_t[:, MLA_ROPE // 2:], rope_t[:, :MLA_ROPE // 2], zeros(LANES - MLA_ROPE)], axis=1)
    assert OFF_MKV + mla_rows.shape[1] == P_WIDTH and q_lora == 512 and kv_lora == 256
    w_p = jnp.zeros((depth, P_WIDTH, d), BF16)
    w_p = lax.dynamic_update_slice(w_p, run_t(7, 11), (0, OFF_RQ, 0))
    w_p = lax.dynamic_update_slice(w_p, run_t(0, 3), (0, OFF_FQ, 0))
    w_p = lax.dynamic_update_slice(w_p, mla_rows, (0, OFF_MKV, 0))
    w_gates = run_t(11, 12)
    b_ff = jnp.pad(fox_b_f, ((0, 0), (FF_LANE, LANES - FF_LANE - FOX_HEADS)))[:, None, :]

    uq = mla_w_uq.reshape(depth, q_lora, MLA_HEADS, MLA_NOPE + MLA_ROPE)
    uq_rope = uq[..., MLA_NOPE:]
    pad_q = jnp.zeros((depth, q_lora, MLA_HEADS, MLA_QK_PAD - MLA_NOPE - MLA_ROPE), uq.dtype)
    w_q = jnp.concatenate([uq, pad_q], axis=-1).reshape(depth, q_lora, -1).astype(BF16)
    pad_r = jnp.zeros((depth, q_lora, MLA_HEADS, LANES - MLA_ROPE), uq.dtype)
    w_qr = jnp.concatenate([_rotate_half_cols(uq_rope), pad_r], axis=-1).reshape(depth, q_lora, -1).astype(BF16)
    ukv = mla_w_ukv.reshape(depth, kv_lora, MLA_HEADS, MLA_NOPE + MLA_V)
    w_k = ukv[..., :MLA_NOPE].reshape(depth, kv_lora, -1).astype(BF16)
    w_v = ukv[..., MLA_NOPE:].reshape(depth, kv_lora, -1).astype(BF16)

    w_bf, w_bm, w_br = w_br_fox.astype(BF16), w_br_mla.astype(BF16), w_br_ret.astype(BF16)
    w_o = w_out.astype(BF16)
    w_up, w_gate, w_down = ffn_w_up.astype(BF16), ffn_w_gate.astype(BF16), ffn_w_down.astype(BF16)
    conv_w = jnp.pad(ffn_conv_w, ((0, 0), (0, 8 - CONV_W), (0, 0)))
    conv_b = ffn_conv_b[:, None, :]
    g1, g2 = norm1_g[:, None, :], norm2_g[:, None, :]
    gq, gkv = mla_q_norm_g[:, None, :], mla_kv_norm_g[:, None, :]

    cos64, sin64 = _rope_tables(seq, MLA_ROPE)
    zpad = jnp.zeros((seq, LANES - MLA_ROPE), F32)
    mla_cos = jnp.concatenate([cos64, cos64, zpad], axis=1)
    mla_sin = jnp.concatenate([sin64, sin64, zpad], axis=1)
    cos128, sin128 = _rope_tables(seq, RET_DK)
    ret_cos = jnp.concatenate([cos128, cos128], axis=1)
    ret_sin = jnp.concatenate([-sin128, sin128], axis=1)
    ret_blk = _tile(seq, 512, CHUNK)
    dec, sin_state, cross, block_decay = _retention_tables(ret_blk)

    att_blk = _tile(seq, 512, CHUNK)
    row = jnp.arange(att_blk, dtype=jnp.int32)[:, None]
    col = jnp.arange(att_blk, dtype=jnp.int32)[None, :]
    rel_frame = row - col
    rel_chunk = (row // CHUNK) * CHUNK + (CHUNK - 1) - col

    xf = x.reshape(t, d)
    for i in range(depth):
        p, f_logit, h1 = _in_proj(xf, g1, w_p, i)
        kbias = _fox_decay(f_logit, b_ff, i, batch, seq)
        a = _attention(p, OFF_FQ // FOX_W, p, OFF_FK // FOX_W, p, OFF_FV // FOX_W, rel_frame, kbias,
                       batch=batch, seq=seq, heads=FOX_HEADS, dq=FOX_DH, dv=FOX_DH,
                       scale=FOX_DH ** -0.5, name="fox_attention")
        qm, km, vm = _mla_prep(p, gq, gkv, w_q, w_qr, w_k, w_v, mla_cos, mla_sin, i, seq)
        bm = _attention(qm, 0, km, 0, vm, 0, rel_chunk, None,
                        batch=batch, seq=seq, heads=MLA_HEADS, dq=MLA_QK_PAD, dv=MLA_V,
                        scale=(MLA_NOPE + MLA_ROPE) ** -0.5, name="mla_attention")
        cr = _retention(p, ret_cos, ret_sin, dec, sin_state, cross, block_decay, batch, seq)
        merged = _gated_merge(h1, a, bm, cr, w_gates, w_bf, w_bm, w_br, i)
        xf, h2 = _out_proj(merged, w_o, xf, g2, i)
        act = _ffn_up(h2, w_up, w_gate, conv_w, conv_b, i, seq)
        xf = _matmul_residual(act, w_down, xf, i, 512, 1024, "ffn_down")
    return _final_norm(xf, final_norm_g[None, :]).reshape(batch, seq, d)
```
